```python
import math
import jax
import jax.numpy as jnp
from jax import lax
import numpy as np

D_MODEL = 2048
BATCH = 2
SEQ = 4096
DEPTH = 2

GRID_W = 64
CTX_LEN = 256
F32 = jnp.float32

HY_WIDTH = D_MODEL // 4
LRU_WIDTH = D_MODEL // 4
ATT_WIDTH = D_MODEL // 2
D_MIX = HY_WIDTH + LRU_WIDTH + ATT_WIDTH

HY_GROUPS = 4
HY_ORDER = 2
HY_CONV = 3
HY_BANDS = 8
HY_EMB = 1 + 2 * HY_BANDS
HY_FILTER_HIDDEN = 64
HY_TARGET = 1e-2
HY_SHORT_DECAY_PCT = 0.3
HY_LONG_DECAY_PCT = 1.5
HY_MAX_DECAY = math.log(HY_TARGET) / HY_SHORT_DECAY_PCT
HY_MIN_DECAY = math.log(HY_TARGET) / HY_LONG_DECAY_PCT
HY_IN = (HY_ORDER + 1) * HY_WIDTH

LRU_HEADS = 8
LRU_HEAD_DIM = LRU_WIDTH // LRU_HEADS
LRU_CONV = 4
LRU_C = 8.0
LRU_IN = 2 * LRU_WIDTH

ATT_HEADS = 4
ATT_V_DIM = ATT_WIDTH // ATT_HEADS
ATT_HEAD_DIM = ATT_V_DIM // 2
ATT_QK = ATT_HEADS * 2 * ATT_HEAD_DIM
ATT_IN = 2 * ATT_QK + ATT_WIDTH
Q_BLOCK = 128
ROPE_THETA = 10000.0
ROPE_AXIS_DIM = ATT_HEAD_DIM // 2

D_IN = HY_IN + LRU_IN + ATT_IN

N_EXPERTS = 64
TOP_K = 8
D_EXPERT = 512
D_SHARED = 512
ROUTED_SCALE = 2.5
EXPERT_BLOCK = 128

LN_EPS = 1e-5
DEEPNORM_ALPHA = (2 * DEPTH) ** 0.25
DEEPNORM_BETA = (8 * DEPTH) ** -0.25

kernel_name = 'hybrid_hyena_rglru_diffattn_moe_dit'


def layer_norm(x, gain=None, bias=None):
    xf = x.astype(F32)
    mu = jnp.mean(xf, axis=-1, keepdims=True)
    var = jnp.mean(jnp.square(xf - mu), axis=-1, keepdims=True)
    y = (xf - mu) * lax.rsqrt(var + LN_EPS)
    if gain is not None:
        y = y * gain + bias
    return y.astype(x.dtype)


def modulate(x, shift, scale):
    return layer_norm(x) * (1.0 + scale) + shift


def depthwise_conv(x, w, b, pad_left, pad_right):
    y = lax.conv_general_dilated(
        x, w[:, None, :].astype(x.dtype), window_strides=(1,),
        padding=[(pad_left, pad_right)], dimension_numbers=('NWC', 'WIO', 'NWC'),
        feature_group_count=x.shape[-1])
    return y + b


def split_groups(u):
    return u[..., :HY_IN], u[..., HY_IN:HY_IN + LRU_IN], u[..., HY_IN + LRU_IN:]


def hyena_filter_spectrum(n, w1, b1, w2, b2, w3, freq):
    pos = jnp.arange(n, dtype=F32)
    t = (pos / max(n - 1, 1))[:, None]
    bands = jnp.linspace(1e-4, HY_BANDS - 1, HY_BANDS, dtype=F32)
    ang = (2.0 * math.pi / n) * pos[:, None] * bands
    feats = jnp.concatenate([t, jnp.cos(ang), -jnp.sin(ang)], axis=-1)
    hid = jnp.sin(freq * (feats @ w1 + b1))
    hid = jnp.sin(freq * (hid @ w2 + b2))
    filt = (hid @ w3).astype(F32).reshape(n, 2, HY_ORDER, HY_WIDTH)
    decay = jnp.abs(jnp.linspace(HY_MIN_DECAY, HY_MAX_DECAY, HY_WIDTH, dtype=F32))
    filt = filt * jnp.exp(-t * decay)[:, None, None, :]
    fwd, bwd = filt[:, 0], filt[:, 1]
    two_sided = jnp.concatenate([fwd, jnp.zeros_like(fwd[:1]), jnp.flip(bwd[1:], axis=0)], axis=0)
    two_sided = two_sided / jnp.sum(jnp.abs(two_sided), axis=0, keepdims=True)
    return jnp.fft.rfft(two_sided, axis=0)


def hyena_mixer(u, conv_w, conv_b, w1, b1, w2, b2, w3, freq, skip):
    n = u.shape[1]
    u = depthwise_conv(u, conv_w, conv_b, HY_CONV // 2, HY_CONV - 1 - HY_CONV // 2).astype(F32)
    gates = (u[..., :HY_WIDTH], u[..., HY_WIDTH:2 * HY_WIDTH])
    z = u[..., 2 * HY_WIDTH:]
    spec = hyena_filter_spectrum(n, w1, b1, w2, b2, w3, freq)
    for o in range(HY_ORDER):
        zc = jnp.fft.irfft(jnp.fft.rfft(z, n=2 * n, axis=1) * spec[None, :, o], n=2 * n, axis=1)[:, :n]
        z = gates[o] * (zc + skip[o] * z)
    return z


def rglru_coeffs(xr, wa, ba, wx, bx, lam):
    bsz, n, _ = xr.shape
    xf = xr.astype(F32)
    xh = xf.reshape(bsz, n, LRU_HEADS, LRU_HEAD_DIM)
    r = jax.nn.sigmoid(jnp.einsum('blhi,hij->blhj', xh, wa).reshape(bsz, n, LRU_WIDTH) + ba)
    i = jax.nn.sigmoid(jnp.einsum('blhi,hij->blhj', xh, wx).reshape(bsz, n, LRU_WIDTH) + bx)
    log_a = -LRU_C * r * jax.nn.softplus(-lam)
    a = jnp.exp(log_a)
    b = jnp.sqrt(-jnp.expm1(2.0 * log_a)) * (i * xf)
    return a, b


def _affine_combine(left, right):
    a_l, b_l = left
    a_r, b_r = right
    return a_l * a_r, a_r * b_l + b_r


def linear_scan(a, b, h0, reverse):
    if reverse:
        a, b = jnp.flip(a, axis=1), jnp.flip(b, axis=1)
    a_cum, b_cum = lax.associative_scan(_affine_combine, (a, b), axis=1)
    h = a_cum * h0[:, None, :] + b_cum
    return jnp.flip(h, axis=1) if reverse else h


def rglru_mixer(u_lat, u_ctx, conv_w, conv_b, wa, ba, wx, bx, lam, need_ctx):
    def recurrent_input(u):
        return depthwise_conv(u[..., LRU_WIDTH:], conv_w, conv_b, LRU_CONV // 2, LRU_CONV - 1 - LRU_CONV // 2)
    xr_lat, xr_ctx = recurrent_input(u_lat), recurrent_input(u_ctx)
    h_lat = jnp.zeros(xr_lat.shape, F32)
    ctx_states = []
    for d, reverse in enumerate((False, True)):
        a_c, b_c = rglru_coeffs(xr_ctx, wa[d], ba[d], wx[d], bx[d], lam[d])
        hs_c = linear_scan(a_c, b_c, jnp.zeros_like(a_c[:, 0]), reverse)
        h_end = hs_c[:, 0] if reverse else hs_c[:, -1]
        a_l, b_l = rglru_coeffs(xr_lat, wa[d], ba[d], wx[d], bx[d], lam[d])
        h_lat = h_lat + linear_scan(a_l, b_l, h_end, reverse)
        ctx_states.append(hs_c)
    y_lat = h_lat * jax.nn.gelu(u_lat[..., :LRU_WIDTH].astype(F32))
    y_ctx = None
    if need_ctx:
        y_ctx = (ctx_states[0] + ctx_states[1]) * jax.nn.gelu(u_ctx[..., :LRU_WIDTH].astype(F32))
    return y_lat, y_ctx


def axial_rope_tables(n):
    rows = n // GRID_W
    row = jnp.broadcast_to(jnp.arange(rows, dtype=F32)[:, None], (rows, GRID_W)).reshape(-1)
    col = jnp.broadcast_to(jnp.arange(GRID_W, dtype=F32)[None, :], (rows, GRID_W)).reshape(-1)
    inv = ROPE_THETA ** (-jnp.arange(0, ROPE_AXIS_DIM, 2, dtype=F32) / ROPE_AXIS_DIM)
    ang = jnp.stack([row[:, None] * inv, col[:, None] * inv], axis=1)
    return jnp.cos(ang), jnp.sin(ang)


def apply_axial_rope(x, cos, sin):
    xs = x.reshape(x.shape[:-1] + (2, 2, ROPE_AXIS_DIM // 2))
    x1, x2 = xs[..., 0, :], xs[..., 1, :]
    cb, sb = cos[None, :, None, None], sin[None, :, None, None]
    out = jnp.stack([x1 * cb - x2 * sb, x2 * cb + x1 * sb], axis=-2)
    return out.reshape(x.shape)


def diff_attend(q, k, v, lam):
    s = jnp.einsum('bqhmd,bkhmd->bhmqk', q, k).astype(F32) * (ATT_HEAD_DIM ** -0.5)
    p = jax.nn.softmax(s, axis=-1)
    w = p[:, :, 0] - lam * p[:, :, 1]
    return jnp.einsum('bhqk,bkhe->bqhe', w, v.astype(F32))


def head_rms_norm(o, gain, lam_init):
    of = o.astype(F32)
    y = of * lax.rsqrt(jnp.mean(of * of, axis=-1, keepdims=True) + LN_EPS) * gain * (1.0 - lam_init)
    return y.reshape(o.shape[0], o.shape[1], -1)


def diff_attention_mixer(u_lat, u_ctx, lam_vecs, subln_gain, lam_init, need_ctx):
    bsz, n_lat, _ = u_lat.shape

    def split_qkv(u):
        bb, n, _ = u.shape
        q = u[..., :ATT_QK].reshape(bb, n, ATT_HEADS, 2, ATT_HEAD_DIM)
        k = u[..., ATT_QK:2 * ATT_QK].reshape(bb, n, ATT_HEADS, 2, ATT_HEAD_DIM)
        v = u[..., 2 * ATT_QK:].reshape(bb, n, ATT_HEADS, ATT_V_DIM)
        return q, k, v

    q_l, k_l, v_l = split_qkv(u_lat)
    q_c, k_c, v_c = split_qkv(u_ctx)
    cos, sin = axial_rope_tables(n_lat)
    q_l = apply_axial_rope(q_l, cos, sin)
    k_l = apply_axial_rope(k_l, cos, sin)
    lv = lam_vecs.astype(F32)
    lam = jnp.exp(jnp.sum(lv[0] * lv[1])) - jnp.exp(jnp.sum(lv[2] * lv[3])) + lam_init
    k_all = jnp.concatenate([k_c, k_l], axis=1)
    v_all = jnp.concatenate([v_c, v_l], axis=1)
    n_blk = n_lat // Q_BLOCK
    q_blocks = jnp.moveaxis(q_l.reshape(bsz, n_blk, Q_BLOCK, ATT_HEADS, 2, ATT_HEAD_DIM), 1, 0)
    o = lax.map(lambda qb: diff_attend(qb, k_all, v_all, lam), q_blocks)
    o_lat = jnp.moveaxis(o, 0, 1).reshape(bsz, n_lat, ATT_HEADS, ATT_V_DIM)
    y_lat = head_rms_norm(o_lat, subln_gain, lam_init)
    y_ctx = head_rms_norm(diff_attend(q_c, k_c, v_c, lam), subln_gain, lam_init) if need_ctx else None
    return y_lat, y_ctx


def moe_ffn(h, router_w, router_b, w_gate, w_up, w_down, sh_gate, sh_up, sh_down):
    n_tok, d = h.shape
    scores = jax.nn.sigmoid((h @ router_w).astype(F32))
    _, top_idx = lax.top_k(scores + router_b.astype(F32), TOP_K)
    top_s = jnp.take_along_axis(scores, top_idx, axis=1)
    top_w = top_s / jnp.sum(top_s, axis=-1, keepdims=True) * ROUTED_SCALE
    n_assign = n_tok * TOP_K
    flat_e = top_idx.reshape(-1)
    order = jnp.argsort(flat_e)
    sorted_e = flat_e[order]
    counts = jnp.bincount(flat_e, length=N_EXPERTS)
    padded = (counts + EXPERT_BLOCK - 1) // EXPERT_BLOCK * EXPERT_BLOCK
    padded_end = jnp.cumsum(padded)
    start = jnp.cumsum(counts) - counts
    dest = (padded_end - padded)[sorted_e] + jnp.arange(n_assign) - start[sorted_e]
    n_blocks = -(-n_assign // EXPERT_BLOCK) + N_EXPERTS
    n_rows = n_blocks * EXPERT_BLOCK
    row_tok = jnp.zeros((n_rows,), jnp.int32).at[dest].set((order // TOP_K).astype(jnp.int32))
    row_w = jnp.zeros((n_rows,), F32).at[dest].set(top_w.reshape(-1)[order])
    block_e = jnp.minimum(
        jnp.searchsorted(padded_end, jnp.arange(n_blocks) * EXPERT_BLOCK, side='right'), N_EXPERTS - 1)

    def expert_block(args):
        tok, wgt, e = args
        xb = h[tok]
        act = jax.nn.silu(xb @ w_gate[e]) * (xb @ w_up[e])
        return (act @ w_down[e]).astype(F32) * wgt[:, None]

    y = lax.map(expert_block, (row_tok.reshape(n_blocks, EXPERT_BLOCK),
                               row_w.reshape(n_blocks, EXPERT_BLOCK), block_e))
    routed = jnp.zeros((n_tok, d), F32).at[row_tok].add(y.reshape(n_rows, d))
    shared = (jax.nn.silu(h @ sh_gate) * (h @ sh_up)) @ sh_down
    return routed + shared


def setup_inputs(seed: int = 0) -> dict:
    key = jax.random.key(seed)
    ks = iter(jax.random.split(key, 48))

    def nrm(shape, scale):
        return jax.random.normal(next(ks), shape, F32) * scale

    L, D = DEPTH, D_MODEL
    a_base = jax.random.uniform(next(ks), (L, 2, LRU_WIDTH), F32, 0.9, 0.999) ** (1.0 / LRU_C)
    return {
        'x': nrm((BATCH, SEQ, D), 1.0),
        'c': nrm((BATCH, D), 1.0),
        'ctx': nrm((BATCH, CTX_LEN, D), 1.0),
        'c_ctx': nrm((D,), 1.0),
        'ada_w': nrm((L, D, 6 * D), 0.5 * D ** -0.5),
        'ada_b': nrm((L, 6 * D), 0.02),
        'w_in': nrm((L, D, D_IN), D ** -0.5),
        'hy_conv_w': nrm((L, HY_CONV, HY_IN), HY_CONV ** -0.5),
        'hy_conv_b': nrm((L, HY_IN), 0.02),
        'hy_w1': nrm((L, HY_EMB, HY_FILTER_HIDDEN), HY_EMB ** -0.5),
        'hy_b1': nrm((L, HY_FILTER_HIDDEN), 0.1),
        'hy_w2': nrm((L, HY_FILTER_HIDDEN, HY_FILTER_HIDDEN), HY_FILTER_HIDDEN ** -0.5),
        'hy_b2': nrm((L, HY_FILTER_HIDDEN), 0.1),
        'hy_w3': nrm((L, HY_FILTER_HIDDEN, 2 * HY_ORDER * HY_WIDTH), HY_FILTER_HIDDEN ** -0.5),
        'hy_freq': 1.0 + nrm((L, HY_FILTER_HIDDEN), 0.1),
        'hy_skip': nrm((L, HY_ORDER, HY_WIDTH), 0.5),
        'lru_conv_w': nrm((L, LRU_CONV, LRU_WIDTH), LRU_CONV ** -0.5),
        'lru_conv_b': nrm((L, LRU_WIDTH), 0.02),
        'lru_wa': nrm((L, 2, LRU_HEADS, LRU_HEAD_DIM, LRU_HEAD_DIM), LRU_HEAD_DIM ** -0.5),
        'lru_ba': nrm((L, 2, LRU_WIDTH), 0.02),
        'lru_wx': nrm((L, 2, LRU_HEADS, LRU_HEAD_DIM, LRU_HEAD_DIM), LRU_HEAD_DIM ** -0.5),
        'lru_bx': nrm((L, 2, LRU_WIDTH), 0.02),
        'lru_lambda': jnp.log(a_base) - jnp.log1p(-a_base),
        'att_lambda': nrm((L, 4, ATT_HEAD_DIM), 0.1),
        'att_subln': 1.0 + nrm((L, ATT_V_DIM), 0.02),
        'w_out': nrm((L, D_MIX, D), DEEPNORM_BETA * D_MIX ** -0.5),
        'ln_g': 1.0 + nrm((L, 2, D), 0.02),
        'ln_b': nrm((L, 2, D), 0.02),
        'router_w': nrm((L, D, N_EXPERTS), D ** -0.5),
        'router_b': nrm((L, N_EXPERTS), 0.01),
        'exp_w_gate': nrm((L, N_EXPERTS, D, D_EXPERT), D ** -0.5),
        'exp_w_up': nrm((L, N_EXPERTS, D, D_EXPERT), D ** -0.5),
        'exp_w_down': nrm((L, N_EXPERTS, D_EXPERT, D), DEEPNORM_BETA * D_EXPERT ** -0.5),
        'sh_w_gate': nrm((L, D, D_SHARED), D ** -0.5),
        'sh_w_up': nrm((L, D, D_SHARED), D ** -0.5),
        'sh_w_down': nrm((L, D_SHARED, D), DEEPNORM_BETA * D_SHARED ** -0.5),
    }


def reference(x, c, ctx, c_ctx, ada_w, ada_b, w_in, hy_conv_w, hy_conv_b, hy_w1, hy_b1, hy_w2,
              hy_b2, hy_w3, hy_freq, hy_skip, lru_conv_w, lru_conv_b, lru_wa, lru_ba, lru_wx,
              lru_bx, lru_lambda, att_lambda, att_subln, w_out, ln_g, ln_b, router_w, router_b,
              exp_w_gate, exp_w_up, exp_w_down, sh_w_gate, sh_w_up, sh_w_down):
    x_lat, x_ctx = x, ctx
    for l in range(DEPTH):
        need_ctx = l < DEPTH - 1
        lam_init = 0.8 - 0.6 * math.exp(-0.3 * l)
        mod_lat = (jax.nn.silu(c) @ ada_w[l] + ada_b[l])[:, None, :]
        mod_ctx = (jax.nn.silu(c_ctx) @ ada_w[l] + ada_b[l])[None, None, :]
        sh1_l, sc1_l, g1_l, sh2_l, sc2_l, g2_l = jnp.split(mod_lat, 6, axis=-1)
        sh1_c, sc1_c, g1_c, sh2_c, sc2_c, g2_c = jnp.split(mod_ctx, 6, axis=-1)
        hy_args = (hy_conv_w[l], hy_conv_b[l], hy_w1[l], hy_b1[l], hy_w2[l], hy_b2[l],
                   hy_w3[l], hy_freq[l], hy_skip[l])

        hy_l, lru_l, att_l = split_groups(modulate(x_lat, sh1_l, sc1_l) @ w_in[l])
        hy_c, lru_c, att_c = split_groups(modulate(x_ctx, sh1_c, sc1_c) @ w_in[l])
        y_hy_l = hyena_mixer(hy_l, *hy_args)
        y_lru_l, y_lru_c = rglru_mixer(lru_l, lru_c, lru_conv_w[l], lru_conv_b[l], lru_wa[l], lru_ba[l],
                                       lru_wx[l], lru_bx[l], lru_lambda[l], need_ctx)
        y_att_l, y_att_c = diff_attention_mixer(att_l, att_c, att_lambda[l], att_subln[l], lam_init, need_ctx)
        mix_l = jnp.concatenate([y_hy_l, y_lru_l, y_att_l], axis=-1) @ w_out[l]
        x_lat = layer_norm(DEEPNORM_ALPHA * x_lat + g1_l * mix_l, ln_g[l, 0], ln_b[l, 0])
        if need_ctx:
            mix_c = jnp.concatenate([hyena_mixer(hy_c, *hy_args), y_lru_c, y_att_c], axis=-1) @ w_out[l]
            x_ctx = layer_norm(DEEPNORM_ALPHA * x_ctx + g1_c * mix_c, ln_g[l, 0], ln_b[l, 0])

        f_lat = modulate(x_lat, sh2_l, sc2_l).reshape(-1, D_MODEL)
        n_lat_tok = f_lat.shape[0]
        if need_ctx:
            tokens = jnp.concatenate([f_lat, modulate(x_ctx, sh2_c, sc2_c).reshape(-1, D_MODEL)], axis=0)
        else:
            tokens = f_lat
        ffn = moe_ffn(tokens, router_w[l], router_b[l], exp_w_gate[l], exp_w_up[l], exp_w_down[l],
                      sh_w_gate[l], sh_w_up[l], sh_w_down[l])
        x_lat = layer_norm(DEEPNORM_ALPHA * x_lat + g2_l * ffn[:n_lat_tok].reshape(x_lat.shape),
                           ln_g[l, 1], ln_b[l, 1])
        if need_ctx:
            x_ctx = layer_norm(DEEPNORM_ALPHA * x_ctx + g2_c * ffn[n_lat_tok:].reshape(x_ctx.shape),
                               ln_g[l, 1], ln_b[l, 1])
    return x_lat
```

```python
import math
from functools import partial

import jax
import jax.numpy as jnp
from jax import lax
from jax.experimental import pallas as pl
from jax.experimental.pallas import tpu as pltpu

D_MODEL = 2048
DEPTH = 2
GRID_W = 64
F32 = jnp.float32
BF16 = jnp.bfloat16

HY_WIDTH = D_MODEL // 4
LRU_WIDTH = D_MODEL // 4
ATT_WIDTH = D_MODEL // 2
HY_ORDER = 2
HY_CONV = 3
HY_BANDS = 8
HY_TARGET = 1e-2
HY_MAX_DECAY = math.log(HY_TARGET) / 0.3
HY_MIN_DECAY = math.log(HY_TARGET) / 1.5
HY_IN = (HY_ORDER + 1) * HY_WIDTH

LRU_HEADS = 8
LRU_HEAD_DIM = LRU_WIDTH // LRU_HEADS
LRU_CONV = 4
LRU_C = 8.0
LRU_IN = 2 * LRU_WIDTH

ATT_HEADS = 4
ATT_V_DIM = ATT_WIDTH // ATT_HEADS
ATT_HEAD_DIM = ATT_V_DIM // 2
ATT_QK = ATT_HEADS * 2 * ATT_HEAD_DIM
Q_BLOCK = 128
ROPE_THETA = 10000.0
ROPE_AXIS_DIM = ATT_HEAD_DIM // 2

N_EXPERTS = 64
TOP_K = 8
ROUTED_SCALE = 2.5
EXPERT_BLOCK = 128

LN_EPS = 1e-5
DEEPNORM_ALPHA = (2 * DEPTH) ** 0.25

VMEM_LIMIT_BYTES = 48 * 1024 * 1024


def _mm_body(a_ref, b_ref, o_ref):
    o_ref[...] = jnp.dot(a_ref[...], b_ref[...], preferred_element_type=F32)


def _matmul(a, b, tm=512, tn=512):
    m, k = a.shape
    n = b.shape[1]
    tm = min(tm, m)
    tn = min(tn, n)
    assert m % tm == 0 and n % tn == 0
    return pl.pallas_call(
        _mm_body,
        grid=(m // tm, n // tn),
        in_specs=[pl.BlockSpec((tm, k), lambda i, j: (i, 0)),
                  pl.BlockSpec((k, tn), lambda i, j: (0, j))],
        out_specs=pl.BlockSpec((tm, tn), lambda i, j: (i, j)),
        out_shape=jax.ShapeDtypeStruct((m, n), F32),
        compiler_params=pltpu.CompilerParams(
            dimension_semantics=("parallel", "arbitrary"),
            vmem_limit_bytes=VMEM_LIMIT_BYTES),
    )(a, b)


def _proj(x, w):
    return _matmul(x.astype(BF16), w.astype(BF16))


def _layer_norm(x, gain=None, bias=None):
    mu = jnp.mean(x, axis=-1, keepdims=True)
    var = jnp.mean(jnp.square(x - mu), axis=-1, keepdims=True)
    y = (x - mu) * lax.rsqrt(var + LN_EPS)
    if gain is not None:
        y = y * gain + bias
    return y


def _modulate(x, shift, scale):
    return _layer_norm(x) * (1.0 + scale) + shift


def _depthwise_conv(x, w, b, pad_left, pad_right):
    y = lax.conv_general_dilated(
        x, w[:, None, :], window_strides=(1,),
        padding=[(pad_left, pad_right)], dimension_numbers=('NWC', 'WIO', 'NWC'),
        feature_group_count=x.shape[-1])
    return y + b


def _hyena_spectrum(n, w1, b1, w2, b2, w3, freq):
    pos = jnp.arange(n, dtype=F32)
    t = (pos / max(n - 1, 1))[:, None]
    bands = jnp.linspace(1e-4, HY_BANDS - 1, HY_BANDS, dtype=F32)
    ang = (2.0 * math.pi / n) * pos[:, None] * bands
    feats = jnp.concatenate([t, jnp.cos(ang), -jnp.sin(ang)], axis=-1)
    hp = lax.Precision.HIGHEST
    hid = jnp.sin(freq * (jnp.dot(feats, w1, precision=hp) + b1))
    hid = jnp.sin(freq * (jnp.dot(hid, w2, precision=hp) + b2))
    filt = jnp.dot(hid, w3, precision=hp).reshape(n, 2, HY_ORDER, HY_WIDTH)
    decay = jnp.abs(jnp.linspace(HY_MIN_DECAY, HY_MAX_DECAY, HY_WIDTH, dtype=F32))
    filt = filt * jnp.exp(-t * decay)[:, None, None, :]
    fwd, bwd = filt[:, 0], filt[:, 1]
    two_sided = jnp.concatenate([fwd, jnp.zeros_like(fwd[:1]), jnp.flip(bwd[1:], axis=0)], axis=0)
    two_sided = two_sided / jnp.sum(jnp.abs(two_sided), axis=0, keepdims=True)
    return jnp.fft.rfft(two_sided, axis=0)


def _hyena_mixer(u, conv_w, conv_b, w1, b1, w2, b2, w3, freq, skip):
    n = u.shape[1]
    u = _depthwise_conv(u, conv_w, conv_b, HY_CONV // 2, HY_CONV - 1 - HY_CONV // 2)
    gates = (u[..., :HY_WIDTH], u[..., HY_WIDTH:2 * HY_WIDTH])
    z = u[..., 2 * HY_WIDTH:]
    spec = _hyena_spectrum(n, w1, b1, w2, b2, w3, freq)
    for o in range(HY_ORDER):
        zc = jnp.fft.irfft(jnp.fft.rfft(z, n=2 * n, axis=1) * spec[None, :, o], n=2 * n, axis=1)[:, :n]
        z = gates[o] * (zc + skip[o] * z)
    return z


def _rglru_coeffs(xr, wa, ba, wx, bx, lam):
    bsz, n, _ = xr.shape
    xh = xr.reshape(bsz, n, LRU_HEADS, LRU_HEAD_DIM)
    r = jax.nn.sigmoid(jnp.einsum('blhi,hij->blhj', xh, wa).reshape(bsz, n, LRU_WIDTH) + ba)
    i = jax.nn.sigmoid(jnp.einsum('blhi,hij->blhj', xh, wx).reshape(bsz, n, LRU_WIDTH) + bx)
    log_a = -LRU_C * r * jax.nn.softplus(-lam)
    a = jnp.exp(log_a)
    b = jnp.sqrt(-jnp.expm1(2.0 * log_a)) * (i * xr)
    return a, b


def _affine_combine(left, right):
    a_l, b_l = left
    a_r, b_r = right
    return a_l * a_r, a_r * b_l + b_r


def _linear_scan(a, b, h0, reverse):
    if reverse:
        a, b = jnp.flip(a, axis=1), jnp.flip(b, axis=1)
    a_cum, b_cum = lax.associative_scan(_affine_combine, (a, b), axis=1)
    h = a_cum * h0[:, None, :] + b_cum
    return jnp.flip(h, axis=1) if reverse else h


def _rglru_mixer(u_lat, u_ctx, conv_w, conv_b, wa, ba, wx, bx, lam, need_ctx):
    def recurrent_input(u):
        return _depthwise_conv(u[..., LRU_WIDTH:], conv_w, conv_b, LRU_CONV // 2, LRU_CONV - 1 - LRU_CONV // 2)
    xr_lat, xr_ctx = recurrent_input(u_lat), recurrent_input(u_ctx)
    h_lat = jnp.zeros(xr_lat.shape, F32)
    ctx_states = []
    for d, reverse in enumerate((False, True)):
        a_c, b_c = _rglru_coeffs(xr_ctx, wa[d], ba[d], wx[d], bx[d], lam[d])
        hs_c = _linear_scan(a_c, b_c, jnp.zeros_like(a_c[:, 0]), reverse)
        h_end = hs_c[:, 0] if reverse else hs_c[:, -1]
        a_l, b_l = _rglru_coeffs(xr_lat, wa[d], ba[d], wx[d], bx[d], lam[d])
        h_lat = h_lat + _linear_scan(a_l, b_l, h_end, reverse)
        ctx_states.append(hs_c)
    y_lat = h_lat * jax.nn.gelu(u_lat[..., :LRU_WIDTH])
    y_ctx = None
    if need_ctx:
        y_ctx = (ctx_states[0] + ctx_states[1]) * jax.nn.gelu(u_ctx[..., :LRU_WIDTH])
    return y_lat, y_ctx


def _axial_rope_tables(n):
    rows = n // GRID_W
    row = jnp.broadcast_to(jnp.arange(rows, dtype=F32)[:, None], (rows, GRID_W)).reshape(-1)
    col = jnp.broadcast_to(jnp.arange(GRID_W, dtype=F32)[None, :], (rows, GRID_W)).reshape(-1)
    inv = ROPE_THETA ** (-jnp.arange(0, ROPE_AXIS_DIM, 2, dtype=F32) / ROPE_AXIS_DIM)
    ang = jnp.stack([row[:, None] * inv, col[:, None] * inv], axis=1)
    return jnp.cos(ang), jnp.sin(ang)


def _apply_axial_rope(x, cos, sin):
    xs = x.reshape(x.shape[:-1] + (2, 2, ROPE_AXIS_DIM // 2))
    x1, x2 = xs[..., 0, :], xs[..., 1, :]
    cb, sb = cos[None, :, None, None], sin[None, :, None, None]
    out = jnp.stack([x1 * cb - x2 * sb, x2 * cb + x1 * sb], axis=-2)
    return out.reshape(x.shape)


def _diff_attend(q, k, v, lam):
    s = jnp.einsum('bqhmd,bkhmd->bhmqk', q, k) * (ATT_HEAD_DIM ** -0.5)
    p = jax.nn.softmax(s, axis=-1)
    w = p[:, :, 0] - lam * p[:, :, 1]
    return jnp.einsum('bhqk,bkhe->bqhe', w, v)


def _head_rms_norm(o, gain, lam_init):
    y = o * lax.rsqrt(jnp.mean(o * o, axis=-1, keepdims=True) + LN_EPS) * gain * (1.0 - lam_init)
    return y.reshape(o.shape[0], o.shape[1], -1)


def _diff_attention_mixer(u_lat, u_ctx, lam_vecs, subln_gain, lam_init, need_ctx):
    bsz, n_lat, _ = u_lat.shape

    def split_qkv(u):
        bb, n, _ = u.shape
        q = u[..., :ATT_QK].reshape(bb, n, ATT_HEADS, 2, ATT_HEAD_DIM)
        k = u[..., ATT_QK:2 * ATT_QK].reshape(bb, n, ATT_HEADS, 2, ATT_HEAD_DIM)
        v = u[..., 2 * ATT_QK:].reshape(bb, n, ATT_HEADS, ATT_V_DIM)
        return q, k, v

    q_l, k_l, v_l = split_qkv(u_lat)
    q_c, k_c, v_c = split_qkv(u_ctx)
    cos, sin = _axial_rope_tables(n_lat)
    q_l = _apply_axial_rope(q_l, cos, sin)
    k_l = _apply_axial_rope(k_l, cos, sin)
    lv = lam_vecs
    lam = jnp.exp(jnp.sum(lv[0] * lv[1])) - jnp.exp(jnp.sum(lv[2] * lv[3])) + lam_init
    k_all = jnp.concatenate([k_c, k_l], axis=1)
    v_all = jnp.concatenate([v_c, v_l], axis=1)
    n_blk = n_lat // Q_BLOCK
    q_blocks = jnp.moveaxis(q_l.reshape(bsz, n_blk, Q_BLOCK, ATT_HEADS, 2, ATT_HEAD_DIM), 1, 0)
    o = lax.map(lambda qb: _diff_attend(qb, k_all, v_all, lam), q_blocks)
    o_lat = jnp.moveaxis(o, 0, 1).reshape(bsz, n_lat, ATT_HEADS, ATT_V_DIM)
    y_lat = _head_rms_norm(o_lat, subln_gain, lam_init)
    y_ctx = _head_rms_norm(_diff_attend(q_c, k_c, v_c, lam), subln_gain, lam_init) if need_ctx else None
    return y_lat, y_ctx


def _moe_ffn(h, router_w, router_b, w_gate, w_up, w_down, sh_gate, sh_up, sh_down):
    n_tok, d = h.shape
    scores = jax.nn.sigmoid(jnp.dot(h, router_w, precision=lax.Precision.HIGHEST))
    _, top_idx = lax.top_k(scores + router_b, TOP_K)
    top_s = jnp.take_along_axis(scores, top_idx, axis=1)
    top_w = top_s / jnp.sum(top_s, axis=-1, keepdims=True) * ROUTED_SCALE
    n_assign = n_tok * TOP_K
    flat_e = top_idx.reshape(-1)
    order = jnp.argsort(flat_e)
    sorted_e = flat_e[order]
    counts = jnp.bincount(flat_e, length=N_EXPERTS)
    padded = (counts + EXPERT_BLOCK - 1) // EXPERT_BLOCK * EXPERT_BLOCK
    padded_end = jnp.cumsum(padded)
    start = jnp.cumsum(counts) - counts
    dest = (padded_end - padded)[sorted_e] + jnp.arange(n_assign) - start[sorted_e]
    n_blocks = -(-n_assign // EXPERT_BLOCK) + N_EXPERTS
    n_rows = n_blocks * EXPERT_BLOCK
    row_tok = jnp.zeros((n_rows,), jnp.int32).at[dest].set((order // TOP_K).astype(jnp.int32))
    row_w = jnp.zeros((n_rows,), F32).at[dest].set(top_w.reshape(-1)[order])
    block_e = jnp.minimum(
        jnp.searchsorted(padded_end, jnp.arange(n_blocks) * EXPERT_BLOCK, side='right'), N_EXPERTS - 1)

    def expert_block(args):
        tok, wgt, e = args
        xb = h[tok]
        act = jax.nn.silu(xb @ w_gate[e]) * (xb @ w_up[e])
        return (act @ w_down[e]) * wgt[:, None]

    y = lax.map(expert_block, (row_tok.reshape(n_blocks, EXPERT_BLOCK),
                               row_w.reshape(n_blocks, EXPERT_BLOCK), block_e))
    routed = jnp.zeros((n_tok, d), F32).at[row_tok].add(y.reshape(n_rows, d))
    hb = h.astype(BF16)
    gu = _matmul(hb, jnp.concatenate([sh_gate, sh_up], axis=1).astype(BF16))
    d_sh = sh_gate.shape[1]
    act = jax.nn.silu(gu[:, :d_sh]) * gu[:, d_sh:]
    shared = _proj(act, sh_down)
    return routed + shared


def kernel(x, c, ctx, c_ctx, ada_w, ada_b, w_in, hy_conv_w, hy_conv_b, hy_w1, hy_b1, hy_w2, hy_b2, hy_w3, hy_freq, hy_skip, lru_conv_w, lru_conv_b, lru_wa, lru_ba, lru_wx, lru_bx, lru_lambda, att_lambda, att_subln, w_out, ln_g, ln_b, router_w, router_b, exp_w_gate, exp_w_up, exp_w_down, sh_w_gate, sh_w_up, sh_w_down):
    x_lat, x_ctx = x, ctx
    bsz, n_lat, d = x.shape
    n_ctx = ctx.shape[1]
    for l in range(DEPTH):
        need_ctx = l < DEPTH - 1
        lam_init = 0.8 - 0.6 * math.exp(-0.3 * l)
        cond = jnp.concatenate([c, c_ctx[None, :]], axis=0)
        mod = jnp.dot(jax.nn.silu(cond), ada_w[l], precision=lax.Precision.HIGHEST) + ada_b[l]
        mod_lat = mod[:bsz, None, :]
        mod_ctx = mod[bsz:, None, :]
        sh1_l, sc1_l, g1_l, sh2_l, sc2_l, g2_l = jnp.split(mod_lat, 6, axis=-1)
        sh1_c, sc1_c, g1_c, sh2_c, sc2_c, g2_c = jnp.split(mod_ctx, 6, axis=-1)
        hy_args = (hy_conv_w[l], hy_conv_b[l], hy_w1[l], hy_b1[l], hy_w2[l], hy_b2[l],
                   hy_w3[l], hy_freq[l], hy_skip[l])

        h_all = jnp.concatenate([_modulate(x_lat, sh1_l, sc1_l).reshape(-1, d),
                                 _modulate(x_ctx, sh1_c, sc1_c).reshape(-1, d)], axis=0)
        u_all = _proj(h_all, w_in[l])
        u_lat = u_all[:bsz * n_lat].reshape(bsz, n_lat, -1)
        u_ctx = u_all[bsz * n_lat:].reshape(bsz, n_ctx, -1)
        hy_l, lru_l, att_l = u_lat[..., :HY_IN], u_lat[..., HY_IN:HY_IN + LRU_IN], u_lat[..., HY_IN + LRU_IN:]
        hy_c, lru_c, att_c = u_ctx[..., :HY_IN], u_ctx[..., HY_IN:HY_IN + LRU_IN], u_ctx[..., HY_IN + LRU_IN:]
        y_hy_l = _hyena_mixer(hy_l, *hy_args)
        y_lru_l, y_lru_c = _rglru_mixer(lru_l, lru_c, lru_conv_w[l], lru_conv_b[l], lru_wa[l], lru_ba[l],
                                        lru_wx[l], lru_bx[l], lru_lambda[l], need_ctx)
        y_att_l, y_att_c = _diff_attention_mixer(att_l, att_c, att_lambda[l], att_subln[l], lam_init, need_ctx)
        mixed = jnp.concatenate([y_hy_l, y_lru_l, y_att_l], axis=-1).reshape(-1, d)
        if need_ctx:
            mixed_c = jnp.concatenate([_hyena_mixer(hy_c, *hy_args), y_lru_c, y_att_c], axis=-1).reshape(-1, d)
            mixed = jnp.concatenate([mixed, mixed_c], axis=0)
        mix = _proj(mixed, w_out[l])
        mix_l = mix[:bsz * n_lat].reshape(bsz, n_lat, d)
        x_lat = _layer_norm(DEEPNORM_ALPHA * x_lat + g1_l * mix_l, ln_g[l, 0], ln_b[l, 0])
        if need_ctx:
            mix_c = mix[bsz * n_lat:].reshape(bsz, n_ctx, d)
            x_ctx = _layer_norm(DEEPNORM_ALPHA * x_ctx + g1_c * mix_c, ln_g[l, 0], ln_b[l, 0])

        f_lat = _modulate(x_lat, sh2_l, sc2_l).reshape(-1, d)
        n_lat_tok = f_lat.shape[0]
        if need_ctx:
            tokens = jnp.concatenate([f_lat, _modulate(x_ctx, sh2_c, sc2_c).reshape(-1, d)], axis=0)
        else:
            tokens = f_lat
        ffn = _moe_ffn(tokens, router_w[l], router_b[l], exp_w_gate[l], exp_w_up[l], exp_w_down[l],
                       sh_w_gate[l], sh_w_up[l], sh_w_down[l])
        x_lat = _layer_norm(DEEPNORM_ALPHA * x_lat + g2_l * ffn[:n_lat_tok].reshape(x_lat.shape),
                            ln_g[l, 1], ln_b[l, 1])
        if need_ctx:
            x_ctx = _layer_norm(DEEPNORM_ALPHA * x_ctx + g2_c * ffn[n_lat_tok:].reshape(x_ctx.shape),
                                ln_g[l, 1], ln_b[l, 1])
    return x_lat
```

```python
import math
from functools import partial

import jax
import jax.numpy as jnp
from jax import lax
from jax.experimental import pallas as pl
from jax.experimental.pallas import tpu as pltpu

D_MODEL = 2048
DEPTH = 2
GRID_W = 64
F32 = jnp.float32
BF16 = jnp.bfloat16

HY_WIDTH = D_MODEL // 4
LRU_WIDTH = D_MODEL // 4
ATT_WIDTH = D_MODEL // 2
HY_ORDER = 2
HY_CONV = 3
HY_BANDS = 8
HY_TARGET = 1e-2
HY_MAX_DECAY = math.log(HY_TARGET) / 0.3
HY_MIN_DECAY = math.log(HY_TARGET) / 1.5
HY_IN = (HY_ORDER + 1) * HY_WIDTH

LRU_HEADS = 8
LRU_HEAD_DIM = LRU_WIDTH // LRU_HEADS
LRU_CONV = 4
LRU_C = 8.0
LRU_IN = 2 * LRU_WIDTH

ATT_HEADS = 4
ATT_V_DIM = ATT_WIDTH // ATT_HEADS
ATT_HEAD_DIM = ATT_V_DIM // 2
ATT_QK = ATT_HEADS * 2 * ATT_HEAD_DIM
Q_BLOCK = 128
ROPE_THETA = 10000.0
ROPE_AXIS_DIM = ATT_HEAD_DIM // 2

N_EXPERTS = 64
TOP_K = 8
ROUTED_SCALE = 2.5
EXPERT_BLOCK = 128

LN_EPS = 1e-5
DEEPNORM_ALPHA = (2 * DEPTH) ** 0.25

VMEM_LIMIT_BYTES = 48 * 1024 * 1024


def _mm_body(a_ref, b_ref, o_ref):
    o_ref[...] = jnp.dot(a_ref[...], b_ref[...], preferred_element_type=F32)


def _matmul(a, b, tm=512, tn=512):
    m, k = a.shape
    n = b.shape[1]
    tm = min(tm, m)
    tn = min(tn, n)
    assert m % tm == 0 and n % tn == 0
    return pl.pallas_call(
        _mm_body,
        grid=(m // tm, n // tn),
        in_specs=[pl.BlockSpec((tm, k), lambda i, j: (i, 0)),
                  pl.BlockSpec((k, tn), lambda i, j: (0, j))],
        out_specs=pl.BlockSpec((tm, tn), lambda i, j: (i, j)),
        out_shape=jax.ShapeDtypeStruct((m, n), F32),
        compiler_params=pltpu.CompilerParams(
            dimension_semantics=("parallel", "arbitrary"),
            vmem_limit_bytes=VMEM_LIMIT_BYTES),
    )(a, b)


def _proj(x, w):
    return _matmul(x.astype(BF16), w.astype(BF16))


def _layer_norm(x, gain=None, bias=None):
    mu = jnp.mean(x, axis=-1, keepdims=True)
    var = jnp.mean(jnp.square(x - mu), axis=-1, keepdims=True)
    y = (x - mu) * lax.rsqrt(var + LN_EPS)
    if gain is not None:
        y = y * gain + bias
    return y


def _modulate(x, shift, scale):
    return _layer_norm(x) * (1.0 + scale) + shift


def _depthwise_conv(x, w, b, pad_left, pad_right):
    y = lax.conv_general_dilated(
        x, w[:, None, :], window_strides=(1,),
        padding=[(pad_left, pad_right)], dimension_numbers=('NWC', 'WIO', 'NWC'),
        feature_group_count=x.shape[-1])
    return y + b


def _hyena_spectrum(n, w1, b1, w2, b2, w3, freq):
    pos = jnp.arange(n, dtype=F32)
    t = (pos / max(n - 1, 1))[:, None]
    bands = jnp.linspace(1e-4, HY_BANDS - 1, HY_BANDS, dtype=F32)
    ang = (2.0 * math.pi / n) * pos[:, None] * bands
    feats = jnp.concatenate([t, jnp.cos(ang), -jnp.sin(ang)], axis=-1)
    hp = lax.Precision.HIGHEST
    hid = jnp.sin(freq * (jnp.dot(feats, w1, precision=hp) + b1))
    hid = jnp.sin(freq * (jnp.dot(hid, w2, precision=hp) + b2))
    filt = jnp.dot(hid, w3, precision=hp).reshape(n, 2, HY_ORDER, HY_WIDTH)
    decay = jnp.abs(jnp.linspace(HY_MIN_DECAY, HY_MAX_DECAY, HY_WIDTH, dtype=F32))
    filt = filt * jnp.exp(-t * decay)[:, None, None, :]
    fwd, bwd = filt[:, 0], filt[:, 1]
    two_sided = jnp.concatenate([fwd, jnp.zeros_like(fwd[:1]), jnp.flip(bwd[1:], axis=0)], axis=0)
    two_sided = two_sided / jnp.sum(jnp.abs(two_sided), axis=0, keepdims=True)
    return jnp.fft.rfft(two_sided, axis=0)


def _hyena_mixer(u, conv_w, conv_b, w1, b1, w2, b2, w3, freq, skip):
    n = u.shape[1]
    u = _depthwise_conv(u, conv_w, conv_b, HY_CONV // 2, HY_CONV - 1 - HY_CONV // 2)
    gates = (u[..., :HY_WIDTH], u[..., HY_WIDTH:2 * HY_WIDTH])
    z = u[..., 2 * HY_WIDTH:]
    spec = _hyena_spectrum(n, w1, b1, w2, b2, w3, freq)
    for o in range(HY_ORDER):
        zc = jnp.fft.irfft(jnp.fft.rfft(z, n=2 * n, axis=1) * spec[None, :, o], n=2 * n, axis=1)[:, :n]
        z = gates[o] * (zc + skip[o] * z)
    return z


def _rglru_coeffs(xr, wa, ba, wx, bx, lam):
    bsz, n, _ = xr.shape
    xh = xr.reshape(bsz, n, LRU_HEADS, LRU_HEAD_DIM)
    r = jax.nn.sigmoid(jnp.einsum('blhi,hij->blhj', xh, wa).reshape(bsz, n, LRU_WIDTH) + ba)
    i = jax.nn.sigmoid(jnp.einsum('blhi,hij->blhj', xh, wx).reshape(bsz, n, LRU_WIDTH) + bx)
    log_a = -LRU_C * r * jax.nn.softplus(-lam)
    a = jnp.exp(log_a)
    b = jnp.sqrt(-jnp.expm1(2.0 * log_a)) * (i * xr)
    return a, b


def _affine_combine(left, right):
    a_l, b_l = left
    a_r, b_r = right
    return a_l * a_r, a_r * b_l + b_r


def _linear_scan(a, b, h0, reverse):
    if reverse:
        a, b = jnp.flip(a, axis=1), jnp.flip(b, axis=1)
    a_cum, b_cum = lax.associative_scan(_affine_combine, (a, b), axis=1)
    h = a_cum * h0[:, None, :] + b_cum
    return jnp.flip(h, axis=1) if reverse else h


def _rglru_mixer(u_lat, u_ctx, conv_w, conv_b, wa, ba, wx, bx, lam, need_ctx):
    def recurrent_input(u):
        return _depthwise_conv(u[..., LRU_WIDTH:], conv_w, conv_b, LRU_CONV // 2, LRU_CONV - 1 - LRU_CONV // 2)
    xr_lat, xr_ctx = recurrent_input(u_lat), recurrent_input(u_ctx)
    h_lat = jnp.zeros(xr_lat.shape, F32)
    ctx_states = []
    for d, reverse in enumerate((False, True)):
        a_c, b_c = _rglru_coeffs(xr_ctx, wa[d], ba[d], wx[d], bx[d], lam[d])
        hs_c = _linear_scan(a_c, b_c, jnp.zeros_like(a_c[:, 0]), reverse)
        h_end = hs_c[:, 0] if reverse else hs_c[:, -1]
        a_l, b_l = _rglru_coeffs(xr_lat, wa[d], ba[d], wx[d], bx[d], lam[d])
        h_lat = h_lat + _linear_scan(a_l, b_l, h_end, reverse)
        ctx_states.append(hs_c)
    y_lat = h_lat * jax.nn.gelu(u_lat[..., :LRU_WIDTH])
    y_ctx = None
    if need_ctx:
        y_ctx = (ctx_states[0] + ctx_states[1]) * jax.nn.gelu(u_ctx[..., :LRU_WIDTH])
    return y_lat, y_ctx


def _axial_rope_tables(n):
    rows = n // GRID_W
    row = jnp.broadcast_to(jnp.arange(rows, dtype=F32)[:, None], (rows, GRID_W)).reshape(-1)
    col = jnp.broadcast_to(jnp.arange(GRID_W, dtype=F32)[None, :], (rows, GRID_W)).reshape(-1)
    inv = ROPE_THETA ** (-jnp.arange(0, ROPE_AXIS_DIM, 2, dtype=F32) / ROPE_AXIS_DIM)
    ang = jnp.stack([row[:, None] * inv, col[:, None] * inv], axis=1)
    return jnp.cos(ang), jnp.sin(ang)


def _apply_axial_rope(x, cos, sin):
    xs = x.reshape(x.shape[:-1] + (2, 2, ROPE_AXIS_DIM // 2))
    x1, x2 = xs[..., 0, :], xs[..., 1, :]
    cb, sb = cos[None, :, None, None], sin[None, :, None, None]
    out = jnp.stack([x1 * cb - x2 * sb, x2 * cb + x1 * sb], axis=-2)
    return out.reshape(x.shape)


ATT_Q_TILE = 256


def _attn_body(q_ref, k_ref, v_ref, lv_ref, g_ref, o_ref, *, lam_init):
    lv = lv_ref[...]
    lam = (jnp.exp(jnp.sum(lv[0:1] * lv[1:2], keepdims=True))
           - jnp.exp(jnp.sum(lv[2:3] * lv[3:4], keepdims=True)) + lam_init)
    q = q_ref[0]
    k = k_ref[0]
    v = v_ref[0]
    outs = []
    for m in range(2):
        qm = q[:, m * ATT_HEAD_DIM:(m + 1) * ATT_HEAD_DIM]
        km = k[:, m * ATT_HEAD_DIM:(m + 1) * ATT_HEAD_DIM]
        s = lax.dot_general(qm, km, (((1,), (1,)), ((), ())), preferred_element_type=F32)
        p = jnp.exp(s - jnp.max(s, axis=-1, keepdims=True))
        denom = jnp.sum(p, axis=-1, keepdims=True)
        o = jnp.dot(p.astype(BF16), v, preferred_element_type=F32)
        outs.append(o / denom)
    o = outs[0] - lam * outs[1]
    y = o * lax.rsqrt(jnp.mean(o * o, axis=-1, keepdims=True) + LN_EPS) * g_ref[...] * (1.0 - lam_init)
    o_ref[0] = y.astype(o_ref.dtype)


def _diff_attention(qkv, lam_vecs, gain, lam_init, q_row0, n_q, n_k):
    bsz = qkv.shape[0]
    tq = ATT_Q_TILE
    q_blk0 = q_row0 // tq
    n_heads = ATT_HEADS
    return pl.pallas_call(
        partial(_attn_body, lam_init=lam_init),
        grid=(bsz, n_heads, n_q // tq),
        in_specs=[
            pl.BlockSpec((1, tq, ATT_V_DIM), lambda b, h, i: (b, q_blk0 + i, h)),
            pl.BlockSpec((1, n_k, ATT_V_DIM), lambda b, h, i: (b, 0, n_heads + h)),
            pl.BlockSpec((1, n_k, ATT_V_DIM), lambda b, h, i: (b, 0, 2 * n_heads + h)),
            pl.BlockSpec((4, ATT_HEAD_DIM), lambda b, h, i: (0, 0)),
            pl.BlockSpec((1, ATT_V_DIM), lambda b, h, i: (0, 0)),
        ],
        out_specs=pl.BlockSpec((1, tq, ATT_V_DIM), lambda b, h, i: (b, i, h)),
        out_shape=jax.ShapeDtypeStruct((bsz, n_q, ATT_WIDTH), BF16),
        compiler_params=pltpu.CompilerParams(
            dimension_semantics=("parallel", "parallel", "arbitrary"),
            vmem_limit_bytes=VMEM_LIMIT_BYTES),
    )(qkv, qkv, qkv, lam_vecs, gain.reshape(1, ATT_V_DIM))


def _diff_attention_mixer(u_lat, u_ctx, lam_vecs, subln_gain, lam_init, need_ctx):
    bsz, n_lat, _ = u_lat.shape
    n_ctx = u_ctx.shape[1]
    scale = ATT_HEAD_DIM ** -0.5

    def split_qk(u):
        bb, n, _ = u.shape
        q = u[..., :ATT_QK].reshape(bb, n, ATT_HEADS, 2, ATT_HEAD_DIM)
        k = u[..., ATT_QK:2 * ATT_QK].reshape(bb, n, ATT_HEADS, 2, ATT_HEAD_DIM)
        return q, k

    q_l, k_l = split_qk(u_lat)
    cos, sin = _axial_rope_tables(n_lat)
    q_l = _apply_axial_rope(q_l, cos, sin) * scale
    k_l = _apply_axial_rope(k_l, cos, sin)
    lat = jnp.concatenate([q_l.reshape(bsz, n_lat, ATT_QK), k_l.reshape(bsz, n_lat, ATT_QK),
                           u_lat[..., 2 * ATT_QK:]], axis=-1).astype(BF16)
    cx = jnp.concatenate([u_ctx[..., :ATT_QK] * scale, u_ctx[..., ATT_QK:]], axis=-1).astype(BF16)
    qkv = jnp.concatenate([cx, lat], axis=1)
    y_lat = _diff_attention(qkv, lam_vecs, subln_gain, lam_init, n_ctx, n_lat, n_ctx + n_lat)
    y_ctx = _diff_attention(qkv, lam_vecs, subln_gain, lam_init, 0, n_ctx, n_ctx) if need_ctx else None
    return y_lat, y_ctx


MOE_BLOCK_ROWS = 256
MOE_VMEM_LIMIT_BYTES = 56 * 1024 * 1024


def _moe_body(be_ref, nu_ref, x_ref, wg_ref, wu_ref, wd_ref, rw_ref, o_ref, wg_s, wu_s, wd_s):
    i = pl.program_id(0)
    e = be_ref[i]
    e_prev = be_ref[jnp.maximum(i - 1, 0)]

    @pl.when((i == 0) | (e != e_prev))
    def _():
        wg_s[...] = wg_ref[0, 0].astype(BF16)
        wu_s[...] = wu_ref[0, 0].astype(BF16)
        wd_s[...] = wd_ref[0, 0].astype(BF16)

    @pl.when(i < nu_ref[0])
    def _():
        xb = x_ref[...]
        g = jnp.dot(xb, wg_s[...], preferred_element_type=F32)
        u = jnp.dot(xb, wu_s[...], preferred_element_type=F32)
        act = (g * jax.nn.sigmoid(g) * u).astype(BF16)
        o_ref[...] = jnp.dot(act, wd_s[...], preferred_element_type=F32) * rw_ref[...]

    @pl.when(i >= nu_ref[0])
    def _():
        o_ref[...] = jnp.zeros_like(o_ref)


def _routed_experts(x_rows, row_w, block_e, n_used, w_gate, w_up, w_down, layer):
    n_rows, d = x_rows.shape
    f = w_gate.shape[-1]
    bm = MOE_BLOCK_ROWS
    n_blocks = n_rows // bm
    grid_spec = pltpu.PrefetchScalarGridSpec(
        num_scalar_prefetch=2,
        grid=(n_blocks,),
        in_specs=[
            pl.BlockSpec((bm, d), lambda i, be, nu: (i, 0)),
            pl.BlockSpec((1, 1, d, f), lambda i, be, nu: (layer, be[i], 0, 0)),
            pl.BlockSpec((1, 1, d, f), lambda i, be, nu: (layer, be[i], 0, 0)),
            pl.BlockSpec((1, 1, f, d), lambda i, be, nu: (layer, be[i], 0, 0)),
            pl.BlockSpec((bm, 1), lambda i, be, nu: (i, 0)),
        ],
        out_specs=pl.BlockSpec((bm, d), lambda i, be, nu: (i, 0)),
        scratch_shapes=[pltpu.VMEM((d, f), BF16), pltpu.VMEM((d, f), BF16), pltpu.VMEM((f, d), BF16)],
    )
    return pl.pallas_call(
        _moe_body,
        grid_spec=grid_spec,
        out_shape=jax.ShapeDtypeStruct((n_rows, d), F32),
        compiler_params=pltpu.CompilerParams(
            dimension_semantics=("arbitrary",),
            vmem_limit_bytes=MOE_VMEM_LIMIT_BYTES),
    )(block_e, n_used, x_rows, w_gate, w_up, w_down, row_w)


def _moe_ffn(h, layer, router_w, router_b, w_gate, w_up, w_down, sh_gate, sh_up, sh_down):
    n_tok, d = h.shape
    bm = MOE_BLOCK_ROWS
    scores = jax.nn.sigmoid(jnp.dot(h, router_w, precision=lax.Precision.HIGHEST))
    _, top_idx = lax.top_k(scores + router_b, TOP_K)
    top_s = jnp.take_along_axis(scores, top_idx, axis=1)
    top_w = top_s / jnp.sum(top_s, axis=-1, keepdims=True) * ROUTED_SCALE
    n_assign = n_tok * TOP_K
    flat_e = top_idx.reshape(-1)
    order = jnp.argsort(flat_e)
    sorted_e = flat_e[order]
    counts = jnp.bincount(flat_e, length=N_EXPERTS)
    padded = (counts + bm - 1) // bm * bm
    padded_end = jnp.cumsum(padded)
    start = jnp.cumsum(counts) - counts
    dest = ((padded_end - padded)[sorted_e] + jnp.arange(n_assign) - start[sorted_e]).astype(jnp.int32)
    n_blocks = -(-n_assign // bm) + N_EXPERTS
    n_rows = n_blocks * bm
    row_tok = jnp.zeros((n_rows,), jnp.int32).at[dest].set((order // TOP_K).astype(jnp.int32))
    row_w = jnp.zeros((n_rows,), F32).at[dest].set(top_w.reshape(-1)[order])
    pos = jnp.zeros((n_assign,), jnp.int32).at[order].set(dest)
    block_e = jnp.minimum(
        jnp.searchsorted(padded_end, jnp.arange(n_blocks) * bm, side='right'), N_EXPERTS - 1).astype(jnp.int32)
    n_used = (padded_end[-1:] // bm).astype(jnp.int32)

    hb = h.astype(BF16)
    y = _routed_experts(hb[row_tok], row_w[:, None], block_e, n_used, w_gate, w_up, w_down, layer)
    routed = jnp.sum(y[pos].reshape(n_tok, TOP_K, d), axis=1)
    gu = _matmul(hb, jnp.concatenate([sh_gate, sh_up], axis=1).astype(BF16))
    d_sh = sh_gate.shape[1]
    act = jax.nn.silu(gu[:, :d_sh]) * gu[:, d_sh:]
    shared = _proj(act, sh_down)
    return routed + shared


def kernel(x, c, ctx, c_ctx, ada_w, ada_b, w_in, hy_conv_w, hy_conv_b, hy_w1, hy_b1, hy_w2, hy_b2, hy_w3, hy_freq, hy_skip, lru_conv_w, lru_conv_b, lru_wa, lru_ba, lru_wx, lru_bx, lru_lambda, att_lambda, att_subln, w_out, ln_g, ln_b, router_w, router_b, exp_w_gate, exp_w_up, exp_w_down, sh_w_gate, sh_w_up, sh_w_down):
    x_lat, x_ctx = x, ctx
    bsz, n_lat, d = x.shape
    n_ctx = ctx.shape[1]
    for l in range(DEPTH):
        need_ctx = l < DEPTH - 1
        lam_init = 0.8 - 0.6 * math.exp(-0.3 * l)
        cond = jnp.concatenate([c, c_ctx[None, :]], axis=0)
        mod = jnp.dot(jax.nn.silu(cond), ada_w[l], precision=lax.Precision.HIGHEST) + ada_b[l]
        mod_lat = mod[:bsz, None, :]
        mod_ctx = mod[bsz:, None, :]
        sh1_l, sc1_l, g1_l, sh2_l, sc2_l, g2_l = jnp.split(mod_lat, 6, axis=-1)
        sh1_c, sc1_c, g1_c, sh2_c, sc2_c, g2_c = jnp.split(mod_ctx, 6, axis=-1)
        hy_args = (hy_conv_w[l], hy_conv_b[l], hy_w1[l], hy_b1[l], hy_w2[l], hy_b2[l],
                   hy_w3[l], hy_freq[l], hy_skip[l])

        h_all = jnp.concatenate([_modulate(x_lat, sh1_l, sc1_l).reshape(-1, d),
                                 _modulate(x_ctx, sh1_c, sc1_c).reshape(-1, d)], axis=0)
        u_all = _proj(h_all, w_in[l])
        u_lat = u_all[:bsz * n_lat].reshape(bsz, n_lat, -1)
        u_ctx = u_all[bsz * n_lat:].reshape(bsz, n_ctx, -1)
        hy_l, lru_l, att_l = u_lat[..., :HY_IN], u_lat[..., HY_IN:HY_IN + LRU_IN], u_lat[..., HY_IN + LRU_IN:]
        hy_c, lru_c, att_c = u_ctx[..., :HY_IN], u_ctx[..., HY_IN:HY_IN + LRU_IN], u_ctx[..., HY_IN + LRU_IN:]
        y_hy_l = _hyena_mixer(hy_l, *hy_args)
        y_lru_l, y_lru_c = _rglru_mixer(lru_l, lru_c, lru_conv_w[l], lru_conv_b[l], lru_wa[l], lru_ba[l],
                                        lru_wx[l], lru_bx[l], lru_lambda[l], need_ctx)
        y_att_l, y_att_c = _diff_attention_mixer(att_l, att_c, att_lambda[l], att_subln[l], lam_init, need_ctx)
        mixed = jnp.concatenate([y_hy_l, y_lru_l, y_att_l], axis=-1).reshape(-1, d)
        if need_ctx:
            mixed_c = jnp.concatenate([_hyena_mixer(hy_c, *hy_args), y_lru_c, y_att_c], axis=-1).reshape(-1, d)
            mixed = jnp.concatenate([mixed, mixed_c], axis=0)
        mix = _proj(mixed, w_out[l])
        mix_l = mix[:bsz * n_lat].reshape(bsz, n_lat, d)
        x_lat = _layer_norm(DEEPNORM_ALPHA * x_lat + g1_l * mix_l, ln_g[l, 0], ln_b[l, 0])
        if need_ctx:
            mix_c = mix[bsz * n_lat:].reshape(bsz, n_ctx, d)
            x_ctx = _layer_norm(DEEPNORM_ALPHA * x_ctx + g1_c * mix_c, ln_g[l, 0], ln_b[l, 0])

        f_lat = _modulate(x_lat, sh2_l, sc2_l).reshape(-1, d)
        n_lat_tok = f_lat.shape[0]
        if need_ctx:
            tokens = jnp.concatenate([f_lat, _modulate(x_ctx, sh2_c, sc2_c).reshape(-1, d)], axis=0)
        else:
            tokens = f_lat
        ffn = _moe_ffn(tokens, l, router_w[l], router_b[l], exp_w_gate, exp_w_up, exp_w_down,
                       sh_w_gate[l], sh_w_up[l], sh_w_down[l])
        x_lat = _layer_norm(DEEPNORM_ALPHA * x_lat + g2_l * ffn[:n_lat_tok].reshape(x_lat.shape),
                            ln_g[l, 1], ln_b[l, 1])
        if need_ctx:
            x_ctx = _layer_norm(DEEPNORM_ALPHA * x_ctx + g2_c * ffn[n_lat_tok:].reshape(x_ctx.shape),
                                ln_g[l, 1], ln_b[l, 1])
    return x_lat
```

```python
import math
from functools import partial

import jax
import jax.numpy as jnp
from jax import lax
from jax.experimental import pallas as pl
from jax.experimental.pallas import tpu as pltpu

D_MODEL = 2048
DEPTH = 2
GRID_W = 64
F32 = jnp.float32
BF16 = jnp.bfloat16
LANES = 128
SUBLANES = 8

HY_WIDTH = D_MODEL // 4
LRU_WIDTH = D_MODEL // 4
ATT_WIDTH = D_MODEL // 2
HY_ORDER = 2
HY_CONV = 3
HY_BANDS = 8
HY_TARGET = 1e-2
HY_MAX_DECAY = math.log(HY_TARGET) / 0.3
HY_MIN_DECAY = math.log(HY_TARGET) / 1.5
HY_IN = (HY_ORDER + 1) * HY_WIDTH

LRU_HEADS = 8
LRU_HEAD_DIM = LRU_WIDTH // LRU_HEADS
LRU_CONV = 4
LRU_C = 8.0
LRU_IN = 2 * LRU_WIDTH

ATT_HEADS = 4
ATT_V_DIM = ATT_WIDTH // ATT_HEADS
ATT_HEAD_DIM = ATT_V_DIM // 2
ATT_QK = ATT_HEADS * 2 * ATT_HEAD_DIM
ROPE_THETA = 10000.0
ROPE_AXIS_DIM = ATT_HEAD_DIM // 2

N_EXPERTS = 64
TOP_K = 8
ROUTED_SCALE = 2.5

LN_EPS = 1e-5
DEEPNORM_ALPHA = (2 * DEPTH) ** 0.25

VMEM_LIMIT_BYTES = 48 * 1024 * 1024


def _mm_body(a_ref, b_ref, o_ref):
    o_ref[...] = jnp.dot(a_ref[...], b_ref[...], preferred_element_type=F32)


def _matmul(a, b, tm=512, tn=512):
    m, k = a.shape
    n = b.shape[1]
    tm = min(tm, m)
    tn = min(tn, n)
    assert m % tm == 0 and n % tn == 0
    return pl.pallas_call(
        _mm_body,
        grid=(m // tm, n // tn),
        in_specs=[pl.BlockSpec((tm, k), lambda i, j: (i, 0)),
                  pl.BlockSpec((k, tn), lambda i, j: (0, j))],
        out_specs=pl.BlockSpec((tm, tn), lambda i, j: (i, j)),
        out_shape=jax.ShapeDtypeStruct((m, n), F32),
        compiler_params=pltpu.CompilerParams(
            dimension_semantics=("parallel", "arbitrary"),
            vmem_limit_bytes=VMEM_LIMIT_BYTES),
        name="dense_matmul",
    )(a, b)


def _proj(x, w):
    return _matmul(x.astype(BF16), w.astype(BF16))


def _layer_norm(x, gain=None, bias=None):
    mu = jnp.mean(x, axis=-1, keepdims=True)
    var = jnp.mean(jnp.square(x - mu), axis=-1, keepdims=True)
    y = (x - mu) * lax.rsqrt(var + LN_EPS)
    if gain is not None:
        y = y * gain + bias
    return y


def _modulate(x, shift, scale):
    return _layer_norm(x) * (1.0 + scale) + shift


def _depthwise_conv(x, w, b, pad_left, pad_right):
    y = lax.conv_general_dilated(
        x, w[:, None, :], window_strides=(1,),
        padding=[(pad_left, pad_right)], dimension_numbers=('NWC', 'WIO', 'NWC'),
        feature_group_count=x.shape[-1])
    return y + b


def _hyena_spectrum(n, w1, b1, w2, b2, w3, freq):
    pos = jnp.arange(n, dtype=F32)
    t = (pos / max(n - 1, 1))[:, None]
    bands = jnp.linspace(1e-4, HY_BANDS - 1, HY_BANDS, dtype=F32)
    ang = (2.0 * math.pi / n) * pos[:, None] * bands
    feats = jnp.concatenate([t, jnp.cos(ang), -jnp.sin(ang)], axis=-1)
    hp = lax.Precision.HIGHEST
    hid = jnp.sin(freq * (jnp.dot(feats, w1, precision=hp) + b1))
    hid = jnp.sin(freq * (jnp.dot(hid, w2, precision=hp) + b2))
    filt = jnp.dot(hid, w3, precision=hp).reshape(n, 2, HY_ORDER, HY_WIDTH)
    decay = jnp.abs(jnp.linspace(HY_MIN_DECAY, HY_MAX_DECAY, HY_WIDTH, dtype=F32))
    filt = filt * jnp.exp(-t * decay)[:, None, None, :]
    fwd, bwd = filt[:, 0], filt[:, 1]
    two_sided = jnp.concatenate([fwd, jnp.zeros_like(fwd[:1]), jnp.flip(bwd[1:], axis=0)], axis=0)
    two_sided = two_sided / jnp.sum(jnp.abs(two_sided), axis=0, keepdims=True)
    return jnp.fft.rfft(two_sided, axis=0)


def _hyena_mixer(u, conv_w, conv_b, w1, b1, w2, b2, w3, freq, skip):
    n = u.shape[1]
    u = _depthwise_conv(u, conv_w, conv_b, HY_CONV // 2, HY_CONV - 1 - HY_CONV // 2)
    gates = (u[..., :HY_WIDTH], u[..., HY_WIDTH:2 * HY_WIDTH])
    z = u[..., 2 * HY_WIDTH:]
    spec = _hyena_spectrum(n, w1, b1, w2, b2, w3, freq)
    for o in range(HY_ORDER):
        zc = jnp.fft.irfft(jnp.fft.rfft(z, n=2 * n, axis=1) * spec[None, :, o], n=2 * n, axis=1)[:, :n]
        z = gates[o] * (zc + skip[o] * z)
    return z


LRU_CONV_MARGIN = SUBLANES
LRU_TIME_CHUNK = 1024


def _gelu_tanh(x):
    return 0.5 * x * (1.0 + jnp.tanh(math.sqrt(2.0 / math.pi) * (x + 0.044715 * (x * x * x))))


def _rglru_body(ug_ref, ur_ref, cw_ref, cb_ref, wa_ref, ba_ref, wx_ref, bx_ref, cl_ref, o_ref,
                pad_s, af_s, bf_s, ar_s, br_s, hf_s, hr_s, *, n_ctx, n_lat):
    a_s = (af_s, ar_s)
    b_s = (bf_s, br_s)
    mg = LRU_CONV_MARGIN
    cw = cw_ref[...]
    zeros_margin = jnp.zeros((mg, LANES), F32)
    for r0, n in ((0, n_ctx), (n_ctx, n_lat)):
        pad_s[0:mg, :] = zeros_margin
        pad_s[mg:mg + n, :] = ur_ref[0, r0:r0 + n, :]
        pad_s[mg + n:2 * mg + n, :] = zeros_margin
        chunk = min(n, LRU_TIME_CHUNK)
        for c0 in range(0, n, chunk):
            xr = cb_ref[...] + cw[0:1] * pad_s[mg + c0 - 2:mg + c0 - 2 + chunk, :]
            for j in range(1, LRU_CONV):
                xr = xr + cw[j:j + 1] * pad_s[mg + c0 + j - 2:mg + c0 + j - 2 + chunk, :]
            xb = xr.astype(BF16)
            for d in range(2):
                r = jax.nn.sigmoid(jnp.dot(xb, wa_ref[d, 0], preferred_element_type=F32) + ba_ref[d:d + 1, :])
                i = jax.nn.sigmoid(jnp.dot(xb, wx_ref[d, 0], preferred_element_type=F32) + bx_ref[d:d + 1, :])
                log_a = cl_ref[d:d + 1, :] * r
                a_s[d][r0 + c0:r0 + c0 + chunk, :] = jnp.exp(log_a)
                b_s[d][r0 + c0:r0 + c0 + chunk, :] = jnp.sqrt(1.0 - jnp.exp(2.0 * log_a)) * (i * xr)

    def scan_segment(r0, n, carry):
        def step(t, hc):
            hf, hr = hc
            tf = r0 + t
            tr = r0 + n - 1 - t
            hf = af_s[pl.ds(tf, 1), :] * hf + bf_s[pl.ds(tf, 1), :]
            hr = ar_s[pl.ds(tr, 1), :] * hr + br_s[pl.ds(tr, 1), :]
            hf_s[pl.ds(tf, 1), :] = hf
            hr_s[pl.ds(tr, 1), :] = hr
            return hf, hr
        return lax.fori_loop(0, n, step, carry, unroll=8)

    h0 = jnp.zeros((1, LANES), F32)
    carry = scan_segment(0, n_ctx, (h0, h0))
    scan_segment(n_ctx, n_lat, carry)
    o_ref[0] = ((hf_s[...] + hr_s[...]) * _gelu_tanh(ug_ref[0])).astype(o_ref.dtype)


def _block_diag_tiles(w):
    w = w.reshape(2, LRU_HEADS // 2, 2, LRU_HEAD_DIM, LRU_HEAD_DIM)
    z = jnp.zeros_like(w[:, :, 0])
    top = jnp.concatenate([w[:, :, 0], z], axis=-1)
    bot = jnp.concatenate([z, w[:, :, 1]], axis=-1)
    return jnp.concatenate([top, bot], axis=-2).astype(BF16)


def _rglru_mixer(u_all, n_ctx, conv_w, conv_b, wa, ba, wx, bx, lam):
    bsz, n_all, _ = u_all.shape
    n_lat = n_all - n_ctx
    n_tiles = LRU_WIDTH // LANES
    gate_blk0 = HY_IN // LANES
    rec_blk0 = (HY_IN + LRU_WIDTH) // LANES
    decay_scale = -LRU_C * jax.nn.softplus(-lam)
    seq = lambda col0: pl.BlockSpec((1, n_all, LANES), lambda b, j: (b, 0, col0 + j))
    vec = lambda rows: pl.BlockSpec((rows, LANES), lambda b, j: (0, j))
    mat = pl.BlockSpec((2, 1, LANES, LANES), lambda b, j: (0, j, 0, 0))
    full = pltpu.VMEM((n_all, LANES), F32)
    return pl.pallas_call(
        partial(_rglru_body, n_ctx=n_ctx, n_lat=n_lat),
        grid=(bsz, n_tiles),
        in_specs=[seq(gate_blk0), seq(rec_blk0), vec(LRU_CONV), vec(1), mat, vec(2), mat, vec(2), vec(2)],
        out_specs=pl.BlockSpec((1, n_all, LANES), lambda b, j: (b, 0, j)),
        out_shape=jax.ShapeDtypeStruct((bsz, n_all, LRU_WIDTH), BF16),
        scratch_shapes=[pltpu.VMEM((n_lat + 2 * LRU_CONV_MARGIN, LANES), F32),
                        full, full, full, full, full, full],
        compiler_params=pltpu.CompilerParams(
            dimension_semantics=("parallel", "parallel"),
            vmem_limit_bytes=VMEM_LIMIT_BYTES),
        name="rglru",
    )(u_all, u_all, conv_w, conv_b.reshape(1, LRU_WIDTH), _block_diag_tiles(wa), ba,
      _block_diag_tiles(wx), bx, decay_scale)


ATT_Q_TILE = 256


def _attn_body(q_ref, k_ref, v_ref, lv_ref, g_ref, o_ref, *, lam_init):
    lv = lv_ref[...]
    lam = (jnp.exp(jnp.sum(lv[0:1] * lv[1:2], keepdims=True))
           - jnp.exp(jnp.sum(lv[2:3] * lv[3:4], keepdims=True)) + lam_init)
    q = q_ref[0]
    k = k_ref[0]
    v = v_ref[0]
    outs = []
    for m in range(2):
        qm = q[:, m * ATT_HEAD_DIM:(m + 1) * ATT_HEAD_DIM]
        km = k[:, m * ATT_HEAD_DIM:(m + 1) * ATT_HEAD_DIM]
        s = lax.dot_general(qm, km, (((1,), (1,)), ((), ())), preferred_element_type=F32)
        p = jnp.exp(s - jnp.max(s, axis=-1, keepdims=True))
        denom = jnp.sum(p, axis=-1, keepdims=True)
        o = jnp.dot(p.astype(BF16), v, preferred_element_type=F32)
        outs.append(o / denom)
    o = outs[0] - lam * outs[1]
    y = o * lax.rsqrt(jnp.mean(o * o, axis=-1, keepdims=True) + LN_EPS) * g_ref[...] * (1.0 - lam_init)
    o_ref[0] = y.astype(o_ref.dtype)


def _diff_attention(qkv, lam_vecs, gain, lam_init, q_row0, n_q, n_k):
    bsz = qkv.shape[0]
    tq = ATT_Q_TILE
    q_blk0 = q_row0 // tq
    n_heads = ATT_HEADS
    return pl.pallas_call(
        partial(_attn_body, lam_init=lam_init),
        grid=(bsz, n_heads, n_q // tq),
        in_specs=[
            pl.BlockSpec((1, tq, ATT_V_DIM), lambda b, h, i: (b, q_blk0 + i, h)),
            pl.BlockSpec((1, n_k, ATT_V_DIM), lambda b, h, i: (b, 0, n_heads + h)),
            pl.BlockSpec((1, n_k, ATT_V_DIM), lambda b, h, i: (b, 0, 2 * n_heads + h)),
            pl.BlockSpec((4, ATT_HEAD_DIM), lambda b, h, i: (0, 0)),
            pl.BlockSpec((1, ATT_V_DIM), lambda b, h, i: (0, 0)),
        ],
        out_specs=pl.BlockSpec((1, tq, ATT_V_DIM), lambda b, h, i: (b, i, h)),
        out_shape=jax.ShapeDtypeStruct((bsz, n_q, ATT_WIDTH), BF16),
        compiler_params=pltpu.CompilerParams(
            dimension_semantics=("parallel", "parallel", "arbitrary"),
            vmem_limit_bytes=VMEM_LIMIT_BYTES),
        name="diff_attention",
    )(qkv, qkv, qkv, lam_vecs, gain.reshape(1, ATT_V_DIM))


def _axial_rope_tables(n):
    rows = n // GRID_W
    row = jnp.broadcast_to(jnp.arange(rows, dtype=F32)[:, None], (rows, GRID_W)).reshape(-1)
    col = jnp.broadcast_to(jnp.arange(GRID_W, dtype=F32)[None, :], (rows, GRID_W)).reshape(-1)
    inv = ROPE_THETA ** (-jnp.arange(0, ROPE_AXIS_DIM, 2, dtype=F32) / ROPE_AXIS_DIM)
    ang = jnp.stack([row[:, None] * inv, col[:, None] * inv], axis=1)
    return jnp.cos(ang), jnp.sin(ang)


def _apply_axial_rope(x, cos, sin):
    xs = x.reshape(x.shape[:-1] + (2, 2, ROPE_AXIS_DIM // 2))
    x1, x2 = xs[..., 0, :], xs[..., 1, :]
    cb, sb = cos[None, :, None, None], sin[None, :, None, None]
    out = jnp.stack([x1 * cb - x2 * sb, x2 * cb + x1 * sb], axis=-2)
    return out.reshape(x.shape)


def _diff_attention_mixer(u_att, n_ctx, lam_vecs, subln_gain, lam_init, need_ctx):
    bsz, n_all, _ = u_att.shape
    n_lat = n_all - n_ctx
    scale = ATT_HEAD_DIM ** -0.5
    u_lat = u_att[:, n_ctx:]
    u_ctx = u_att[:, :n_ctx]
    q_l = u_lat[..., :ATT_QK].reshape(bsz, n_lat, ATT_HEADS, 2, ATT_HEAD_DIM)
    k_l = u_lat[..., ATT_QK:2 * ATT_QK].reshape(bsz, n_lat, ATT_HEADS, 2, ATT_HEAD_DIM)
    cos, sin = _axial_rope_tables(n_lat)
    q_l = _apply_axial_rope(q_l, cos, sin) * scale
    k_l = _apply_axial_rope(k_l, cos, sin)
    lat = jnp.concatenate([q_l.reshape(bsz, n_lat, ATT_QK), k_l.reshape(bsz, n_lat, ATT_QK),
                           u_lat[..., 2 * ATT_QK:]], axis=-1).astype(BF16)
    cx = jnp.concatenate([u_ctx[..., :ATT_QK] * scale, u_ctx[..., ATT_QK:]], axis=-1).astype(BF16)
    qkv = jnp.concatenate([cx, lat], axis=1)
    y_lat = _diff_attention(qkv, lam_vecs, subln_gain, lam_init, n_ctx, n_lat, n_all)
    y_ctx = _diff_attention(qkv, lam_vecs, subln_gain, lam_init, 0, n_ctx, n_ctx) if need_ctx else None
    return y_lat, y_ctx


MOE_BLOCK_ROWS = 256
MOE_VMEM_LIMIT_BYTES = 56 * 1024 * 1024


def _moe_body(be_ref, nu_ref, x_ref, wg_ref, wu_ref, wd_ref, rw_ref, o_ref, wg_s, wu_s, wd_s):
    i = pl.program_id(0)
    e = be_ref[i]
    e_prev = be_ref[jnp.maximum(i - 1, 0)]

    @pl.when((i == 0) | (e != e_prev))
    def _():
        wg_s[...] = wg_ref[0, 0].astype(BF16)
        wu_s[...] = wu_ref[0, 0].astype(BF16)
        wd_s[...] = wd_ref[0, 0].astype(BF16)

    @pl.when(i < nu_ref[0])
    def _():
        xb = x_ref[...]
        g = jnp.dot(xb, wg_s[...], preferred_element_type=F32)
        u = jnp.dot(xb, wu_s[...], preferred_element_type=F32)
        act = (g * jax.nn.sigmoid(g) * u).astype(BF16)
        o_ref[...] = jnp.dot(act, wd_s[...], preferred_element_type=F32) * rw_ref[...]

    @pl.when(i >= nu_ref[0])
    def _():
        o_ref[...] = jnp.zeros_like(o_ref)


def _routed_experts(x_rows, row_w, block_e, n_used, w_gate, w_up, w_down, layer):
    n_rows, d = x_rows.shape
    f = w_gate.shape[-1]
    bm = MOE_BLOCK_ROWS
    n_blocks = n_rows // bm
    grid_spec = pltpu.PrefetchScalarGridSpec(
        num_scalar_prefetch=2,
        grid=(n_blocks,),
        in_specs=[
            pl.BlockSpec((bm, d), lambda i, be, nu: (i, 0)),
            pl.BlockSpec((1, 1, d, f), lambda i, be, nu: (layer, be[i], 0, 0)),
            pl.BlockSpec((1, 1, d, f), lambda i, be, nu: (layer, be[i], 0, 0)),
            pl.BlockSpec((1, 1, f, d), lambda i, be, nu: (layer, be[i], 0, 0)),
            pl.BlockSpec((bm, 1), lambda i, be, nu: (i, 0)),
        ],
        out_specs=pl.BlockSpec((bm, d), lambda i, be, nu: (i, 0)),
        scratch_shapes=[pltpu.VMEM((d, f), BF16), pltpu.VMEM((d, f), BF16), pltpu.VMEM((f, d), BF16)],
    )
    return pl.pallas_call(
        _moe_body,
        grid_spec=grid_spec,
        out_shape=jax.ShapeDtypeStruct((n_rows, d), F32),
        compiler_params=pltpu.CompilerParams(
            dimension_semantics=("arbitrary",),
            vmem_limit_bytes=MOE_VMEM_LIMIT_BYTES),
        name="routed_experts",
    )(block_e, n_used, x_rows, w_gate, w_up, w_down, row_w)


def _moe_ffn(h, layer, router_w, router_b, w_gate, w_up, w_down, sh_gate, sh_up, sh_down):
    n_tok, d = h.shape
    bm = MOE_BLOCK_ROWS
    scores = jax.nn.sigmoid(jnp.dot(h, router_w, precision=lax.Precision.HIGHEST))
    _, top_idx = lax.top_k(scores + router_b, TOP_K)
    top_s = jnp.take_along_axis(scores, top_idx, axis=1)
    top_w = (top_s / jnp.sum(top_s, axis=-1, keepdims=True) * ROUTED_SCALE).reshape(-1)
    n_assign = n_tok * TOP_K
    flat_e = top_idx.reshape(-1)
    order = jnp.argsort(flat_e).astype(jnp.int32)
    rank = jnp.argsort(order).astype(jnp.int32)
    counts = jnp.bincount(flat_e, length=N_EXPERTS).astype(jnp.int32)
    padded = (counts + bm - 1) // bm * bm
    padded_end = jnp.cumsum(padded)
    start = jnp.cumsum(counts) - counts
    shift = padded_end - padded - start
    n_blocks = -(-n_assign // bm) + N_EXPERTS
    block_e = jnp.minimum(
        jnp.searchsorted(padded_end, jnp.arange(n_blocks) * bm, side='right'), N_EXPERTS - 1).astype(jnp.int32)
    n_used = (padded_end[-1:] // bm).astype(jnp.int32)
    row_e = jnp.repeat(block_e, bm)
    row_slot = jnp.arange(n_blocks * bm, dtype=jnp.int32) - shift[row_e]
    row_live = row_slot < (start + counts)[row_e]
    row_assign = order[jnp.where(row_live, row_slot, 0)]
    row_tok = jnp.where(row_live, row_assign // TOP_K, 0)
    row_w = jnp.where(row_live, top_w[row_assign], 0.0)
    pos = rank + shift[flat_e]

    hb = h.astype(BF16)
    y = _routed_experts(hb[row_tok], row_w[:, None], block_e, n_used, w_gate, w_up, w_down, layer)
    picked = lax.optimization_barrier(y[pos])
    routed = jnp.sum(picked.reshape(n_tok, TOP_K, d), axis=1)
    gu = _matmul(hb, jnp.concatenate([sh_gate, sh_up], axis=1).astype(BF16))
    d_sh = sh_gate.shape[1]
    act = jax.nn.silu(gu[:, :d_sh]) * gu[:, d_sh:]
    shared = _proj(act, sh_down)
    return routed + shared


def kernel(x, c, ctx, c_ctx, ada_w, ada_b, w_in, hy_conv_w, hy_conv_b, hy_w1, hy_b1, hy_w2, hy_b2, hy_w3, hy_freq, hy_skip, lru_conv_w, lru_conv_b, lru_wa, lru_ba, lru_wx, lru_bx, lru_lambda, att_lambda, att_subln, w_out, ln_g, ln_b, router_w, router_b, exp_w_gate, exp_w_up, exp_w_down, sh_w_gate, sh_w_up, sh_w_down):
    bsz, n_lat, d = x.shape
    n_ctx = ctx.shape[1]
    n_all = n_ctx + n_lat
    x_all = jnp.concatenate([ctx, x], axis=1)
    x_lat = x

    def per_row(v_ctx, v_lat):
        return jnp.concatenate([jnp.broadcast_to(v_ctx[None, None, :], (bsz, n_ctx, d)),
                                jnp.broadcast_to(v_lat[:, None, :], (bsz, n_lat, d))], axis=1)

    for l in range(DEPTH):
        need_ctx = l < DEPTH - 1
        lam_init = 0.8 - 0.6 * math.exp(-0.3 * l)
        cond = jnp.concatenate([c, c_ctx[None, :]], axis=0)
        mod = jnp.dot(jax.nn.silu(cond), ada_w[l], precision=lax.Precision.HIGHEST) + ada_b[l]
        sh1, sc1, g1, sh2, sc2, g2 = [per_row(m[bsz], m[:bsz]) for m in jnp.split(mod, 6, axis=-1)]
        hy_args = (hy_conv_w[l], hy_conv_b[l], hy_w1[l], hy_b1[l], hy_w2[l], hy_b2[l],
                   hy_w3[l], hy_freq[l], hy_skip[l])

        u_all = _proj(_modulate(x_all, sh1, sc1).reshape(-1, d), w_in[l]).reshape(bsz, n_all, -1)
        y_hy_l = _hyena_mixer(u_all[:, n_ctx:, :HY_IN], *hy_args)
        y_lru = _rglru_mixer(u_all, n_ctx, lru_conv_w[l], lru_conv_b[l], lru_wa[l], lru_ba[l],
                             lru_wx[l], lru_bx[l], lru_lambda[l])
        y_att_l, y_att_c = _diff_attention_mixer(u_all[..., HY_IN + LRU_IN:], n_ctx, att_lambda[l],
                                                 att_subln[l], lam_init, need_ctx)
        if need_ctx:
            y_hy = jnp.concatenate([_hyena_mixer(u_all[:, :n_ctx, :HY_IN], *hy_args), y_hy_l], axis=1)
            y_att = jnp.concatenate([y_att_c, y_att_l], axis=1)
            mixed = jnp.concatenate([y_hy.astype(BF16), y_lru, y_att], axis=-1)
            mix = _matmul(mixed.reshape(-1, d), w_out[l].astype(BF16)).reshape(bsz, n_all, d)
            x_all = _layer_norm(DEEPNORM_ALPHA * x_all + g1 * mix, ln_g[l, 0], ln_b[l, 0])
            tokens = _modulate(x_all, sh2, sc2)
        else:
            mixed = jnp.concatenate([y_hy_l.astype(BF16), y_lru[:, n_ctx:], y_att_l], axis=-1)
            mix = _matmul(mixed.reshape(-1, d), w_out[l].astype(BF16)).reshape(bsz, n_lat, d)
            x_lat = _layer_norm(DEEPNORM_ALPHA * x_all[:, n_ctx:] + g1[:, n_ctx:] * mix, ln_g[l, 0], ln_b[l, 0])
            tokens = _modulate(x_lat, sh2[:, n_ctx:], sc2[:, n_ctx:])

        ffn = _moe_ffn(tokens.reshape(-1, d), l, router_w[l], router_b[l], exp_w_gate, exp_w_up, exp_w_down,
                       sh_w_gate[l], sh_w_up[l], sh_w_down[l]).reshape(tokens.shape)
        if need_ctx:
            x_all = _layer_norm(DEEPNORM_ALPHA * x_all + g2 * ffn, ln_g[l, 1], ln_b[l, 1])
        else:
            x_lat = _layer_norm(DEEPNORM_ALPHA * x_lat + g2[:, n_ctx:] * ffn, ln_g[l, 1], ln_b[l, 1])
    return x_lat
```

```python
import math
from functools import partial

import jax
import jax.numpy as jnp
from jax import lax
from jax.experimental import pallas as pl
from jax.experimental.pallas import tpu as pltpu

D_MODEL = 2048
DEPTH = 2
GRID_W = 64
F32 = jnp.float32
BF16 = jnp.bfloat16
LANES = 128
SUBLANES = 8

HY_WIDTH = D_MODEL // 4
LRU_WIDTH = D_MODEL // 4
ATT_WIDTH = D_MODEL // 2
HY_ORDER = 2
HY_CONV = 3
HY_BANDS = 8
HY_TARGET = 1e-2
HY_MAX_DECAY = math.log(HY_TARGET) / 0.3
HY_MIN_DECAY = math.log(HY_TARGET) / 1.5
HY_IN = (HY_ORDER + 1) * HY_WIDTH

LRU_HEADS = 8
LRU_HEAD_DIM = LRU_WIDTH // LRU_HEADS
LRU_CONV = 4
LRU_C = 8.0
LRU_IN = 2 * LRU_WIDTH

ATT_HEADS = 4
ATT_V_DIM = ATT_WIDTH // ATT_HEADS
ATT_HEAD_DIM = ATT_V_DIM // 2
ATT_QK = ATT_HEADS * 2 * ATT_HEAD_DIM
ROPE_THETA = 10000.0
ROPE_AXIS_DIM = ATT_HEAD_DIM // 2

N_EXPERTS = 64
TOP_K = 8
ROUTED_SCALE = 2.5

LN_EPS = 1e-5
DEEPNORM_ALPHA = (2 * DEPTH) ** 0.25

VMEM_LIMIT_BYTES = 48 * 1024 * 1024


def _mm_body(a_ref, b_ref, o_ref):
    o_ref[...] = jnp.dot(a_ref[...], b_ref[...], preferred_element_type=F32)


def _matmul(a, b, tm=512, tn=512):
    m, k = a.shape
    n = b.shape[1]
    tm = min(tm, m)
    tn = min(tn, n)
    assert m % tm == 0 and n % tn == 0
    return pl.pallas_call(
        _mm_body,
        grid=(m // tm, n // tn),
        in_specs=[pl.BlockSpec((tm, k), lambda i, j: (i, 0)),
                  pl.BlockSpec((k, tn), lambda i, j: (0, j))],
        out_specs=pl.BlockSpec((tm, tn), lambda i, j: (i, j)),
        out_shape=jax.ShapeDtypeStruct((m, n), F32),
        compiler_params=pltpu.CompilerParams(
            dimension_semantics=("parallel", "arbitrary"),
            vmem_limit_bytes=VMEM_LIMIT_BYTES),
        name="dense_matmul",
    )(a, b)


def _proj(x, w):
    return _matmul(x.astype(BF16), w.astype(BF16))


def _layer_norm(x, gain=None, bias=None):
    mu = jnp.mean(x, axis=-1, keepdims=True)
    var = jnp.mean(jnp.square(x - mu), axis=-1, keepdims=True)
    y = (x - mu) * lax.rsqrt(var + LN_EPS)
    if gain is not None:
        y = y * gain + bias
    return y


def _modulate(x, shift, scale):
    return _layer_norm(x) * (1.0 + scale) + shift


def _depthwise_conv(x, w, b, pad_left, pad_right):
    y = lax.conv_general_dilated(
        x, w[:, None, :], window_strides=(1,),
        padding=[(pad_left, pad_right)], dimension_numbers=('NWC', 'WIO', 'NWC'),
        feature_group_count=x.shape[-1])
    return y + b


def _hyena_spectrum(n, w1, b1, w2, b2, w3, freq):
    pos = jnp.arange(n, dtype=F32)
    t = (pos / max(n - 1, 1))[:, None]
    bands = jnp.linspace(1e-4, HY_BANDS - 1, HY_BANDS, dtype=F32)
    ang = (2.0 * math.pi / n) * pos[:, None] * bands
    feats = jnp.concatenate([t, jnp.cos(ang), -jnp.sin(ang)], axis=-1)
    hp = lax.Precision.HIGHEST
    hid = jnp.sin(freq * (jnp.dot(feats, w1, precision=hp) + b1))
    hid = jnp.sin(freq * (jnp.dot(hid, w2, precision=hp) + b2))
    filt = jnp.dot(hid, w3, precision=hp).reshape(n, 2, HY_ORDER, HY_WIDTH)
    decay = jnp.abs(jnp.linspace(HY_MIN_DECAY, HY_MAX_DECAY, HY_WIDTH, dtype=F32))
    filt = filt * jnp.exp(-t * decay)[:, None, None, :]
    fwd, bwd = filt[:, 0], filt[:, 1]
    two_sided = jnp.concatenate([fwd, jnp.zeros_like(fwd[:1]), jnp.flip(bwd[1:], axis=0)], axis=0)
    two_sided = two_sided / jnp.sum(jnp.abs(two_sided), axis=0, keepdims=True)
    return jnp.fft.rfft(two_sided, axis=0)


def _hyena_mixer(u, conv_w, conv_b, w1, b1, w2, b2, w3, freq, skip):
    n = u.shape[1]
    u = _depthwise_conv(u, conv_w, conv_b, HY_CONV // 2, HY_CONV - 1 - HY_CONV // 2)
    gates = (u[..., :HY_WIDTH], u[..., HY_WIDTH:2 * HY_WIDTH])
    z = u[..., 2 * HY_WIDTH:]
    spec = _hyena_spectrum(n, w1, b1, w2, b2, w3, freq)
    for o in range(HY_ORDER):
        zc = jnp.fft.irfft(jnp.fft.rfft(z, n=2 * n, axis=1) * spec[None, :, o], n=2 * n, axis=1)[:, :n]
        z = gates[o] * (zc + skip[o] * z)
    return z


LRU_CONV_MARGIN = SUBLANES
LRU_TIME_CHUNK = 1024


def _gelu_tanh(x):
    return 0.5 * x * (1.0 + jnp.tanh(math.sqrt(2.0 / math.pi) * (x + 0.044715 * (x * x * x))))


def _rglru_body(ug_ref, ur_ref, cw_ref, cb_ref, wa_ref, ba_ref, wx_ref, bx_ref, cl_ref, o_ref,
                pad_s, af_s, bf_s, ar_s, br_s, hf_s, hr_s, *, n_ctx, n_lat):
    a_s = (af_s, ar_s)
    b_s = (bf_s, br_s)
    mg = LRU_CONV_MARGIN
    cw = cw_ref[...]
    zeros_margin = jnp.zeros((mg, LANES), F32)
    for r0, n in ((0, n_ctx), (n_ctx, n_lat)):
        pad_s[0:mg, :] = zeros_margin
        pad_s[mg:mg + n, :] = ur_ref[0, r0:r0 + n, :]
        pad_s[mg + n:2 * mg + n, :] = zeros_margin
        chunk = min(n, LRU_TIME_CHUNK)
        for c0 in range(0, n, chunk):
            xr = cb_ref[...] + cw[0:1] * pad_s[mg + c0 - 2:mg + c0 - 2 + chunk, :]
            for j in range(1, LRU_CONV):
                xr = xr + cw[j:j + 1] * pad_s[mg + c0 + j - 2:mg + c0 + j - 2 + chunk, :]
            xb = xr.astype(BF16)
            for d in range(2):
                r = jax.nn.sigmoid(jnp.dot(xb, wa_ref[d, 0], preferred_element_type=F32) + ba_ref[d:d + 1, :])
                i = jax.nn.sigmoid(jnp.dot(xb, wx_ref[d, 0], preferred_element_type=F32) + bx_ref[d:d + 1, :])
                log_a = cl_ref[d:d + 1, :] * r
                a_s[d][r0 + c0:r0 + c0 + chunk, :] = jnp.exp(log_a)
                b_s[d][r0 + c0:r0 + c0 + chunk, :] = jnp.sqrt(1.0 - jnp.exp(2.0 * log_a)) * (i * xr)

    def scan_segment(r0, n, carry):
        def step(t, hc):
            hf, hr = hc
            tf = r0 + t
            tr = r0 + n - 1 - t
            hf = af_s[pl.ds(tf, 1), :] * hf + bf_s[pl.ds(tf, 1), :]
            hr = ar_s[pl.ds(tr, 1), :] * hr + br_s[pl.ds(tr, 1), :]
            hf_s[pl.ds(tf, 1), :] = hf
            hr_s[pl.ds(tr, 1), :] = hr
            return hf, hr
        return lax.fori_loop(0, n, step, carry, unroll=8)

    h0 = jnp.zeros((1, LANES), F32)
    carry = scan_segment(0, n_ctx, (h0, h0))
    scan_segment(n_ctx, n_lat, carry)
    o_ref[0] = ((hf_s[...] + hr_s[...]) * _gelu_tanh(ug_ref[0])).astype(o_ref.dtype)


def _block_diag_tiles(w):
    w = w.reshape(2, LRU_HEADS // 2, 2, LRU_HEAD_DIM, LRU_HEAD_DIM)
    z = jnp.zeros_like(w[:, :, 0])
    top = jnp.concatenate([w[:, :, 0], z], axis=-1)
    bot = jnp.concatenate([z, w[:, :, 1]], axis=-1)
    return jnp.concatenate([top, bot], axis=-2).astype(BF16)


def _rglru_mixer(u_all, n_ctx, conv_w, conv_b, wa, ba, wx, bx, lam):
    bsz, n_all, _ = u_all.shape
    n_lat = n_all - n_ctx
    n_tiles = LRU_WIDTH // LANES
    gate_blk0 = HY_IN // LANES
    rec_blk0 = (HY_IN + LRU_WIDTH) // LANES
    decay_scale = -LRU_C * jax.nn.softplus(-lam)
    seq = lambda col0: pl.BlockSpec((1, n_all, LANES), lambda b, j: (b, 0, col0 + j))
    vec = lambda rows: pl.BlockSpec((rows, LANES), lambda b, j: (0, j))
    mat = pl.BlockSpec((2, 1, LANES, LANES), lambda b, j: (0, j, 0, 0))
    full = pltpu.VMEM((n_all, LANES), F32)
    return pl.pallas_call(
        partial(_rglru_body, n_ctx=n_ctx, n_lat=n_lat),
        grid=(bsz, n_tiles),
        in_specs=[seq(gate_blk0), seq(rec_blk0), vec(LRU_CONV), vec(1), mat, vec(2), mat, vec(2), vec(2)],
        out_specs=pl.BlockSpec((1, n_all, LANES), lambda b, j: (b, 0, j)),
        out_shape=jax.ShapeDtypeStruct((bsz, n_all, LRU_WIDTH), BF16),
        scratch_shapes=[pltpu.VMEM((n_lat + 2 * LRU_CONV_MARGIN, LANES), F32),
                        full, full, full, full, full, full],
        compiler_params=pltpu.CompilerParams(
            dimension_semantics=("parallel", "parallel"),
            vmem_limit_bytes=VMEM_LIMIT_BYTES),
        name="rglru",
    )(u_all, u_all, conv_w, conv_b.reshape(1, LRU_WIDTH), _block_diag_tiles(wa), ba,
      _block_diag_tiles(wx), bx, decay_scale)


ATT_Q_TILE = 256


def _attn_body(q_ref, k_ref, v_ref, lv_ref, g_ref, o_ref, *, lam_init):
    lv = lv_ref[...]
    lam = (jnp.exp(jnp.sum(lv[0:1] * lv[1:2], keepdims=True))
           - jnp.exp(jnp.sum(lv[2:3] * lv[3:4], keepdims=True)) + lam_init)
    q = q_ref[0]
    k = k_ref[0]
    v = v_ref[0]
    outs = []
    for m in range(2):
        qm = q[:, m * ATT_HEAD_DIM:(m + 1) * ATT_HEAD_DIM]
        km = k[:, m * ATT_HEAD_DIM:(m + 1) * ATT_HEAD_DIM]
        s = lax.dot_general(qm, km, (((1,), (1,)), ((), ())), preferred_element_type=F32)
        p = jnp.exp(s - jnp.max(s, axis=-1, keepdims=True))
        denom = jnp.sum(p, axis=-1, keepdims=True)
        o = jnp.dot(p.astype(BF16), v, preferred_element_type=F32)
        outs.append(o / denom)
    o = outs[0] - lam * outs[1]
    y = o * lax.rsqrt(jnp.mean(o * o, axis=-1, keepdims=True) + LN_EPS) * g_ref[...] * (1.0 - lam_init)
    o_ref[0] = y.astype(o_ref.dtype)


def _diff_attention(qkv, lam_vecs, gain, lam_init, q_row0, n_q, n_k):
    bsz = qkv.shape[0]
    tq = ATT_Q_TILE
    q_blk0 = q_row0 // tq
    n_heads = ATT_HEADS
    return pl.pallas_call(
        partial(_attn_body, lam_init=lam_init),
        grid=(bsz, n_heads, n_q // tq),
        in_specs=[
            pl.BlockSpec((1, tq, ATT_V_DIM), lambda b, h, i: (b, q_blk0 + i, h)),
            pl.BlockSpec((1, n_k, ATT_V_DIM), lambda b, h, i: (b, 0, n_heads + h)),
            pl.BlockSpec((1, n_k, ATT_V_DIM), lambda b, h, i: (b, 0, 2 * n_heads + h)),
            pl.BlockSpec((4, ATT_HEAD_DIM), lambda b, h, i: (0, 0)),
            pl.BlockSpec((1, ATT_V_DIM), lambda b, h, i: (0, 0)),
        ],
        out_specs=pl.BlockSpec((1, tq, ATT_V_DIM), lambda b, h, i: (b, i, h)),
        out_shape=jax.ShapeDtypeStruct((bsz, n_q, ATT_WIDTH), BF16),
        compiler_params=pltpu.CompilerParams(
            dimension_semantics=("parallel", "parallel", "arbitrary"),
            vmem_limit_bytes=VMEM_LIMIT_BYTES),
        name="diff_attention",
    )(qkv, qkv, qkv, lam_vecs, gain.reshape(1, ATT_V_DIM))


def _axial_rope_tables(n):
    rows = n // GRID_W
    row = jnp.broadcast_to(jnp.arange(rows, dtype=F32)[:, None], (rows, GRID_W)).reshape(-1)
    col = jnp.broadcast_to(jnp.arange(GRID_W, dtype=F32)[None, :], (rows, GRID_W)).reshape(-1)
    inv = ROPE_THETA ** (-jnp.arange(0, ROPE_AXIS_DIM, 2, dtype=F32) / ROPE_AXIS_DIM)
    ang = jnp.stack([row[:, None] * inv, col[:, None] * inv], axis=1)
    return jnp.cos(ang), jnp.sin(ang)


def _apply_axial_rope(x, cos, sin):
    xs = x.reshape(x.shape[:-1] + (2, 2, ROPE_AXIS_DIM // 2))
    x1, x2 = xs[..., 0, :], xs[..., 1, :]
    cb, sb = cos[None, :, None, None], sin[None, :, None, None]
    out = jnp.stack([x1 * cb - x2 * sb, x2 * cb + x1 * sb], axis=-2)
    return out.reshape(x.shape)


def _diff_attention_mixer(u_att, n_ctx, lam_vecs, subln_gain, lam_init, need_ctx):
    bsz, n_all, _ = u_att.shape
    n_lat = n_all - n_ctx
    scale = ATT_HEAD_DIM ** -0.5
    u_lat = u_att[:, n_ctx:]
    u_ctx = u_att[:, :n_ctx]
    q_l = u_lat[..., :ATT_QK].reshape(bsz, n_lat, ATT_HEADS, 2, ATT_HEAD_DIM)
    k_l = u_lat[..., ATT_QK:2 * ATT_QK].reshape(bsz, n_lat, ATT_HEADS, 2, ATT_HEAD_DIM)
    cos, sin = _axial_rope_tables(n_lat)
    q_l = _apply_axial_rope(q_l, cos, sin) * scale
    k_l = _apply_axial_rope(k_l, cos, sin)
    lat = jnp.concatenate([q_l.reshape(bsz, n_lat, ATT_QK), k_l.reshape(bsz, n_lat, ATT_QK),
                           u_lat[..., 2 * ATT_QK:]], axis=-1).astype(BF16)
    cx = jnp.concatenate([u_ctx[..., :ATT_QK] * scale, u_ctx[..., ATT_QK:]], axis=-1).astype(BF16)
    qkv = jnp.concatenate([cx, lat], axis=1)
    y_lat = _diff_attention(qkv, lam_vecs, subln_gain, lam_init, n_ctx, n_lat, n_all)
    y_ctx = _diff_attention(qkv, lam_vecs, subln_gain, lam_init, 0, n_ctx, n_ctx) if need_ctx else None
    return y_lat, y_ctx


MOE_BLOCK_ROWS = 256
MOE_VMEM_LIMIT_BYTES = 56 * 1024 * 1024


def _moe_body(be_ref, nu_ref, idx_ref, idx_nxt_ref, wg_ref, wu_ref, wd_ref, rw_ref, h_hbm, o_hbm,
              xbuf, ybuf, wg_s, wu_s, wd_s, gsem, ssem, *, bm):
    i = pl.program_id(0)
    n_used = nu_ref[0]
    slot = i % 2

    def gather_rows(tok_ref, dst_slot):
        def issue(r, carry):
            pltpu.make_async_copy(h_hbm.at[pl.ds(tok_ref[0, 0, r], 1)], xbuf.at[dst_slot, pl.ds(r, 1)],
                                  gsem.at[dst_slot]).start()
            return carry
        lax.fori_loop(0, bm, issue, 0, unroll=8)

    def wait_rows(buf, sem, s):
        pltpu.make_async_copy(buf.at[s], buf.at[s], sem.at[s]).wait()

    @pl.when(i == 0)
    def _():
        gather_rows(idx_ref, 0)

    @pl.when(i + 1 < n_used)
    def _():
        gather_rows(idx_nxt_ref, 1 - slot)

    e = be_ref[i]
    e_prev = be_ref[jnp.maximum(i - 1, 0)]

    @pl.when((i < n_used) & ((i == 0) | (e != e_prev)))
    def _():
        wg_s[...] = wg_ref[0, 0].astype(BF16)
        wu_s[...] = wu_ref[0, 0].astype(BF16)
        wd_s[...] = wd_ref[0, 0].astype(BF16)

    @pl.when(i < n_used)
    def _():
        wait_rows(xbuf, gsem, slot)

        @pl.when(i >= 2)
        def _():
            wait_rows(ybuf, ssem, slot)

        xb = xbuf[slot].astype(BF16)
        g = jnp.dot(xb, wg_s[...], preferred_element_type=F32)
        u = jnp.dot(xb, wu_s[...], preferred_element_type=F32)
        act = (g * jax.nn.sigmoid(g) * u).astype(BF16)
        ybuf[slot] = jnp.dot(act, wd_s[...], preferred_element_type=F32) * rw_ref[...]

        def issue(r, carry):
            pltpu.make_async_copy(ybuf.at[slot, pl.ds(r, 1)], o_hbm.at[pl.ds(idx_ref[0, 0, bm + r], 1)],
                                  ssem.at[slot]).start()
            return carry
        lax.fori_loop(0, bm, issue, 0, unroll=8)

        @pl.when(i == n_used - 1)
        def _():
            @pl.when(i >= 1)
            def _():
                wait_rows(ybuf, ssem, 1 - slot)
            wait_rows(ybuf, ssem, slot)

    @pl.when(i >= n_used)
    def _():
        xbuf[0] = jnp.zeros(xbuf.shape[1:], F32)
        fill = pltpu.make_async_copy(xbuf.at[0], o_hbm.at[pl.ds(pl.multiple_of(i * bm, bm), bm)], gsem.at[0])
        fill.start()
        fill.wait()


def _routed_experts(h, idx, row_w, block_e, n_used, w_gate, w_up, w_down, layer):
    bm = MOE_BLOCK_ROWS
    n_blocks = idx.shape[0]
    d = h.shape[1]
    f = w_gate.shape[-1]
    grid_spec = pltpu.PrefetchScalarGridSpec(
        num_scalar_prefetch=2,
        grid=(n_blocks,),
        in_specs=[
            pl.BlockSpec((1, 1, 2 * bm), lambda i, be, nu: (i, 0, 0), memory_space=pltpu.SMEM),
            pl.BlockSpec((1, 1, 2 * bm), lambda i, be, nu: (jnp.minimum(i + 1, n_blocks - 1), 0, 0),
                         memory_space=pltpu.SMEM),
            pl.BlockSpec((1, 1, d, f), lambda i, be, nu: (layer, be[i], 0, 0)),
            pl.BlockSpec((1, 1, d, f), lambda i, be, nu: (layer, be[i], 0, 0)),
            pl.BlockSpec((1, 1, f, d), lambda i, be, nu: (layer, be[i], 0, 0)),
            pl.BlockSpec((bm, 1), lambda i, be, nu: (i, 0)),
            pl.BlockSpec(memory_space=pl.ANY),
        ],
        out_specs=pl.BlockSpec(memory_space=pl.ANY),
        scratch_shapes=[pltpu.VMEM((2, bm, d), F32), pltpu.VMEM((2, bm, d), F32),
                        pltpu.VMEM((d, f), BF16), pltpu.VMEM((d, f), BF16), pltpu.VMEM((f, d), BF16),
                        pltpu.SemaphoreType.DMA((2,)), pltpu.SemaphoreType.DMA((2,))],
    )
    return pl.pallas_call(
        partial(_moe_body, bm=bm),
        grid_spec=grid_spec,
        out_shape=jax.ShapeDtypeStruct((n_blocks * bm, d), F32),
        compiler_params=pltpu.CompilerParams(
            dimension_semantics=("arbitrary",),
            vmem_limit_bytes=MOE_VMEM_LIMIT_BYTES),
        name="routed_experts",
    )(block_e, n_used, idx, idx, w_gate, w_up, w_down, row_w, h)


def _route(scores, router_b):
    bm = MOE_BLOCK_ROWS
    n_tok = scores.shape[0]
    _, top_idx = lax.top_k(scores + router_b, TOP_K)
    top_s = jnp.take_along_axis(scores, top_idx, axis=1)
    top_w = (top_s / jnp.sum(top_s, axis=-1, keepdims=True) * ROUTED_SCALE).reshape(-1)
    n_assign = n_tok * TOP_K
    flat_e = top_idx.reshape(-1)
    order = jnp.argsort(flat_e).astype(jnp.int32)
    counts = jnp.sum(flat_e[None, :] == jnp.arange(N_EXPERTS)[:, None], axis=1).astype(jnp.int32)
    padded = (counts + bm - 1) // bm * bm
    padded_end = jnp.cumsum(padded)
    live_end = jnp.cumsum(counts)
    shift = padded_end - padded - (live_end - counts)
    n_blocks = -(-n_assign // bm) + N_EXPERTS
    block_e = jnp.minimum(
        jnp.searchsorted(padded_end, jnp.arange(n_blocks) * bm, side='right'), N_EXPERTS - 1).astype(jnp.int32)
    n_used = (padded_end[-1:] // bm).astype(jnp.int32)
    rows = jnp.arange(n_blocks * bm, dtype=jnp.int32).reshape(n_blocks, bm)
    row_slot = rows - shift[block_e][:, None]
    row_live = row_slot < live_end[block_e][:, None]
    row_assign = order[jnp.where(row_live, row_slot, 0)]
    row_tok = jnp.where(row_live, row_assign // TOP_K, 0)
    row_dst = jnp.where(row_live, row_assign, n_assign + rows - live_end[block_e][:, None])
    row_w = jnp.where(row_live, top_w[row_assign], 0.0)
    idx = jnp.concatenate([row_tok, row_dst], axis=1).astype(jnp.int32)[:, None, :]
    return idx, row_w.reshape(-1, 1), block_e, n_used


def _moe_ffn(h, layer, router_w, router_b, w_gate, w_up, w_down, sh_gate, sh_up, sh_down):
    n_tok, d = h.shape
    scores = jax.nn.sigmoid(jnp.dot(h, router_w, precision=lax.Precision.HIGHEST))
    idx, row_w, block_e, n_used = _route(scores, router_b)
    y = _routed_experts(h, idx, row_w, block_e, n_used, w_gate, w_up, w_down, layer)
    routed = jnp.sum(y[:n_tok * TOP_K].reshape(n_tok, TOP_K, d), axis=1)
    hb = h.astype(BF16)
    gu = _matmul(hb, jnp.concatenate([sh_gate, sh_up], axis=1).astype(BF16))
    d_sh = sh_gate.shape[1]
    act = jax.nn.silu(gu[:, :d_sh]) * gu[:, d_sh:]
    shared = _proj(act, sh_down)
    return routed + shared


def kernel(x, c, ctx, c_ctx, ada_w, ada_b, w_in, hy_conv_w, hy_conv_b, hy_w1, hy_b1, hy_w2, hy_b2, hy_w3, hy_freq, hy_skip, lru_conv_w, lru_conv_b, lru_wa, lru_ba, lru_wx, lru_bx, lru_lambda, att_lambda, att_subln, w_out, ln_g, ln_b, router_w, router_b, exp_w_gate, exp_w_up, exp_w_down, sh_w_gate, sh_w_up, sh_w_down):
    bsz, n_lat, d = x.shape
    n_ctx = ctx.shape[1]
    n_all = n_ctx + n_lat
    x_all = jnp.concatenate([ctx, x], axis=1)
    x_lat = x

    def per_row(v_ctx, v_lat):
        return jnp.concatenate([jnp.broadcast_to(v_ctx[None, None, :], (bsz, n_ctx, d)),
                                jnp.broadcast_to(v_lat[:, None, :], (bsz, n_lat, d))], axis=1)

    for l in range(DEPTH):
        need_ctx = l < DEPTH - 1
        lam_init = 0.8 - 0.6 * math.exp(-0.3 * l)
        cond = jnp.concatenate([c, c_ctx[None, :]], axis=0)
        mod = jnp.dot(jax.nn.silu(cond), ada_w[l], precision=lax.Precision.HIGHEST) + ada_b[l]
        sh1, sc1, g1, sh2, sc2, g2 = [per_row(m[bsz], m[:bsz]) for m in jnp.split(mod, 6, axis=-1)]
        hy_args = (hy_conv_w[l], hy_conv_b[l], hy_w1[l], hy_b1[l], hy_w2[l], hy_b2[l],
                   hy_w3[l], hy_freq[l], hy_skip[l])

        u_all = _proj(_modulate(x_all, sh1, sc1).reshape(-1, d), w_in[l]).reshape(bsz, n_all, -1)
        y_hy_l = _hyena_mixer(u_all[:, n_ctx:, :HY_IN], *hy_args)
        y_lru = _rglru_mixer(u_all, n_ctx, lru_conv_w[l], lru_conv_b[l], lru_wa[l], lru_ba[l],
                             lru_wx[l], lru_bx[l], lru_lambda[l])
        y_att_l, y_att_c = _diff_attention_mixer(u_all[..., HY_IN + LRU_IN:], n_ctx, att_lambda[l],
                                                 att_subln[l], lam_init, need_ctx)
        if need_ctx:
            y_hy = jnp.concatenate([_hyena_mixer(u_all[:, :n_ctx, :HY_IN], *hy_args), y_hy_l], axis=1)
            y_att = jnp.concatenate([y_att_c, y_att_l], axis=1)
            mixed = jnp.concatenate([y_hy.astype(BF16), y_lru, y_att], axis=-1)
            mix = _matmul(mixed.reshape(-1, d), w_out[l].astype(BF16)).reshape(bsz, n_all, d)
            x_all = _layer_norm(DEEPNORM_ALPHA * x_all + g1 * mix, ln_g[l, 0], ln_b[l, 0])
            tokens = _modulate(x_all, sh2, sc2)
        else:
            mixed = jnp.concatenate([y_hy_l.astype(BF16), y_lru[:, n_ctx:], y_att_l], axis=-1)
            mix = _matmul(mixed.reshape(-1, d), w_out[l].astype(BF16)).reshape(bsz, n_lat, d)
            x_lat = _layer_norm(DEEPNORM_ALPHA * x_all[:, n_ctx:] + g1[:, n_ctx:] * mix, ln_g[l, 0], ln_b[l, 0])
            tokens = _modulate(x_lat, sh2[:, n_ctx:], sc2[:, n_ctx:])

        ffn = _moe_ffn(tokens.reshape(-1, d), l, router_w[l], router_b[l], exp_w_gate, exp_w_up, exp_w_down,
                       sh_w_gate[l], sh_w_up[l], sh_w_down[l]).reshape(tokens.shape)
        if need_ctx:
            x_all = _layer_norm(DEEPNORM_ALPHA * x_all + g2 * ffn, ln_g[l, 1], ln_b[l, 1])
        else:
            x_lat = _layer_norm(DEEPNORM_ALPHA * x_lat + g2[:, n_ctx:] * ffn, ln_g[l, 1], ln_b[l, 1])
    return x_lat
```

```python
import math
from functools import lru_cache, partial

import numpy as np
import jax
import jax.numpy as jnp
from jax import lax
from jax.experimental import pallas as pl
from jax.experimental.pallas import tpu as pltpu

D_MODEL = 2048
DEPTH = 2
GRID_W = 64
F32 = jnp.float32
BF16 = jnp.bfloat16
LANES = 128
SUBLANES = 8

HY_WIDTH = D_MODEL // 4
LRU_WIDTH = D_MODEL // 4
ATT_WIDTH = D_MODEL // 2
HY_ORDER = 2
HY_CONV = 3
HY_BANDS = 8
HY_TARGET = 1e-2
HY_MAX_DECAY = math.log(HY_TARGET) / 0.3
HY_MIN_DECAY = math.log(HY_TARGET) / 1.5
HY_IN = (HY_ORDER + 1) * HY_WIDTH

LRU_HEADS = 8
LRU_HEAD_DIM = LRU_WIDTH // LRU_HEADS
LRU_CONV = 4
LRU_C = 8.0
LRU_IN = 2 * LRU_WIDTH

ATT_HEADS = 4
ATT_V_DIM = ATT_WIDTH // ATT_HEADS
ATT_HEAD_DIM = ATT_V_DIM // 2
ATT_QK = ATT_HEADS * 2 * ATT_HEAD_DIM
ROPE_THETA = 10000.0
ROPE_AXIS_DIM = ATT_HEAD_DIM // 2

N_EXPERTS = 64
TOP_K = 8
ROUTED_SCALE = 2.5

LN_EPS = 1e-5
DEEPNORM_ALPHA = (2 * DEPTH) ** 0.25

VMEM_LIMIT_BYTES = 48 * 1024 * 1024


def _mm_body(a_ref, b_ref, o_ref):
    o_ref[...] = jnp.dot(a_ref[...], b_ref[...], preferred_element_type=F32)


def _matmul(a, b, tm=512, tn=512):
    m, k = a.shape
    n = b.shape[1]
    tm = min(tm, m)
    tn = min(tn, n)
    assert m % tm == 0 and n % tn == 0
    return pl.pallas_call(
        _mm_body,
        grid=(m // tm, n // tn),
        in_specs=[pl.BlockSpec((tm, k), lambda i, j: (i, 0)),
                  pl.BlockSpec((k, tn), lambda i, j: (0, j))],
        out_specs=pl.BlockSpec((tm, tn), lambda i, j: (i, j)),
        out_shape=jax.ShapeDtypeStruct((m, n), F32),
        compiler_params=pltpu.CompilerParams(
            dimension_semantics=("parallel", "arbitrary"),
            vmem_limit_bytes=VMEM_LIMIT_BYTES),
        name="dense_matmul",
    )(a, b)


def _proj(x, w):
    return _matmul(x.astype(BF16), w.astype(BF16))


def _layer_norm(x, gain=None, bias=None):
    mu = jnp.mean(x, axis=-1, keepdims=True)
    var = jnp.mean(jnp.square(x - mu), axis=-1, keepdims=True)
    y = (x - mu) * lax.rsqrt(var + LN_EPS)
    if gain is not None:
        y = y * gain + bias
    return y


def _modulate(x, shift, scale):
    return _layer_norm(x) * (1.0 + scale) + shift


def _depthwise_conv(x, w, b, pad_left, pad_right):
    y = lax.conv_general_dilated(
        x, w[:, None, :], window_strides=(1,),
        padding=[(pad_left, pad_right)], dimension_numbers=('NWC', 'WIO', 'NWC'),
        feature_group_count=x.shape[-1])
    return y + b


def _hyena_filter(n, w1, b1, w2, b2, w3, freq):
    pos = jnp.arange(n, dtype=F32)
    t = (pos / max(n - 1, 1))[:, None]
    bands = jnp.linspace(1e-4, HY_BANDS - 1, HY_BANDS, dtype=F32)
    ang = (2.0 * math.pi / n) * pos[:, None] * bands
    feats = jnp.concatenate([t, jnp.cos(ang), -jnp.sin(ang)], axis=-1)
    hp = lax.Precision.HIGHEST
    hid = jnp.sin(freq * (jnp.dot(feats, w1, precision=hp) + b1))
    hid = jnp.sin(freq * (jnp.dot(hid, w2, precision=hp) + b2))
    filt = jnp.dot(hid, w3, precision=hp).reshape(n, 2, HY_ORDER, HY_WIDTH)
    decay = jnp.abs(jnp.linspace(HY_MIN_DECAY, HY_MAX_DECAY, HY_WIDTH, dtype=F32))
    filt = filt * jnp.exp(-t * decay)[:, None, None, :]
    fwd, bwd = filt[:, 0], filt[:, 1]
    two_sided = jnp.concatenate([fwd, jnp.zeros_like(fwd[:1]), jnp.flip(bwd[1:], axis=0)], axis=0)
    return two_sided / jnp.sum(jnp.abs(two_sided), axis=0, keepdims=True)


DFT_BLOCK_BYTES = 4 * 1024 * 1024
DFT_MINOR_LEN = 128
DFT_SINGLE_LEVEL_MAX = 1024


def _bmm3_body(lh_ref, ll_ref, r_ref, o_ref, *, nb, shared):
    for b in range(nb):
        r = r_ref[b]
        r_hi = r.astype(BF16)
        r_lo = (r - r_hi.astype(F32)).astype(BF16)
        lh = lh_ref[0 if shared else b]
        ll = ll_ref[0 if shared else b]
        o_ref[b] = (jnp.dot(lh, r_hi, preferred_element_type=F32)
                    + jnp.dot(lh, r_lo, preferred_element_type=F32)
                    + jnp.dot(ll, r_hi, preferred_element_type=F32))


def _bmm3(l_hi, l_lo, r, nb):
    batch, k, c = r.shape
    m = l_hi.shape[1]
    shared = l_hi.shape[0] == 1
    nb = max(1, min(nb, batch, DFT_BLOCK_BYTES // (4 * c * max(k, m))))
    assert batch % nb == 0
    l_spec = pl.BlockSpec((1 if shared else nb, m, k), (lambda i: (0, 0, 0)) if shared else (lambda i: (i, 0, 0)))
    return pl.pallas_call(
        partial(_bmm3_body, nb=nb, shared=shared),
        grid=(batch // nb,),
        in_specs=[l_spec, l_spec, pl.BlockSpec((nb, k, c), lambda i: (i, 0, 0))],
        out_specs=pl.BlockSpec((nb, m, c), lambda i: (i, 0, 0)),
        out_shape=jax.ShapeDtypeStruct((batch, m, c), F32),
        compiler_params=pltpu.CompilerParams(dimension_semantics=("parallel",), vmem_limit_bytes=VMEM_LIMIT_BYTES),
        name="dft_matmul",
    )(l_hi, l_lo, r)


def _stack_complex(f):
    return np.concatenate([np.concatenate([f.real, -f.imag], axis=-1),
                           np.concatenate([f.imag, f.real], axis=-1)], axis=-2)


def _split_bf16(a):
    a = jnp.asarray(a, F32)
    hi = a.astype(BF16)
    return hi, (a - hi.astype(F32)).astype(BF16)


@lru_cache(maxsize=None)
def _dft_tables(n1_io, n1, n2):
    n = n1 * n2
    k1 = np.arange(n1)
    a = np.arange(n1_io)
    b = np.arange(n2)
    ang = -2.0 * np.pi * (((np.outer(k1, a) * n2)[None] + b[:, None, None] * k1[None, :, None]) % n) / n
    fa = np.exp(1j * ang)
    fb = np.exp(-2j * np.pi * (np.outer(b, b) % n2) / n2)
    return dict(
        fwd_a=_stack_complex(fa),
        fwd_a_real=np.concatenate([fa.real, fa.imag], axis=-2),
        fwd_b=_stack_complex(fb)[None],
        inv_b=_stack_complex(np.conj(fb).T)[None],
        inv_a=_stack_complex(np.conj(fa).transpose(0, 2, 1) / n),
    )


def _dft_split(length):
    if length <= DFT_SINGLE_LEVEL_MAX:
        return 1, length
    return length // DFT_MINOR_LEN, DFT_MINOR_LEN


def _swap_stage(x, n1, n2):
    c = x.shape[-1]
    return x.reshape(n2, 2, n1, c).transpose(2, 1, 0, 3).reshape(n1, 2 * n2, c)


def _dft_forward(rows, table, n1, n2, tabs):
    if n1 > 1:
        rows = _bmm3(*_split_bf16(tabs[table]), rows, 16)
    return _bmm3(*_split_bf16(tabs['fwd_b']), _swap_stage(rows, n1, n2), 8)


def _filter_spectrum(two_sided):
    length, c = two_sided.shape
    n1, n2 = _dft_split(length)
    tabs = _dft_tables(n1, n1, n2)
    rows = two_sided.reshape(n1, n2, c).transpose(1, 0, 2)
    if n1 == 1:
        rows = jnp.concatenate([rows, jnp.zeros_like(rows)], axis=1)
    return _dft_forward(rows, 'fwd_a_real', n1, n2, tabs)


def _long_conv_pair(z, spec):
    pair, n, c = z.shape
    assert pair == 2
    n1, n2 = _dft_split(2 * n)
    n1_io = max(n1 // 2, 1)
    tabs = _dft_tables(n1_io, n1, n2)
    if n1 > 1:
        rows = z.reshape(2, n1_io, n2, c).transpose(2, 0, 1, 3).reshape(n2, 2 * n1_io, c)
    else:
        rows = jnp.pad(z, ((0, 0), (0, n), (0, 0))).transpose(1, 0, 2)
    y = _dft_forward(rows, 'fwd_a', n1, n2, tabs)
    yr, yi = y[:, :n2], y[:, n2:]
    hr, hi = spec[:, :n2], spec[:, n2:]
    y = jnp.concatenate([yr * hr - yi * hi, yr * hi + yi * hr], axis=1)
    d = _bmm3(*_split_bf16(tabs['inv_b']), y, 8)
    d = _swap_stage(d, n2, n1)
    if n1 > 1:
        out = _bmm3(*_split_bf16(tabs['inv_a']), d, 16)
        return out.reshape(n2, 2, n1_io, c).transpose(1, 2, 0, 3).reshape(2, n, c)
    return (d.reshape(n2, 2, c).transpose(1, 0, 2) / (n1 * n2))[:, :n]


def _hyena_mixer(u, conv_w, conv_b, w1, b1, w2, b2, w3, freq, skip):
    n = u.shape[1]
    u = _depthwise_conv(u, conv_w, conv_b, HY_CONV // 2, HY_CONV - 1 - HY_CONV // 2)
    gates = (u[..., :HY_WIDTH], u[..., HY_WIDTH:2 * HY_WIDTH])
    z = u[..., 2 * HY_WIDTH:]
    two_sided = _hyena_filter(n, w1, b1, w2, b2, w3, freq)
    spec = _filter_spectrum(two_sided.reshape(2 * n, HY_ORDER * HY_WIDTH))
    for o in range(HY_ORDER):
        zc = _long_conv_pair(z, spec[..., o * HY_WIDTH:(o + 1) * HY_WIDTH])
        z = gates[o] * (zc + skip[o] * z)
    return z


LRU_CONV_MARGIN = SUBLANES
LRU_TIME_CHUNK = 1024


def _gelu_tanh(x):
    return 0.5 * x * (1.0 + jnp.tanh(math.sqrt(2.0 / math.pi) * (x + 0.044715 * (x * x * x))))


def _rglru_body(ug_ref, ur_ref, cw_ref, cb_ref, wa_ref, ba_ref, wx_ref, bx_ref, cl_ref, o_ref,
                pad_s, af_s, bf_s, ar_s, br_s, hf_s, hr_s, *, n_ctx, n_lat):
    a_s = (af_s, ar_s)
    b_s = (bf_s, br_s)
    mg = LRU_CONV_MARGIN
    cw = cw_ref[...]
    zeros_margin = jnp.zeros((mg, LANES), F32)
    for r0, n in ((0, n_ctx), (n_ctx, n_lat)):
        pad_s[0:mg, :] = zeros_margin
        pad_s[mg:mg + n, :] = ur_ref[0, r0:r0 + n, :]
        pad_s[mg + n:2 * mg + n, :] = zeros_margin
        chunk = min(n, LRU_TIME_CHUNK)
        for c0 in range(0, n, chunk):
            xr = cb_ref[...] + cw[0:1] * pad_s[mg + c0 - 2:mg + c0 - 2 + chunk, :]
            for j in range(1, LRU_CONV):
                xr = xr + cw[j:j + 1] * pad_s[mg + c0 + j - 2:mg + c0 + j - 2 + chunk, :]
            xb = xr.astype(BF16)
            for d in range(2):
                r = jax.nn.sigmoid(jnp.dot(xb, wa_ref[d, 0], preferred_element_type=F32) + ba_ref[d:d + 1, :])
                i = jax.nn.sigmoid(jnp.dot(xb, wx_ref[d, 0], preferred_element_type=F32) + bx_ref[d:d + 1, :])
                log_a = cl_ref[d:d + 1, :] * r
                a_s[d][r0 + c0:r0 + c0 + chunk, :] = jnp.exp(log_a)
                b_s[d][r0 + c0:r0 + c0 + chunk, :] = jnp.sqrt(1.0 - jnp.exp(2.0 * log_a)) * (i * xr)

    def scan_segment(r0, n, carry):
        def step(t, hc):
            hf, hr = hc
            tf = r0 + t
            tr = r0 + n - 1 - t
            hf = af_s[pl.ds(tf, 1), :] * hf + bf_s[pl.ds(tf, 1), :]
            hr = ar_s[pl.ds(tr, 1), :] * hr + br_s[pl.ds(tr, 1), :]
            hf_s[pl.ds(tf, 1), :] = hf
            hr_s[pl.ds(tr, 1), :] = hr
            return hf, hr
        return lax.fori_loop(0, n, step, carry, unroll=8)

    h0 = jnp.zeros((1, LANES), F32)
    carry = scan_segment(0, n_ctx, (h0, h0))
    scan_segment(n_ctx, n_lat, carry)
    o_ref[0] = ((hf_s[...] + hr_s[...]) * _gelu_tanh(ug_ref[0])).astype(o_ref.dtype)


def _block_diag_tiles(w):
    w = w.reshape(2, LRU_HEADS // 2, 2, LRU_HEAD_DIM, LRU_HEAD_DIM)
    z = jnp.zeros_like(w[:, :, 0])
    top = jnp.concatenate([w[:, :, 0], z], axis=-1)
    bot = jnp.concatenate([z, w[:, :, 1]], axis=-1)
    return jnp.concatenate([top, bot], axis=-2).astype(BF16)


def _rglru_mixer(u_all, n_ctx, conv_w, conv_b, wa, ba, wx, bx, lam):
    bsz, n_all, _ = u_all.shape
    n_lat = n_all - n_ctx
    n_tiles = LRU_WIDTH // LANES
    gate_blk0 = HY_IN // LANES
    rec_blk0 = (HY_IN + LRU_WIDTH) // LANES
    decay_scale = -LRU_C * jax.nn.softplus(-lam)
    seq = lambda col0: pl.BlockSpec((1, n_all, LANES), lambda b, j: (b, 0, col0 + j))
    vec = lambda rows: pl.BlockSpec((rows, LANES), lambda b, j: (0, j))
    mat = pl.BlockSpec((2, 1, LANES, LANES), lambda b, j: (0, j, 0, 0))
    full = pltpu.VMEM((n_all, LANES), F32)
    return pl.pallas_call(
        partial(_rglru_body, n_ctx=n_ctx, n_lat=n_lat),
        grid=(bsz, n_tiles),
        in_specs=[seq(gate_blk0), seq(rec_blk0), vec(LRU_CONV), vec(1), mat, vec(2), mat, vec(2), vec(2)],
        out_specs=pl.BlockSpec((1, n_all, LANES), lambda b, j: (b, 0, j)),
        out_shape=jax.ShapeDtypeStruct((bsz, n_all, LRU_WIDTH), BF16),
        scratch_shapes=[pltpu.VMEM((n_lat + 2 * LRU_CONV_MARGIN, LANES), F32),
                        full, full, full, full, full, full],
        compiler_params=pltpu.CompilerParams(
            dimension_semantics=("parallel", "parallel"),
            vmem_limit_bytes=VMEM_LIMIT_BYTES),
        name="rglru",
    )(u_all, u_all, conv_w, conv_b.reshape(1, LRU_WIDTH), _block_diag_tiles(wa), ba,
      _block_diag_tiles(wx), bx, decay_scale)


ATT_Q_TILE = 256


def _attn_body(q_ref, k_ref, v_ref, lv_ref, g_ref, o_ref, *, lam_init):
    lv = lv_ref[...]
    lam = (jnp.exp(jnp.sum(lv[0:1] * lv[1:2], keepdims=True))
           - jnp.exp(jnp.sum(lv[2:3] * lv[3:4], keepdims=True)) + lam_init)
    q = q_ref[0]
    k = k_ref[0]
    v = v_ref[0]
    outs = []
    for m in range(2):
        qm = q[:, m * ATT_HEAD_DIM:(m + 1) * ATT_HEAD_DIM]
        km = k[:, m * ATT_HEAD_DIM:(m + 1) * ATT_HEAD_DIM]
        s = lax.dot_general(qm, km, (((1,), (1,)), ((), ())), preferred_element_type=F32)
        p = jnp.exp(s - jnp.max(s, axis=-1, keepdims=True))
        denom = jnp.sum(p, axis=-1, keepdims=True)
        o = jnp.dot(p.astype(BF16), v, preferred_element_type=F32)
        outs.append(o / denom)
    o = outs[0] - lam * outs[1]
    y = o * lax.rsqrt(jnp.mean(o * o, axis=-1, keepdims=True) + LN_EPS) * g_ref[...] * (1.0 - lam_init)
    o_ref[0] = y.astype(o_ref.dtype)


def _diff_attention(qkv, lam_vecs, gain, lam_init, q_row0, n_q, n_k):
    bsz = qkv.shape[0]
    tq = ATT_Q_TILE
    q_blk0 = q_row0 // tq
    n_heads = ATT_HEADS
    return pl.pallas_call(
        partial(_attn_body, lam_init=lam_init),
        grid=(bsz, n_heads, n_q // tq),
        in_specs=[
            pl.BlockSpec((1, tq, ATT_V_DIM), lambda b, h, i: (b, q_blk0 + i, h)),
            pl.BlockSpec((1, n_k, ATT_V_DIM), lambda b, h, i: (b, 0, n_heads + h)),
            pl.BlockSpec((1, n_k, ATT_V_DIM), lambda b, h, i: (b, 0, 2 * n_heads + h)),
            pl.BlockSpec((4, ATT_HEAD_DIM), lambda b, h, i: (0, 0)),
            pl.BlockSpec((1, ATT_V_DIM), lambda b, h, i: (0, 0)),
        ],
        out_specs=pl.BlockSpec((1, tq, ATT_V_DIM), lambda b, h, i: (b, i, h)),
        out_shape=jax.ShapeDtypeStruct((bsz, n_q, ATT_WIDTH), BF16),
        compiler_params=pltpu.CompilerParams(
            dimension_semantics=("parallel", "parallel", "arbitrary"),
            vmem_limit_bytes=VMEM_LIMIT_BYTES),
        name="diff_attention",
    )(qkv, qkv, qkv, lam_vecs, gain.reshape(1, ATT_V_DIM))


def _axial_rope_tables(n):
    rows = n // GRID_W
    row = jnp.broadcast_to(jnp.arange(rows, dtype=F32)[:, None], (rows, GRID_W)).reshape(-1)
    col = jnp.broadcast_to(jnp.arange(GRID_W, dtype=F32)[None, :], (rows, GRID_W)).reshape(-1)
    inv = ROPE_THETA ** (-jnp.arange(0, ROPE_AXIS_DIM, 2, dtype=F32) / ROPE_AXIS_DIM)
    ang = jnp.stack([row[:, None] * inv, col[:, None] * inv], axis=1)
    return jnp.cos(ang), jnp.sin(ang)


def _apply_axial_rope(x, cos, sin):
    xs = x.reshape(x.shape[:-1] + (2, 2, ROPE_AXIS_DIM // 2))
    x1, x2 = xs[..., 0, :], xs[..., 1, :]
    cb, sb = cos[None, :, None, None], sin[None, :, None, None]
    out = jnp.stack([x1 * cb - x2 * sb, x2 * cb + x1 * sb], axis=-2)
    return out.reshape(x.shape)


def _diff_attention_mixer(u_att, n_ctx, lam_vecs, subln_gain, lam_init, need_ctx):
    bsz, n_all, _ = u_att.shape
    n_lat = n_all - n_ctx
    scale = ATT_HEAD_DIM ** -0.5
    u_lat = u_att[:, n_ctx:]
    u_ctx = u_att[:, :n_ctx]
    q_l = u_lat[..., :ATT_QK].reshape(bsz, n_lat, ATT_HEADS, 2, ATT_HEAD_DIM)
    k_l = u_lat[..., ATT_QK:2 * ATT_QK].reshape(bsz, n_lat, ATT_HEADS, 2, ATT_HEAD_DIM)
    cos, sin = _axial_rope_tables(n_lat)
    q_l = _apply_axial_rope(q_l, cos, sin) * scale
    k_l = _apply_axial_rope(k_l, cos, sin)
    lat = jnp.concatenate([q_l.reshape(bsz, n_lat, ATT_QK), k_l.reshape(bsz, n_lat, ATT_QK),
                           u_lat[..., 2 * ATT_QK:]], axis=-1).astype(BF16)
    cx = jnp.concatenate([u_ctx[..., :ATT_QK] * scale, u_ctx[..., ATT_QK:]], axis=-1).astype(BF16)
    qkv = jnp.concatenate([cx, lat], axis=1)
    y_lat = _diff_attention(qkv, lam_vecs, subln_gain, lam_init, n_ctx, n_lat, n_all)
    y_ctx = _diff_attention(qkv, lam_vecs, subln_gain, lam_init, 0, n_ctx, n_ctx) if need_ctx else None
    return y_lat, y_ctx


MOE_BLOCK_ROWS = 256
MOE_VMEM_LIMIT_BYTES = 56 * 1024 * 1024


MOE_TOKEN_PITCH_PAD = 4


def _moe_body(be_ref, nu_ref, idx_ref, idx_nxt_ref, wg_ref, wu_ref, wd_ref, rw_ref, h_hbm, o_hbm,
              xbuf, ybuf, wg_s, wu_s, wd_s, gsem, ssem, *, bm, tok_rows):
    i = pl.program_id(0)
    n_used = nu_ref[0]
    slot = i % 2
    pitch = tok_rows + MOE_TOKEN_PITCH_PAD
    blk_rows = bm * tok_rows

    def gather_rows(tok_ref, dst_slot):
        for r in range(bm):
            src0 = pl.multiple_of(tok_ref[0, 0, r] * tok_rows, tok_rows)
            pltpu.make_async_copy(h_hbm.at[pl.ds(src0, tok_rows)], xbuf.at[dst_slot, pl.ds(r * pitch, tok_rows)],
                                  gsem.at[dst_slot]).start()

    def wait_gather(s):
        pltpu.make_async_copy(h_hbm.at[pl.ds(0, blk_rows)], xbuf.at[s, pl.ds(0, blk_rows)], gsem.at[s]).wait()

    def wait_scatter(s):
        pltpu.make_async_copy(ybuf.at[s, pl.ds(0, blk_rows)], o_hbm.at[pl.ds(0, blk_rows)], ssem.at[s]).wait()

    @pl.when(i == 0)
    def _():
        gather_rows(idx_ref, 0)

    e = be_ref[i]
    e_prev = be_ref[jnp.maximum(i - 1, 0)]

    @pl.when((i < n_used) & ((i == 0) | (e != e_prev)))
    def _():
        wg_s[...] = wg_ref[0, 0].astype(BF16)
        wu_s[...] = wu_ref[0, 0].astype(BF16)
        wd_s[...] = wd_ref[0, 0].astype(BF16)

    @pl.when(i < n_used)
    def _():
        wait_gather(slot)

        @pl.when(i >= 2)
        def _():
            wait_scatter(slot)

        gather_rows(idx_nxt_ref, 1 - slot)

        xb = jnp.concatenate([xbuf[slot, pl.ds(j, bm, stride=pitch), :] for j in range(tok_rows)],
                             axis=1).astype(BF16)
        g = jnp.dot(xb, wg_s[...], preferred_element_type=F32)
        u = jnp.dot(xb, wu_s[...], preferred_element_type=F32)
        act = (g * jax.nn.sigmoid(g) * u).astype(BF16)
        y = jnp.dot(act, wd_s[...], preferred_element_type=F32) * rw_ref[...]
        for j in range(tok_rows):
            ybuf[slot, pl.ds(j, bm, stride=pitch), :] = y[:, j * LANES:(j + 1) * LANES]
        for r in range(bm):
            dst0 = pl.multiple_of(idx_ref[0, 0, bm + r] * tok_rows, tok_rows)
            pltpu.make_async_copy(ybuf.at[slot, pl.ds(r * pitch, tok_rows)], o_hbm.at[pl.ds(dst0, tok_rows)],
                                  ssem.at[slot]).start()

        @pl.when(i == n_used - 1)
        def _():
            @pl.when(i >= 1)
            def _():
                wait_scatter(1 - slot)
            wait_scatter(slot)
            wait_gather(1 - slot)

    @pl.when(i >= n_used)
    def _():
        xbuf[0, pl.ds(0, blk_rows), :] = jnp.zeros((blk_rows, LANES), F32)
        fill = pltpu.make_async_copy(xbuf.at[0, pl.ds(0, blk_rows)],
                                     o_hbm.at[pl.ds(pl.multiple_of(i * blk_rows, blk_rows), blk_rows)], gsem.at[0])
        fill.start()
        fill.wait()


def _routed_experts(h, idx, row_w, block_e, n_used, w_gate, w_up, w_down, layer):
    bm = MOE_BLOCK_ROWS
    n_blocks = idx.shape[0]
    d = h.shape[1]
    f = w_gate.shape[-1]
    tok_rows = d // LANES
    pitch = tok_rows + MOE_TOKEN_PITCH_PAD
    grid_spec = pltpu.PrefetchScalarGridSpec(
        num_scalar_prefetch=2,
        grid=(n_blocks,),
        in_specs=[
            pl.BlockSpec((1, 1, 2 * bm), lambda i, be, nu: (i, 0, 0), memory_space=pltpu.SMEM),
            pl.BlockSpec((1, 1, 2 * bm), lambda i, be, nu: (jnp.minimum(i + 1, n_blocks - 1), 0, 0),
                         memory_space=pltpu.SMEM),
            pl.BlockSpec((1, 1, d, f), lambda i, be, nu: (layer, be[i], 0, 0)),
            pl.BlockSpec((1, 1, d, f), lambda i, be, nu: (layer, be[i], 0, 0)),
            pl.BlockSpec((1, 1, f, d), lambda i, be, nu: (layer, be[i], 0, 0)),
            pl.BlockSpec((bm, 1), lambda i, be, nu: (i, 0)),
            pl.BlockSpec(memory_space=pl.ANY),
        ],
        out_specs=pl.BlockSpec(memory_space=pl.ANY),
        scratch_shapes=[pltpu.VMEM((2, bm * pitch, LANES), F32), pltpu.VMEM((2, bm * pitch, LANES), F32),
                        pltpu.VMEM((d, f), BF16), pltpu.VMEM((d, f), BF16), pltpu.VMEM((f, d), BF16),
                        pltpu.SemaphoreType.DMA((2,)), pltpu.SemaphoreType.DMA((2,))],
    )
    out = pl.pallas_call(
        partial(_moe_body, bm=bm, tok_rows=tok_rows),
        grid_spec=grid_spec,
        out_shape=jax.ShapeDtypeStruct((n_blocks * bm * tok_rows, LANES), F32),
        compiler_params=pltpu.CompilerParams(
            dimension_semantics=("arbitrary",),
            vmem_limit_bytes=MOE_VMEM_LIMIT_BYTES),
        name="routed_experts",
    )(block_e, n_used, idx, idx, w_gate, w_up, w_down, row_w, h.reshape(-1, LANES))
    return out.reshape(n_blocks * bm, d)


def _route(scores, router_b):
    bm = MOE_BLOCK_ROWS
    n_tok = scores.shape[0]
    _, top_idx = lax.top_k(scores + router_b, TOP_K)
    top_s = jnp.take_along_axis(scores, top_idx, axis=1)
    top_w = (top_s / jnp.sum(top_s, axis=-1, keepdims=True) * ROUTED_SCALE).reshape(-1)
    n_assign = n_tok * TOP_K
    flat_e = top_idx.reshape(-1)
    order = jnp.argsort(flat_e).astype(jnp.int32)
    counts = jnp.sum(flat_e[None, :] == jnp.arange(N_EXPERTS)[:, None], axis=1).astype(jnp.int32)
    padded = (counts + bm - 1) // bm * bm
    padded_end = jnp.cumsum(padded)
    live_end = jnp.cumsum(counts)
    shift = padded_end - padded - (live_end - counts)
    n_blocks = -(-n_assign // bm) + N_EXPERTS
    block_e = jnp.minimum(
        jnp.searchsorted(padded_end, jnp.arange(n_blocks) * bm, side='right'), N_EXPERTS - 1).astype(jnp.int32)
    n_used = (padded_end[-1:] // bm).astype(jnp.int32)
    rows = jnp.arange(n_blocks * bm, dtype=jnp.int32).reshape(n_blocks, bm)
    row_slot = rows - shift[block_e][:, None]
    row_live = row_slot < live_end[block_e][:, None]
    row_assign = order[jnp.where(row_live, row_slot, 0)]
    row_tok = jnp.where(row_live, row_assign // TOP_K, 0)
    row_dst = jnp.where(row_live, row_assign, n_assign + rows - live_end[block_e][:, None])
    row_w = jnp.where(row_live, top_w[row_assign], 0.0)
    idx = jnp.concatenate([row_tok, row_dst], axis=1).astype(jnp.int32)[:, None, :]
    return idx, row_w.reshape(-1, 1), block_e, n_used


def _moe_ffn(h, layer, router_w, router_b, w_gate, w_up, w_down, sh_gate, sh_up, sh_down):
    n_tok, d = h.shape
    scores = jax.nn.sigmoid(jnp.dot(h, router_w, precision=lax.Precision.HIGHEST))
    idx, row_w, block_e, n_used = _route(scores, router_b)
    y = _routed_experts(h, idx, row_w, block_e, n_used, w_gate, w_up, w_down, layer)
    routed = jnp.sum(y[:n_tok * TOP_K].reshape(n_tok, TOP_K, d), axis=1)
    hb = h.astype(BF16)
    gu = _matmul(hb, jnp.concatenate([sh_gate, sh_up], axis=1).astype(BF16))
    d_sh = sh_gate.shape[1]
    act = jax.nn.silu(gu[:, :d_sh]) * gu[:, d_sh:]
    shared = _proj(act, sh_down)
    return routed + shared


def kernel(x, c, ctx, c_ctx, ada_w, ada_b, w_in, hy_conv_w, hy_conv_b, hy_w1, hy_b1, hy_w2, hy_b2, hy_w3, hy_freq, hy_skip, lru_conv_w, lru_conv_b, lru_wa, lru_ba, lru_wx, lru_bx, lru_lambda, att_lambda, att_subln, w_out, ln_g, ln_b, router_w, router_b, exp_w_gate, exp_w_up, exp_w_down, sh_w_gate, sh_w_up, sh_w_down):
    bsz, n_lat, d = x.shape
    n_ctx = ctx.shape[1]
    n_all = n_ctx + n_lat
    x_all = jnp.concatenate([ctx, x], axis=1)
    x_lat = x

    def per_row(v_ctx, v_lat):
        return jnp.concatenate([jnp.broadcast_to(v_ctx[None, None, :], (bsz, n_ctx, d)),
                                jnp.broadcast_to(v_lat[:, None, :], (bsz, n_lat, d))], axis=1)

    for l in range(DEPTH):
        need_ctx = l < DEPTH - 1
        lam_init = 0.8 - 0.6 * math.exp(-0.3 * l)
        cond = jnp.concatenate([c, c_ctx[None, :]], axis=0)
        mod = jnp.dot(jax.nn.silu(cond), ada_w[l], precision=lax.Precision.HIGHEST) + ada_b[l]
        sh1, sc1, g1, sh2, sc2, g2 = [per_row(m[bsz], m[:bsz]) for m in jnp.split(mod, 6, axis=-1)]
        hy_args = (hy_conv_w[l], hy_conv_b[l], hy_w1[l], hy_b1[l], hy_w2[l], hy_b2[l],
                   hy_w3[l], hy_freq[l], hy_skip[l])

        u_all = _proj(_modulate(x_all, sh1, sc1).reshape(-1, d), w_in[l]).reshape(bsz, n_all, -1)
        y_hy_l = _hyena_mixer(u_all[:, n_ctx:, :HY_IN], *hy_args)
        y_lru = _rglru_mixer(u_all, n_ctx, lru_conv_w[l], lru_conv_b[l], lru_wa[l], lru_ba[l],
                             lru_wx[l], lru_bx[l], lru_lambda[l])
        y_att_l, y_att_c = _diff_attention_mixer(u_all[..., HY_IN + LRU_IN:], n_ctx, att_lambda[l],
                                                 att_subln[l], lam_init, need_ctx)
        if need_ctx:
            y_hy = jnp.concatenate([_hyena_mixer(u_all[:, :n_ctx, :HY_IN], *hy_args), y_hy_l], axis=1)
            y_att = jnp.concatenate([y_att_c, y_att_l], axis=1)
            mixed = jnp.concatenate([y_hy.astype(BF16), y_lru, y_att], axis=-1)
            mix = _matmul(mixed.reshape(-1, d), w_out[l].astype(BF16)).reshape(bsz, n_all, d)
            x_all = _layer_norm(DEEPNORM_ALPHA * x_all + g1 * mix, ln_g[l, 0], ln_b[l, 0])
            tokens = _modulate(x_all, sh2, sc2)
        else:
            mixed = jnp.concatenate([y_hy_l.astype(BF16), y_lru[:, n_ctx:], y_att_l], axis=-1)
            mix = _matmul(mixed.reshape(-1, d), w_out[l].astype(BF16)).reshape(bsz, n_lat, d)
            x_lat = _layer_norm(DEEPNORM_ALPHA * x_all[:, n_ctx:] + g1[:, n_ctx:] * mix, ln_g[l, 0], ln_b[l, 0])
            tokens = _modulate(x_lat, sh2[:, n_ctx:], sc2[:, n_ctx:])

        ffn = _moe_ffn(tokens.reshape(-1, d), l, router_w[l], router_b[l], exp_w_gate, exp_w_up, exp_w_down,
                       sh_w_gate[l], sh_w_up[l], sh_w_down[l]).reshape(tokens.shape)
        if need_ctx:
            x_all = _layer_norm(DEEPNORM_ALPHA * x_all + g2 * ffn, ln_g[l, 1], ln_b[l, 1])
        else:
            x_lat = _layer_norm(DEEPNORM_ALPHA * x_lat + g2[:, n_ctx:] * ffn, ln_g[l, 1], ln_b[l, 1])
    return x_lat
```

```python
import math
from functools import lru_cache, partial

import numpy as np
import jax
import jax.numpy as jnp
from jax import lax
from jax.experimental import pallas as pl
from jax.experimental.pallas import tpu as pltpu

D_MODEL = 2048
DEPTH = 2
GRID_W = 64
F32 = jnp.float32
BF16 = jnp.bfloat16
LANES = 128
SUBLANES = 8

HY_WIDTH = D_MODEL // 4
LRU_WIDTH = D_MODEL // 4
ATT_WIDTH = D_MODEL // 2
HY_ORDER = 2
HY_CONV = 3
HY_BANDS = 8
HY_TARGET = 1e-2
HY_MAX_DECAY = math.log(HY_TARGET) / 0.3
HY_MIN_DECAY = math.log(HY_TARGET) / 1.5
HY_IN = (HY_ORDER + 1) * HY_WIDTH

LRU_HEADS = 8
LRU_HEAD_DIM = LRU_WIDTH // LRU_HEADS
LRU_CONV = 4
LRU_C = 8.0
LRU_IN = 2 * LRU_WIDTH

ATT_HEADS = 4
ATT_V_DIM = ATT_WIDTH // ATT_HEADS
ATT_HEAD_DIM = ATT_V_DIM // 2
ATT_QK = ATT_HEADS * 2 * ATT_HEAD_DIM
ROPE_THETA = 10000.0
ROPE_AXIS_DIM = ATT_HEAD_DIM // 2

N_EXPERTS = 64
TOP_K = 8
ROUTED_SCALE = 2.5

LN_EPS = 1e-5
DEEPNORM_ALPHA = (2 * DEPTH) ** 0.25

VMEM_LIMIT_BYTES = 48 * 1024 * 1024


def _depthwise_conv(x, w, b, pad_left, pad_right):
    y = lax.conv_general_dilated(
        x, w[:, None, :], window_strides=(1,),
        padding=[(pad_left, pad_right)], dimension_numbers=('NWC', 'WIO', 'NWC'),
        feature_group_count=x.shape[-1])
    return y + b


def _hyena_filter(n, w1, b1, w2, b2, w3, freq):
    pos = jnp.arange(n, dtype=F32)
    t = (pos / max(n - 1, 1))[:, None]
    bands = jnp.linspace(1e-4, HY_BANDS - 1, HY_BANDS, dtype=F32)
    ang = (2.0 * math.pi / n) * pos[:, None] * bands
    feats = jnp.concatenate([t, jnp.cos(ang), -jnp.sin(ang)], axis=-1)
    hp = lax.Precision.HIGHEST
    hid = jnp.sin(freq * (jnp.dot(feats, w1, precision=hp) + b1))
    hid = jnp.sin(freq * (jnp.dot(hid, w2, precision=hp) + b2))
    filt = jnp.dot(hid, w3, precision=hp).reshape(n, 2, HY_ORDER, HY_WIDTH)
    decay = jnp.abs(jnp.linspace(HY_MIN_DECAY, HY_MAX_DECAY, HY_WIDTH, dtype=F32))
    filt = filt * jnp.exp(-t * decay)[:, None, None, :]
    fwd, bwd = filt[:, 0], filt[:, 1]
    two_sided = jnp.concatenate([fwd, jnp.zeros_like(fwd[:1]), jnp.flip(bwd[1:], axis=0)], axis=0)
    return two_sided / jnp.sum(jnp.abs(two_sided), axis=0, keepdims=True)


DFT_BLOCK_BYTES = 4 * 1024 * 1024
DFT_MINOR_LEN = 128
DFT_SINGLE_LEVEL_MAX = 1024


def _bmm3_body(lh_ref, ll_ref, r_ref, o_ref, *, nb, shared):
    for b in range(nb):
        r = r_ref[b]
        r_hi = r.astype(BF16)
        r_lo = (r - r_hi.astype(F32)).astype(BF16)
        lh = lh_ref[0 if shared else b]
        ll = ll_ref[0 if shared else b]
        o_ref[b] = (jnp.dot(lh, r_hi, preferred_element_type=F32)
                    + jnp.dot(lh, r_lo, preferred_element_type=F32)
                    + jnp.dot(ll, r_hi, preferred_element_type=F32))


def _bmm3(l_hi, l_lo, r, nb):
    batch, k, c = r.shape
    m = l_hi.shape[1]
    shared = l_hi.shape[0] == 1
    nb = max(1, min(nb, batch, DFT_BLOCK_BYTES // (4 * c * max(k, m))))
    assert batch % nb == 0
    l_spec = pl.BlockSpec((1 if shared else nb, m, k), (lambda i: (0, 0, 0)) if shared else (lambda i: (i, 0, 0)))
    return pl.pallas_call(
        partial(_bmm3_body, nb=nb, shared=shared),
        grid=(batch // nb,),
        in_specs=[l_spec, l_spec, pl.BlockSpec((nb, k, c), lambda i: (i, 0, 0))],
        out_specs=pl.BlockSpec((nb, m, c), lambda i: (i, 0, 0)),
        out_shape=jax.ShapeDtypeStruct((batch, m, c), F32),
        compiler_params=pltpu.CompilerParams(dimension_semantics=("parallel",), vmem_limit_bytes=VMEM_LIMIT_BYTES),
        name="dft_matmul",
    )(l_hi, l_lo, r)


def _stack_complex(f):
    return np.concatenate([np.concatenate([f.real, -f.imag], axis=-1),
                           np.concatenate([f.imag, f.real], axis=-1)], axis=-2)


def _split_bf16(a):
    a = jnp.asarray(a, F32)
    hi = a.astype(BF16)
    return hi, (a - hi.astype(F32)).astype(BF16)


@lru_cache(maxsize=None)
def _dft_tables(n1_io, n1, n2):
    n = n1 * n2
    k1 = np.arange(n1)
    a = np.arange(n1_io)
    b = np.arange(n2)
    ang = -2.0 * np.pi * (((np.outer(k1, a) * n2)[None] + b[:, None, None] * k1[None, :, None]) % n) / n
    fa = np.exp(1j * ang)
    fb = np.exp(-2j * np.pi * (np.outer(b, b) % n2) / n2)
    return dict(
        fwd_a=_stack_complex(fa),
        fwd_a_real=np.concatenate([fa.real, fa.imag], axis=-2),
        fwd_b=_stack_complex(fb)[None],
        inv_b=_stack_complex(np.conj(fb).T)[None],
        inv_a=_stack_complex(np.conj(fa).transpose(0, 2, 1) / n),
    )


def _dft_split(length):
    if length <= DFT_SINGLE_LEVEL_MAX:
        return 1, length
    return length // DFT_MINOR_LEN, DFT_MINOR_LEN


def _swap_stage(x, n1, n2):
    c = x.shape[-1]
    return x.reshape(n2, 2, n1, c).transpose(2, 1, 0, 3).reshape(n1, 2 * n2, c)


def _dft_forward(rows, table, n1, n2, tabs):
    if n1 > 1:
        rows = _bmm3(*_split_bf16(tabs[table]), rows, 16)
    return _bmm3(*_split_bf16(tabs['fwd_b']), _swap_stage(rows, n1, n2), 8)


def _filter_spectrum(two_sided):
    length, c = two_sided.shape
    n1, n2 = _dft_split(length)
    tabs = _dft_tables(n1, n1, n2)
    rows = two_sided.reshape(n1, n2, c).transpose(1, 0, 2)
    if n1 == 1:
        rows = jnp.concatenate([rows, jnp.zeros_like(rows)], axis=1)
    return _dft_forward(rows, 'fwd_a_real', n1, n2, tabs)


def _long_conv_pair(z, spec):
    pair, n, c = z.shape
    assert pair == 2
    n1, n2 = _dft_split(2 * n)
    n1_io = max(n1 // 2, 1)
    tabs = _dft_tables(n1_io, n1, n2)
    if n1 > 1:
        rows = z.reshape(2, n1_io, n2, c).transpose(2, 0, 1, 3).reshape(n2, 2 * n1_io, c)
    else:
        rows = jnp.pad(z, ((0, 0), (0, n), (0, 0))).transpose(1, 0, 2)
    y = _dft_forward(rows, 'fwd_a', n1, n2, tabs)
    yr, yi = y[:, :n2], y[:, n2:]
    hr, hi = spec[:, :n2], spec[:, n2:]
    y = jnp.concatenate([yr * hr - yi * hi, yr * hi + yi * hr], axis=1)
    d = _bmm3(*_split_bf16(tabs['inv_b']), y, 8)
    d = _swap_stage(d, n2, n1)
    if n1 > 1:
        out = _bmm3(*_split_bf16(tabs['inv_a']), d, 16)
        return out.reshape(n2, 2, n1_io, c).transpose(1, 2, 0, 3).reshape(2, n, c)
    return (d.reshape(n2, 2, c).transpose(1, 0, 2) / (n1 * n2))[:, :n]


def _hyena_mixer(u, conv_w, conv_b, w1, b1, w2, b2, w3, freq, skip):
    n = u.shape[1]
    u = _depthwise_conv(u, conv_w, conv_b, HY_CONV // 2, HY_CONV - 1 - HY_CONV // 2)
    gates = (u[..., :HY_WIDTH], u[..., HY_WIDTH:2 * HY_WIDTH])
    z = u[..., 2 * HY_WIDTH:]
    two_sided = _hyena_filter(n, w1, b1, w2, b2, w3, freq)
    spec = _filter_spectrum(two_sided.reshape(2 * n, HY_ORDER * HY_WIDTH))
    for o in range(HY_ORDER):
        zc = _long_conv_pair(z, spec[..., o * HY_WIDTH:(o + 1) * HY_WIDTH])
        z = gates[o] * (zc + skip[o] * z)
    return z


LRU_CONV_MARGIN = SUBLANES
LRU_TIME_CHUNK = 1024


def _gelu_tanh(x):
    return 0.5 * x * (1.0 + jnp.tanh(math.sqrt(2.0 / math.pi) * (x + 0.044715 * (x * x * x))))


def _rglru_body(ug_ref, ur_ref, cw_ref, cb_ref, wa_ref, ba_ref, wx_ref, bx_ref, cl_ref, o_ref,
                pad_s, af_s, bf_s, ar_s, br_s, hf_s, hr_s, *, n_ctx, n_lat):
    a_s = (af_s, ar_s)
    b_s = (bf_s, br_s)
    mg = LRU_CONV_MARGIN
    cw = cw_ref[...]
    zeros_margin = jnp.zeros((mg, LANES), F32)
    for r0, n in ((0, n_ctx), (n_ctx, n_lat)):
        pad_s[0:mg, :] = zeros_margin
        pad_s[mg:mg + n, :] = ur_ref[0, r0:r0 + n, :]
        pad_s[mg + n:2 * mg + n, :] = zeros_margin
        chunk = min(n, LRU_TIME_CHUNK)
        for c0 in range(0, n, chunk):
            xr = cb_ref[...] + cw[0:1] * pad_s[mg + c0 - 2:mg + c0 - 2 + chunk, :]
            for j in range(1, LRU_CONV):
                xr = xr + cw[j:j + 1] * pad_s[mg + c0 + j - 2:mg + c0 + j - 2 + chunk, :]
            xb = xr.astype(BF16)
            for d in range(2):
                r = jax.nn.sigmoid(jnp.dot(xb, wa_ref[d, 0], preferred_element_type=F32) + ba_ref[d:d + 1, :])
                i = jax.nn.sigmoid(jnp.dot(xb, wx_ref[d, 0], preferred_element_type=F32) + bx_ref[d:d + 1, :])
                log_a = cl_ref[d:d + 1, :] * r
                a_s[d][r0 + c0:r0 + c0 + chunk, :] = jnp.exp(log_a)
                b_s[d][r0 + c0:r0 + c0 + chunk, :] = jnp.sqrt(1.0 - jnp.exp(2.0 * log_a)) * (i * xr)

    def scan_segment(r0, n, carry):
        def step(t, hc):
            hf, hr = hc
            tf = r0 + t
            tr = r0 + n - 1 - t
            hf = af_s[pl.ds(tf, 1), :] * hf + bf_s[pl.ds(tf, 1), :]
            hr = ar_s[pl.ds(tr, 1), :] * hr + br_s[pl.ds(tr, 1), :]
            hf_s[pl.ds(tf, 1), :] = hf
            hr_s[pl.ds(tr, 1), :] = hr
            return hf, hr
        return lax.fori_loop(0, n, step, carry, unroll=8)

    h0 = jnp.zeros((1, LANES), F32)
    carry = scan_segment(0, n_ctx, (h0, h0))
    scan_segment(n_ctx, n_lat, carry)
    o_ref[0] = ((hf_s[...] + hr_s[...]) * _gelu_tanh(ug_ref[0])).astype(o_ref.dtype)


def _block_diag_tiles(w):
    w = w.reshape(2, LRU_HEADS // 2, 2, LRU_HEAD_DIM, LRU_HEAD_DIM)
    z = jnp.zeros_like(w[:, :, 0])
    top = jnp.concatenate([w[:, :, 0], z], axis=-1)
    bot = jnp.concatenate([z, w[:, :, 1]], axis=-1)
    return jnp.concatenate([top, bot], axis=-2).astype(BF16)


def _rglru_mixer(u_all, n_ctx, conv_w, conv_b, wa, ba, wx, bx, lam):
    bsz, n_all, _ = u_all.shape
    n_lat = n_all - n_ctx
    n_tiles = LRU_WIDTH // LANES
    gate_blk0 = HY_IN // LANES
    rec_blk0 = (HY_IN + LRU_WIDTH) // LANES
    decay_scale = -LRU_C * jax.nn.softplus(-lam)
    seq = lambda col0: pl.BlockSpec((1, n_all, LANES), lambda b, j: (b, 0, col0 + j))
    vec = lambda rows: pl.BlockSpec((rows, LANES), lambda b, j: (0, j))
    mat = pl.BlockSpec((2, 1, LANES, LANES), lambda b, j: (0, j, 0, 0))
    full = pltpu.VMEM((n_all, LANES), F32)
    return pl.pallas_call(
        partial(_rglru_body, n_ctx=n_ctx, n_lat=n_lat),
        grid=(bsz, n_tiles),
        in_specs=[seq(gate_blk0), seq(rec_blk0), vec(LRU_CONV), vec(1), mat, vec(2), mat, vec(2), vec(2)],
        out_specs=pl.BlockSpec((1, n_all, LANES), lambda b, j: (b, 0, j)),
        out_shape=jax.ShapeDtypeStruct((bsz, n_all, LRU_WIDTH), BF16),
        scratch_shapes=[pltpu.VMEM((n_lat + 2 * LRU_CONV_MARGIN, LANES), F32),
                        full, full, full, full, full, full],
        compiler_params=pltpu.CompilerParams(
            dimension_semantics=("parallel", "parallel"),
            vmem_limit_bytes=VMEM_LIMIT_BYTES),
        name="rglru",
    )(u_all, u_all, conv_w, conv_b.reshape(1, LRU_WIDTH), _block_diag_tiles(wa), ba,
      _block_diag_tiles(wx), bx, decay_scale)


ATT_Q_TILE = 256


def _attn_body(q_ref, k_ref, v_ref, lv_ref, g_ref, o_ref, *, lam_init):
    lv = lv_ref[...]
    lam = (jnp.exp(jnp.sum(lv[0:1] * lv[1:2], keepdims=True))
           - jnp.exp(jnp.sum(lv[2:3] * lv[3:4], keepdims=True)) + lam_init)
    q = q_ref[0]
    k = k_ref[0]
    v = v_ref[0]
    outs = []
    for m in range(2):
        qm = q[:, m * ATT_HEAD_DIM:(m + 1) * ATT_HEAD_DIM]
        km = k[:, m * ATT_HEAD_DIM:(m + 1) * ATT_HEAD_DIM]
        s = lax.dot_general(qm, km, (((1,), (1,)), ((), ())), preferred_element_type=F32)
        p = jnp.exp(s - jnp.max(s, axis=-1, keepdims=True))
        denom = jnp.sum(p, axis=-1, keepdims=True)
        o = jnp.dot(p.astype(BF16), v, preferred_element_type=F32)
        outs.append(o / denom)
    o = outs[0] - lam * outs[1]
    y = o * lax.rsqrt(jnp.mean(o * o, axis=-1, keepdims=True) + LN_EPS) * g_ref[...] * (1.0 - lam_init)
    o_ref[0] = y.astype(o_ref.dtype)


def _diff_attention(qkv, lam_vecs, gain, lam_init, q_row0, n_q, n_k):
    bsz = qkv.shape[0]
    tq = ATT_Q_TILE
    q_blk0 = q_row0 // tq
    n_heads = ATT_HEADS
    return pl.pallas_call(
        partial(_attn_body, lam_init=lam_init),
        grid=(bsz, n_heads, n_q // tq),
        in_specs=[
            pl.BlockSpec((1, tq, ATT_V_DIM), lambda b, h, i: (b, q_blk0 + i, h)),
            pl.BlockSpec((1, n_k, ATT_V_DIM), lambda b, h, i: (b, 0, n_heads + h)),
            pl.BlockSpec((1, n_k, ATT_V_DIM), lambda b, h, i: (b, 0, 2 * n_heads + h)),
            pl.BlockSpec((4, ATT_HEAD_DIM), lambda b, h, i: (0, 0)),
            pl.BlockSpec((1, ATT_V_DIM), lambda b, h, i: (0, 0)),
        ],
        out_specs=pl.BlockSpec((1, tq, ATT_V_DIM), lambda b, h, i: (b, i, h)),
        out_shape=jax.ShapeDtypeStruct((bsz, n_q, ATT_WIDTH), BF16),
        compiler_params=pltpu.CompilerParams(
            dimension_semantics=("parallel", "parallel", "arbitrary"),
            vmem_limit_bytes=VMEM_LIMIT_BYTES),
        name="diff_attention",
    )(qkv, qkv, qkv, lam_vecs, gain.reshape(1, ATT_V_DIM))


def _axial_rope_tables(n):
    rows = n // GRID_W
    row = jnp.broadcast_to(jnp.arange(rows, dtype=F32)[:, None], (rows, GRID_W)).reshape(-1)
    col = jnp.broadcast_to(jnp.arange(GRID_W, dtype=F32)[None, :], (rows, GRID_W)).reshape(-1)
    inv = ROPE_THETA ** (-jnp.arange(0, ROPE_AXIS_DIM, 2, dtype=F32) / ROPE_AXIS_DIM)
    ang = jnp.stack([row[:, None] * inv, col[:, None] * inv], axis=1)
    return jnp.cos(ang), jnp.sin(ang)


def _apply_axial_rope(x, cos, sin):
    xs = x.reshape(x.shape[:-1] + (2, 2, ROPE_AXIS_DIM // 2))
    x1, x2 = xs[..., 0, :], xs[..., 1, :]
    cb, sb = cos[None, :, None, None], sin[None, :, None, None]
    out = jnp.stack([x1 * cb - x2 * sb, x2 * cb + x1 * sb], axis=-2)
    return out.reshape(x.shape)


def _diff_attention_mixer(u_att, n_ctx, lam_vecs, subln_gain, lam_init, need_ctx):
    bsz, n_all, _ = u_att.shape
    n_lat = n_all - n_ctx
    scale = ATT_HEAD_DIM ** -0.5
    u_lat = u_att[:, n_ctx:]
    u_ctx = u_att[:, :n_ctx]
    q_l = u_lat[..., :ATT_QK].reshape(bsz, n_lat, ATT_HEADS, 2, ATT_HEAD_DIM)
    k_l = u_lat[..., ATT_QK:2 * ATT_QK].reshape(bsz, n_lat, ATT_HEADS, 2, ATT_HEAD_DIM)
    cos, sin = _axial_rope_tables(n_lat)
    q_l = _apply_axial_rope(q_l, cos, sin) * scale
    k_l = _apply_axial_rope(k_l, cos, sin)
    lat = jnp.concatenate([q_l.reshape(bsz, n_lat, ATT_QK), k_l.reshape(bsz, n_lat, ATT_QK),
                           u_lat[..., 2 * ATT_QK:]], axis=-1).astype(BF16)
    cx = jnp.concatenate([u_ctx[..., :ATT_QK] * scale, u_ctx[..., ATT_QK:]], axis=-1).astype(BF16)
    qkv = jnp.concatenate([cx, lat], axis=1)
    y_lat = _diff_attention(qkv, lam_vecs, subln_gain, lam_init, n_ctx, n_lat, n_all)
    y_ctx = _diff_attention(qkv, lam_vecs, subln_gain, lam_init, 0, n_ctx, n_ctx) if need_ctx else None
    return y_lat, y_ctx


ROW_TILE = 256
IN_PROJ_COL_TILE = 1408
COMBINE_ROW_TILE = 128


def _ln_rows(x):
    mu = jnp.mean(x, axis=-1, keepdims=True)
    xc = x - mu
    return xc * lax.rsqrt(jnp.mean(xc * xc, axis=-1, keepdims=True) + LN_EPS)


def _segments(bsz, seg_rows, tile):
    ids = []
    for b in range(bsz):
        for rows, kind in seg_rows:
            assert rows % tile == 0
            ids += [bsz if kind == 'ctx' else b] * (rows // tile)
    return jnp.asarray(ids, jnp.int32)


def _ln_mm_body(seg_ref, x_ref, mod_ref, w_ref, o_ref):
    m = mod_ref[0]
    h = _ln_rows(x_ref[...]) * (1.0 + m[1:2]) + m[0:1]
    o_ref[...] = jnp.dot(h.astype(BF16), w_ref[...], preferred_element_type=F32)


def _ln_mod_matmul(x, mod, seg, w):
    t, d = x.shape
    n = w.shape[1]
    tm, tn = ROW_TILE, IN_PROJ_COL_TILE
    assert t % tm == 0 and n % tn == 0
    grid_spec = pltpu.PrefetchScalarGridSpec(
        num_scalar_prefetch=1,
        grid=(n // tn, t // tm),
        in_specs=[pl.BlockSpec((tm, d), lambda j, i, s: (i, 0)),
                  pl.BlockSpec((1, 2, d), lambda j, i, s: (s[i], 0, 0)),
                  pl.BlockSpec((d, tn), lambda j, i, s: (0, j))],
        out_specs=pl.BlockSpec((tm, tn), lambda j, i, s: (i, j)),
    )
    return pl.pallas_call(
        _ln_mm_body, grid_spec=grid_spec,
        out_shape=jax.ShapeDtypeStruct((t, n), F32),
        compiler_params=pltpu.CompilerParams(dimension_semantics=("parallel", "arbitrary"),
                                             vmem_limit_bytes=VMEM_LIMIT_BYTES),
        name="ln_mod_in_proj",
    )(seg, x, mod, w)


def _split_dot(a, b_hi, b_lo):
    a_hi = a.astype(BF16)
    a_lo = (a - a_hi.astype(F32)).astype(BF16)
    return (jnp.dot(a_hi, b_hi, preferred_element_type=F32) + jnp.dot(a_hi, b_lo, preferred_element_type=F32)
            + jnp.dot(a_lo, b_hi, preferred_element_type=F32))


def _out_ln_body(seg_ref, mixed_ref, x_ref, mod_ref, w_ref, lng_ref, rwh_ref, rwl_ref, rb_ref,
                 x1_ref, h2_ref, h2b_ref, ti_ref, tw_ref):
    m = mod_ref[0]
    mix = jnp.dot(mixed_ref[...], w_ref[...], preferred_element_type=F32)
    x1 = _ln_rows(DEEPNORM_ALPHA * x_ref[...] + m[0:1] * mix) * lng_ref[0:1] + lng_ref[1:2]
    x1_ref[...] = x1
    h2 = _ln_rows(x1) * (1.0 + m[2:3]) + m[1:2]
    h2_ref[...] = h2
    h2b_ref[...] = h2.astype(BF16)
    scores = jax.nn.sigmoid(_split_dot(h2, rwh_ref[...], rwl_ref[...]))
    lane = lax.broadcasted_iota(jnp.int32, scores.shape, 1).astype(F32)
    biased = jnp.where(lane < N_EXPERTS, scores + rb_ref[...], -jnp.inf)
    top_i = jnp.zeros(scores.shape, F32)
    top_s = jnp.zeros(scores.shape, F32)
    for k in range(TOP_K):
        best = jnp.max(biased, axis=-1, keepdims=True)
        pick = jnp.min(jnp.where(biased == best, lane, float(LANES)), axis=-1, keepdims=True)
        hit = lane == pick
        top_i = jnp.where(lane == k, pick, top_i)
        top_s = jnp.where(lane == k, jnp.sum(jnp.where(hit, scores, 0.0), axis=-1, keepdims=True), top_s)
        biased = jnp.where(hit, -jnp.inf, biased)
    ti_ref[...] = top_i.astype(jnp.int32)
    tw_ref[...] = top_s / jnp.sum(top_s, axis=-1, keepdims=True) * ROUTED_SCALE


def _out_proj_norm_route(mixed, x, mod, seg, w, ln_gb, router_w, router_b):
    t, d = x.shape
    tm = ROW_TILE
    rw = jnp.pad(router_w, ((0, 0), (0, LANES - N_EXPERTS)))
    rw_hi, rw_lo = _split_bf16(rw)
    rb = jnp.pad(router_b, (0, LANES - N_EXPERTS)).reshape(1, LANES)
    row = lambda width: pl.BlockSpec((tm, width), lambda i, s: (i, 0))
    const = lambda shape: pl.BlockSpec(shape, lambda i, s: (0,) * len(shape))
    grid_spec = pltpu.PrefetchScalarGridSpec(
        num_scalar_prefetch=1,
        grid=(t // tm,),
        in_specs=[row(d), row(d), pl.BlockSpec((1, 3, d), lambda i, s: (s[i], 0, 0)), const((d, d)),
                  const((2, d)), const((d, LANES)), const((d, LANES)), const((1, LANES))],
        out_specs=[row(d), row(d), row(d), row(LANES), row(LANES)],
    )
    x1, h2, h2b, top_i, top_w = pl.pallas_call(
        _out_ln_body, grid_spec=grid_spec,
        out_shape=[jax.ShapeDtypeStruct((t, d), F32), jax.ShapeDtypeStruct((t, d), F32),
                   jax.ShapeDtypeStruct((t, d), BF16), jax.ShapeDtypeStruct((t, LANES), jnp.int32),
                   jax.ShapeDtypeStruct((t, LANES), F32)],
        compiler_params=pltpu.CompilerParams(dimension_semantics=("parallel",),
                                             vmem_limit_bytes=VMEM_LIMIT_BYTES),
        name="out_proj_norm_route",
    )(seg, mixed, x, mod, w, ln_gb, rw_hi, rw_lo, rb)
    return x1, h2, h2b, top_i[:, :TOP_K], top_w[:, :TOP_K]


def _shared_body(h_ref, wgu_ref, wd_ref, o_ref):
    gu = jnp.dot(h_ref[...], wgu_ref[...], preferred_element_type=F32)
    f = wd_ref.shape[0]
    g, u = gu[:, :f], gu[:, f:]
    act = (g * jax.nn.sigmoid(g) * u).astype(BF16)
    o_ref[...] = jnp.dot(act, wd_ref[...], preferred_element_type=F32)


def _shared_expert(hb, w_gate, w_up, w_down):
    t, d = hb.shape
    f = w_gate.shape[1]
    tm = ROW_TILE
    wgu = jnp.concatenate([w_gate, w_up], axis=1).astype(BF16)
    return pl.pallas_call(
        _shared_body, grid=(t // tm,),
        in_specs=[pl.BlockSpec((tm, d), lambda i: (i, 0)), pl.BlockSpec((d, 2 * f), lambda i: (0, 0)),
                  pl.BlockSpec((f, d), lambda i: (0, 0))],
        out_specs=pl.BlockSpec((tm, d), lambda i: (i, 0)),
        out_shape=jax.ShapeDtypeStruct((t, d), F32),
        compiler_params=pltpu.CompilerParams(dimension_semantics=("parallel",),
                                             vmem_limit_bytes=VMEM_LIMIT_BYTES),
        name="shared_expert",
    )(hb, wgu, w_down.astype(BF16))


def _combine_body(seg_ref, *refs):
    y_refs, (sh_ref, x_ref, mod_ref, lng_ref, o_ref) = refs[:TOP_K], refs[TOP_K:]
    ffn = sh_ref[...]
    for y_ref in y_refs:
        ffn = ffn + y_ref[...]
    o_ref[...] = _ln_rows(DEEPNORM_ALPHA * x_ref[...] + mod_ref[0] * ffn) * lng_ref[0:1] + lng_ref[1:2]


def _combine_norm(y, shared, x1, gate, seg, ln_gb):
    t, d = x1.shape
    tm = COMBINE_ROW_TILE
    tiles = t // tm
    row = pl.BlockSpec((tm, d), lambda i, s: (i, 0))
    y_specs = [pl.BlockSpec((tm, d), lambda i, s, k=k: (k * tiles + i, 0)) for k in range(TOP_K)]
    grid_spec = pltpu.PrefetchScalarGridSpec(
        num_scalar_prefetch=1,
        grid=(tiles,),
        in_specs=y_specs + [row, row, pl.BlockSpec((1, 1, d), lambda i, s: (s[i], 0, 0)),
                            pl.BlockSpec((2, d), lambda i, s: (0, 0))],
        out_specs=row,
    )
    return pl.pallas_call(
        _combine_body, grid_spec=grid_spec,
        out_shape=jax.ShapeDtypeStruct((t, d), F32),
        compiler_params=pltpu.CompilerParams(dimension_semantics=("parallel",),
                                             vmem_limit_bytes=VMEM_LIMIT_BYTES),
        name="combine_norm",
    )(seg, *([y] * TOP_K), shared, x1, gate, ln_gb)


MOE_BLOCK_ROWS = 256
MOE_VMEM_LIMIT_BYTES = 56 * 1024 * 1024


def _moe_body(be_ref, nu_ref, idx_ref, idx_nxt_ref, wg_ref, wu_ref, wd_ref, rw_ref, h_hbm, o_hbm,
              xbuf, ybuf, wg_s, wu_s, wd_s, gsem, ssem, *, bm):
    i = pl.program_id(0)
    n_used = nu_ref[0]
    slot = i % 2

    def gather_rows(tok_ref, dst_slot):
        for r in range(bm):
            pltpu.make_async_copy(h_hbm.at[pl.ds(tok_ref[0, 0, r], 1)], xbuf.at[dst_slot, pl.ds(r, 1)],
                                  gsem.at[dst_slot]).start()

    def wait_gather(s):
        pltpu.make_async_copy(h_hbm.at[pl.ds(0, bm)], xbuf.at[s], gsem.at[s]).wait()

    def wait_scatter(s):
        pltpu.make_async_copy(ybuf.at[s], o_hbm.at[pl.ds(0, bm)], ssem.at[s]).wait()

    @pl.when(i == 0)
    def _():
        gather_rows(idx_ref, 0)

    e = be_ref[i]
    e_prev = be_ref[jnp.maximum(i - 1, 0)]

    @pl.when((i < n_used) & ((i == 0) | (e != e_prev)))
    def _():
        wg_s[...] = wg_ref[0, 0].astype(BF16)
        wu_s[...] = wu_ref[0, 0].astype(BF16)
        wd_s[...] = wd_ref[0, 0].astype(BF16)

    @pl.when(i < n_used)
    def _():
        wait_gather(slot)

        @pl.when(i >= 2)
        def _():
            wait_scatter(slot)

        gather_rows(idx_nxt_ref, 1 - slot)

        xb = xbuf[slot].astype(BF16)
        g = jnp.dot(xb, wg_s[...], preferred_element_type=F32)
        u = jnp.dot(xb, wu_s[...], preferred_element_type=F32)
        act = (g * jax.nn.sigmoid(g) * u).astype(BF16)
        ybuf[slot] = jnp.dot(act, wd_s[...], preferred_element_type=F32) * rw_ref[...]
        for r in range(bm):
            pltpu.make_async_copy(ybuf.at[slot, pl.ds(r, 1)], o_hbm.at[pl.ds(idx_ref[0, 0, bm + r], 1)],
                                  ssem.at[slot]).start()

        @pl.when(i == n_used - 1)
        def _():
            @pl.when(i >= 1)
            def _():
                wait_scatter(1 - slot)
            wait_scatter(slot)
            wait_gather(1 - slot)

    @pl.when(i >= n_used)
    def _():
        xbuf[0] = jnp.zeros(xbuf.shape[1:], F32)
        fill = pltpu.make_async_copy(xbuf.at[0], o_hbm.at[pl.ds(pl.multiple_of(i * bm, bm), bm)], gsem.at[0])
        fill.start()
        fill.wait()


def _routed_experts(h, idx, row_w, block_e, n_used, w_gate, w_up, w_down, layer):
    bm = MOE_BLOCK_ROWS
    n_blocks = idx.shape[0]
    d = h.shape[1]
    f = w_gate.shape[-1]
    grid_spec = pltpu.PrefetchScalarGridSpec(
        num_scalar_prefetch=2,
        grid=(n_blocks,),
        in_specs=[
            pl.BlockSpec((1, 1, 2 * bm), lambda i, be, nu: (i, 0, 0), memory_space=pltpu.SMEM),
            pl.BlockSpec((1, 1, 2 * bm), lambda i, be, nu: (jnp.minimum(i + 1, n_blocks - 1), 0, 0),
                         memory_space=pltpu.SMEM),
            pl.BlockSpec((1, 1, d, f), lambda i, be, nu: (layer, be[i], 0, 0)),
            pl.BlockSpec((1, 1, d, f), lambda i, be, nu: (layer, be[i], 0, 0)),
            pl.BlockSpec((1, 1, f, d), lambda i, be, nu: (layer, be[i], 0, 0)),
            pl.BlockSpec((bm, 1), lambda i, be, nu: (i, 0)),
            pl.BlockSpec(memory_space=pl.ANY),
        ],
        out_specs=pl.BlockSpec(memory_space=pl.ANY),
        scratch_shapes=[pltpu.VMEM((2, bm, d), F32), pltpu.VMEM((2, bm, d), F32),
                        pltpu.VMEM((d, f), BF16), pltpu.VMEM((d, f), BF16), pltpu.VMEM((f, d), BF16),
                        pltpu.SemaphoreType.DMA((2,)), pltpu.SemaphoreType.DMA((2,))],
    )
    return pl.pallas_call(
        partial(_moe_body, bm=bm),
        grid_spec=grid_spec,
        out_shape=jax.ShapeDtypeStruct((n_blocks * bm, d), F32),
        compiler_params=pltpu.CompilerParams(
            dimension_semantics=("arbitrary",),
            vmem_limit_bytes=MOE_VMEM_LIMIT_BYTES),
        name="routed_experts",
    )(block_e, n_used, idx, idx, w_gate, w_up, w_down, row_w, h)


def _route(top_idx, top_w):
    bm = MOE_BLOCK_ROWS
    n_tok = top_idx.shape[0]
    n_assign = n_tok * TOP_K
    flat_e = top_idx.reshape(-1)
    flat_w = top_w.reshape(-1)
    order = jnp.argsort(flat_e).astype(jnp.int32)
    counts = jnp.sum(flat_e[None, :] == jnp.arange(N_EXPERTS)[:, None], axis=1).astype(jnp.int32)
    padded = (counts + bm - 1) // bm * bm
    padded_end = jnp.cumsum(padded)
    live_end = jnp.cumsum(counts)
    shift = padded_end - padded - (live_end - counts)
    n_blocks = -(-n_assign // bm) + N_EXPERTS
    block_e = jnp.minimum(
        jnp.searchsorted(padded_end, jnp.arange(n_blocks) * bm, side='right'), N_EXPERTS - 1).astype(jnp.int32)
    n_used = (padded_end[-1:] // bm).astype(jnp.int32)
    rows = jnp.arange(n_blocks * bm, dtype=jnp.int32).reshape(n_blocks, bm)
    row_slot = rows - shift[block_e][:, None]
    row_live = row_slot < live_end[block_e][:, None]
    row_assign = order[jnp.where(row_live, row_slot, 0)]
    row_tok = jnp.where(row_live, row_assign // TOP_K, 0)
    row_dst = jnp.where(row_live, (row_assign % TOP_K) * n_tok + row_tok,
                        n_assign + rows - live_end[block_e][:, None])
    row_w = jnp.where(row_live, flat_w[row_assign], 0.0)
    idx = jnp.concatenate([row_tok, row_dst], axis=1).astype(jnp.int32)[:, None, :]
    return idx, row_w.reshape(-1, 1), block_e, n_used


def kernel(x, c, ctx, c_ctx, ada_w, ada_b, w_in, hy_conv_w, hy_conv_b, hy_w1, hy_b1, hy_w2, hy_b2, hy_w3, hy_freq, hy_skip, lru_conv_w, lru_conv_b, lru_wa, lru_ba, lru_wx, lru_bx, lru_lambda, att_lambda, att_subln, w_out, ln_g, ln_b, router_w, router_b, exp_w_gate, exp_w_up, exp_w_down, sh_w_gate, sh_w_up, sh_w_down):
    bsz, n_lat, d = x.shape
    n_ctx = ctx.shape[1]
    n_all = n_ctx + n_lat
    x_all = jnp.concatenate([ctx, x], axis=1).reshape(bsz * n_all, d)
    both = ((n_ctx, 'ctx'), (n_lat, 'lat'))
    seg_all = _segments(bsz, both, ROW_TILE)
    x_out = None

    for l in range(DEPTH):
        need_ctx = l < DEPTH - 1
        lam_init = 0.8 - 0.6 * math.exp(-0.3 * l)
        cond = jnp.concatenate([c, c_ctx[None, :]], axis=0)
        mod = jnp.dot(jax.nn.silu(cond), ada_w[l], precision=lax.Precision.HIGHEST) + ada_b[l]
        sh1, sc1, g1, sh2, sc2, g2 = jnp.split(mod, 6, axis=-1)
        hy_args = (hy_conv_w[l], hy_conv_b[l], hy_w1[l], hy_b1[l], hy_w2[l], hy_b2[l],
                   hy_w3[l], hy_freq[l], hy_skip[l])

        u_all = _ln_mod_matmul(x_all, jnp.stack([sh1, sc1], axis=1), seg_all,
                               w_in[l].astype(BF16)).reshape(bsz, n_all, -1)
        y_hy_l = _hyena_mixer(u_all[:, n_ctx:, :HY_IN], *hy_args)
        y_lru = _rglru_mixer(u_all, n_ctx, lru_conv_w[l], lru_conv_b[l], lru_wa[l], lru_ba[l],
                             lru_wx[l], lru_bx[l], lru_lambda[l])
        y_att_l, y_att_c = _diff_attention_mixer(u_all[..., HY_IN + LRU_IN:], n_ctx, att_lambda[l],
                                                 att_subln[l], lam_init, need_ctx)
        if need_ctx:
            y_hy = jnp.concatenate([_hyena_mixer(u_all[:, :n_ctx, :HY_IN], *hy_args), y_hy_l], axis=1)
            mixed = jnp.concatenate([y_hy.astype(BF16), y_lru, jnp.concatenate([y_att_c, y_att_l], axis=1)], axis=-1)
            x_res, seg, seg_c = x_all, seg_all, _segments(bsz, both, COMBINE_ROW_TILE)
        else:
            mixed = jnp.concatenate([y_hy_l.astype(BF16), y_lru[:, n_ctx:], y_att_l], axis=-1)
            x_res = x_all.reshape(bsz, n_all, d)[:, n_ctx:].reshape(bsz * n_lat, d)
            seg = _segments(bsz, ((n_lat, 'lat'),), ROW_TILE)
            seg_c = _segments(bsz, ((n_lat, 'lat'),), COMBINE_ROW_TILE)
        n_tok = x_res.shape[0]
        ln_gb1 = jnp.stack([ln_g[l, 0], ln_b[l, 0]])
        ln_gb2 = jnp.stack([ln_g[l, 1], ln_b[l, 1]])
        x1, h2, h2b, top_idx, top_w = _out_proj_norm_route(
            mixed.reshape(n_tok, d), x_res, jnp.stack([g1, sh2, sc2], axis=1), seg,
            w_out[l].astype(BF16), ln_gb1, router_w[l], router_b[l])

        idx, row_w, block_e, n_used = _route(top_idx, top_w)
        y = _routed_experts(h2, idx, row_w, block_e, n_used, exp_w_gate, exp_w_up, exp_w_down, l)
        shared = _shared_expert(h2b, sh_w_gate[l], sh_w_up[l], sh_w_down[l])
        x_out = _combine_norm(y, shared, x1, g2[:, None, :], seg_c, ln_gb2)
        x_all = x_out
    return x_out.reshape(bsz, n_lat, d)
```

```python
import math
from functools import lru_cache, partial

import numpy as np
import jax
import jax.numpy as jnp
from jax import lax
from jax.experimental import pallas as pl
from jax.experimental.pallas import tpu as pltpu

D_MODEL = 2048
DEPTH = 2
GRID_W = 64
F32 = jnp.float32
BF16 = jnp.bfloat16
LANES = 128
SUBLANES = 8

HY_WIDTH = D_MODEL // 4
LRU_WIDTH = D_MODEL // 4
ATT_WIDTH = D_MODEL // 2
HY_ORDER = 2
HY_CONV = 3
HY_BANDS = 8
HY_TARGET = 1e-2
HY_MAX_DECAY = math.log(HY_TARGET) / 0.3
HY_MIN_DECAY = math.log(HY_TARGET) / 1.5
HY_IN = (HY_ORDER + 1) * HY_WIDTH

LRU_HEADS = 8
LRU_HEAD_DIM = LRU_WIDTH // LRU_HEADS
LRU_CONV = 4
LRU_C = 8.0
LRU_IN = 2 * LRU_WIDTH

ATT_HEADS = 4
ATT_V_DIM = ATT_WIDTH // ATT_HEADS
ATT_HEAD_DIM = ATT_V_DIM // 2
ATT_QK = ATT_HEADS * 2 * ATT_HEAD_DIM
ROPE_THETA = 10000.0
ROPE_AXIS_DIM = ATT_HEAD_DIM // 2

N_EXPERTS = 64
TOP_K = 8
ROUTED_SCALE = 2.5

LN_EPS = 1e-5
DEEPNORM_ALPHA = (2 * DEPTH) ** 0.25

VMEM_LIMIT_BYTES = 48 * 1024 * 1024


def _depthwise_conv(x, w, b, pad_left, pad_right):
    y = lax.conv_general_dilated(
        x, w[:, None, :], window_strides=(1,),
        padding=[(pad_left, pad_right)], dimension_numbers=('NWC', 'WIO', 'NWC'),
        feature_group_count=x.shape[-1])
    return y + b


def _hyena_filter(n, w1, b1, w2, b2, w3, freq):
    pos = jnp.arange(n, dtype=F32)
    t = (pos / max(n - 1, 1))[:, None]
    bands = jnp.linspace(1e-4, HY_BANDS - 1, HY_BANDS, dtype=F32)
    ang = (2.0 * math.pi / n) * pos[:, None] * bands
    feats = jnp.concatenate([t, jnp.cos(ang), -jnp.sin(ang)], axis=-1)
    hp = lax.Precision.HIGHEST
    hid = jnp.sin(freq * (jnp.dot(feats, w1, precision=hp) + b1))
    hid = jnp.sin(freq * (jnp.dot(hid, w2, precision=hp) + b2))
    filt = jnp.dot(hid, w3, precision=hp).reshape(n, 2, HY_ORDER, HY_WIDTH)
    decay = jnp.abs(jnp.linspace(HY_MIN_DECAY, HY_MAX_DECAY, HY_WIDTH, dtype=F32))
    filt = filt * jnp.exp(-t * decay)[:, None, None, :]
    fwd, bwd = filt[:, 0], filt[:, 1]
    two_sided = jnp.concatenate([fwd, jnp.zeros_like(fwd[:1]), jnp.flip(bwd[1:], axis=0)], axis=0)
    return two_sided / jnp.sum(jnp.abs(two_sided), axis=0, keepdims=True)


DFT_BLOCK_BYTES = 4 * 1024 * 1024
DFT_MINOR_LEN = 128
DFT_SINGLE_LEVEL_MAX = 1024


def _bmm3_body(lh_ref, ll_ref, r_ref, o_ref, *, nb, shared):
    for b in range(nb):
        r = r_ref[b]
        r_hi = r.astype(BF16)
        r_lo = (r - r_hi.astype(F32)).astype(BF16)
        lh = lh_ref[0 if shared else b]
        ll = ll_ref[0 if shared else b]
        o_ref[b] = (jnp.dot(lh, r_hi, preferred_element_type=F32)
                    + jnp.dot(lh, r_lo, preferred_element_type=F32)
                    + jnp.dot(ll, r_hi, preferred_element_type=F32))


def _bmm3(l_hi, l_lo, r, nb):
    batch, k, c = r.shape
    m = l_hi.shape[1]
    shared = l_hi.shape[0] == 1
    nb = max(1, min(nb, batch, DFT_BLOCK_BYTES // (4 * c * max(k, m))))
    assert batch % nb == 0
    l_spec = pl.BlockSpec((1 if shared else nb, m, k), (lambda i: (0, 0, 0)) if shared else (lambda i: (i, 0, 0)))
    return pl.pallas_call(
        partial(_bmm3_body, nb=nb, shared=shared),
        grid=(batch // nb,),
        in_specs=[l_spec, l_spec, pl.BlockSpec((nb, k, c), lambda i: (i, 0, 0))],
        out_specs=pl.BlockSpec((nb, m, c), lambda i: (i, 0, 0)),
        out_shape=jax.ShapeDtypeStruct((batch, m, c), F32),
        compiler_params=pltpu.CompilerParams(dimension_semantics=("parallel",), vmem_limit_bytes=VMEM_LIMIT_BYTES),
        name="dft_matmul",
    )(l_hi, l_lo, r)


def _stack_complex(f):
    return np.concatenate([np.concatenate([f.real, -f.imag], axis=-1),
                           np.concatenate([f.imag, f.real], axis=-1)], axis=-2)


def _split_bf16(a):
    a = jnp.asarray(a, F32)
    hi = a.astype(BF16)
    return hi, (a - hi.astype(F32)).astype(BF16)


@lru_cache(maxsize=None)
def _dft_tables(n1_io, n1, n2):
    n = n1 * n2
    k1 = np.arange(n1)
    a = np.arange(n1_io)
    b = np.arange(n2)
    ang = -2.0 * np.pi * (((np.outer(k1, a) * n2)[None] + b[:, None, None] * k1[None, :, None]) % n) / n
    fa = np.exp(1j * ang)
    fb = np.exp(-2j * np.pi * (np.outer(b, b) % n2) / n2)
    return dict(
        fwd_a=_stack_complex(fa),
        fwd_a_real=np.concatenate([fa.real, fa.imag], axis=-2),
        fwd_b=_stack_complex(fb)[None],
        inv_b=_stack_complex(np.conj(fb).T)[None],
        inv_a=_stack_complex(np.conj(fa).transpose(0, 2, 1) / n),
    )


def _dft_split(length):
    if length <= DFT_SINGLE_LEVEL_MAX:
        return 1, length
    return length // DFT_MINOR_LEN, DFT_MINOR_LEN


def _swap_stage(x, n1, n2):
    c = x.shape[-1]
    return x.reshape(n2, 2, n1, c).transpose(2, 1, 0, 3).reshape(n1, 2 * n2, c)


def _dft_forward(rows, table, n1, n2, tabs):
    if n1 > 1:
        rows = _bmm3(*_split_bf16(tabs[table]), rows, 16)
    return _bmm3(*_split_bf16(tabs['fwd_b']), _swap_stage(rows, n1, n2), 8)


def _filter_spectrum(two_sided):
    length, c = two_sided.shape
    n1, n2 = _dft_split(length)
    tabs = _dft_tables(n1, n1, n2)
    rows = two_sided.reshape(n1, n2, c).transpose(1, 0, 2)
    if n1 == 1:
        rows = jnp.concatenate([rows, jnp.zeros_like(rows)], axis=1)
    return _dft_forward(rows, 'fwd_a_real', n1, n2, tabs)


def _long_conv_pair(z, spec):
    pair, n, c = z.shape
    assert pair == 2
    n1, n2 = _dft_split(2 * n)
    n1_io = max(n1 // 2, 1)
    tabs = _dft_tables(n1_io, n1, n2)
    if n1 > 1:
        rows = z.reshape(2, n1_io, n2, c).transpose(2, 0, 1, 3).reshape(n2, 2 * n1_io, c)
    else:
        rows = jnp.pad(z, ((0, 0), (0, n), (0, 0))).transpose(1, 0, 2)
    y = _dft_forward(rows, 'fwd_a', n1, n2, tabs)
    yr, yi = y[:, :n2], y[:, n2:]
    hr, hi = spec[:, :n2], spec[:, n2:]
    y = jnp.concatenate([yr * hr - yi * hi, yr * hi + yi * hr], axis=1)
    d = _bmm3(*_split_bf16(tabs['inv_b']), y, 8)
    d = _swap_stage(d, n2, n1)
    if n1 > 1:
        out = _bmm3(*_split_bf16(tabs['inv_a']), d, 16)
        return out.reshape(n2, 2, n1_io, c).transpose(1, 2, 0, 3).reshape(2, n, c)
    return (d.reshape(n2, 2, c).transpose(1, 0, 2) / (n1 * n2))[:, :n]


def _hyena_mixer(u, conv_w, conv_b, w1, b1, w2, b2, w3, freq, skip):
    n = u.shape[1]
    u = _depthwise_conv(u, conv_w, conv_b, HY_CONV // 2, HY_CONV - 1 - HY_CONV // 2)
    gates = (u[..., :HY_WIDTH], u[..., HY_WIDTH:2 * HY_WIDTH])
    z = u[..., 2 * HY_WIDTH:]
    two_sided = _hyena_filter(n, w1, b1, w2, b2, w3, freq)
    spec = _filter_spectrum(two_sided.reshape(2 * n, HY_ORDER * HY_WIDTH))
    for o in range(HY_ORDER):
        zc = _long_conv_pair(z, spec[..., o * HY_WIDTH:(o + 1) * HY_WIDTH])
        z = gates[o] * (zc + skip[o] * z)
    return z


LRU_CONV_MARGIN = SUBLANES
LRU_TIME_CHUNK = 1024


def _gelu_tanh(x):
    return 0.5 * x * (1.0 + jnp.tanh(math.sqrt(2.0 / math.pi) * (x + 0.044715 * (x * x * x))))


def _rglru_body(ug_ref, ur_ref, cw_ref, cb_ref, wa_ref, ba_ref, wx_ref, bx_ref, cl_ref, o_ref,
                pad_s, af_s, bf_s, ar_s, br_s, hf_s, hr_s, *, n_ctx, n_lat):
    a_s = (af_s, ar_s)
    b_s = (bf_s, br_s)
    mg = LRU_CONV_MARGIN
    cw = cw_ref[...]
    zeros_margin = jnp.zeros((mg, LANES), F32)
    for r0, n in ((0, n_ctx), (n_ctx, n_lat)):
        pad_s[0:mg, :] = zeros_margin
        pad_s[mg:mg + n, :] = ur_ref[0, r0:r0 + n, :]
        pad_s[mg + n:2 * mg + n, :] = zeros_margin
        chunk = min(n, LRU_TIME_CHUNK)
        for c0 in range(0, n, chunk):
            xr = cb_ref[...] + cw[0:1] * pad_s[mg + c0 - 2:mg + c0 - 2 + chunk, :]
            for j in range(1, LRU_CONV):
                xr = xr + cw[j:j + 1] * pad_s[mg + c0 + j - 2:mg + c0 + j - 2 + chunk, :]
            xb = xr.astype(BF16)
            for d in range(2):
                r = jax.nn.sigmoid(jnp.dot(xb, wa_ref[d, 0], preferred_element_type=F32) + ba_ref[d:d + 1, :])
                i = jax.nn.sigmoid(jnp.dot(xb, wx_ref[d, 0], preferred_element_type=F32) + bx_ref[d:d + 1, :])
                log_a = cl_ref[d:d + 1, :] * r
                a_s[d][r0 + c0:r0 + c0 + chunk, :] = jnp.exp(log_a)
                b_s[d][r0 + c0:r0 + c0 + chunk, :] = jnp.sqrt(1.0 - jnp.exp(2.0 * log_a)) * (i * xr)

    def scan_segment(r0, n, carry):
        def step(t, hc):
            hf, hr = hc
            tf = r0 + t
            tr = r0 + n - 1 - t
            hf = af_s[pl.ds(tf, 1), :] * hf + bf_s[pl.ds(tf, 1), :]
            hr = ar_s[pl.ds(tr, 1), :] * hr + br_s[pl.ds(tr, 1), :]
            hf_s[pl.ds(tf, 1), :] = hf
            hr_s[pl.ds(tr, 1), :] = hr
            return hf, hr
        return lax.fori_loop(0, n, step, carry, unroll=8)

    h0 = jnp.zeros((1, LANES), F32)
    carry = scan_segment(0, n_ctx, (h0, h0))
    scan_segment(n_ctx, n_lat, carry)
    o_ref[0] = ((hf_s[...] + hr_s[...]) * _gelu_tanh(ug_ref[0])).astype(o_ref.dtype)


def _block_diag_tiles(w):
    w = w.reshape(2, LRU_HEADS // 2, 2, LRU_HEAD_DIM, LRU_HEAD_DIM)
    z = jnp.zeros_like(w[:, :, 0])
    top = jnp.concatenate([w[:, :, 0], z], axis=-1)
    bot = jnp.concatenate([z, w[:, :, 1]], axis=-1)
    return jnp.concatenate([top, bot], axis=-2).astype(BF16)


def _rglru_mixer(u_all, n_ctx, conv_w, conv_b, wa, ba, wx, bx, lam):
    bsz, n_all, _ = u_all.shape
    n_lat = n_all - n_ctx
    n_tiles = LRU_WIDTH // LANES
    gate_blk0 = HY_IN // LANES
    rec_blk0 = (HY_IN + LRU_WIDTH) // LANES
    decay_scale = -LRU_C * jax.nn.softplus(-lam)
    seq = lambda col0: pl.BlockSpec((1, n_all, LANES), lambda b, j: (b, 0, col0 + j))
    vec = lambda rows: pl.BlockSpec((rows, LANES), lambda b, j: (0, j))
    mat = pl.BlockSpec((2, 1, LANES, LANES), lambda b, j: (0, j, 0, 0))
    full = pltpu.VMEM((n_all, LANES), F32)
    return pl.pallas_call(
        partial(_rglru_body, n_ctx=n_ctx, n_lat=n_lat),
        grid=(bsz, n_tiles),
        in_specs=[seq(gate_blk0), seq(rec_blk0), vec(LRU_CONV), vec(1), mat, vec(2), mat, vec(2), vec(2)],
        out_specs=pl.BlockSpec((1, n_all, LANES), lambda b, j: (b, 0, j)),
        out_shape=jax.ShapeDtypeStruct((bsz, n_all, LRU_WIDTH), BF16),
        scratch_shapes=[pltpu.VMEM((n_lat + 2 * LRU_CONV_MARGIN, LANES), F32),
                        full, full, full, full, full, full],
        compiler_params=pltpu.CompilerParams(
            dimension_semantics=("parallel", "parallel"),
            vmem_limit_bytes=VMEM_LIMIT_BYTES),
        name="rglru",
    )(u_all, u_all, conv_w, conv_b.reshape(1, LRU_WIDTH), _block_diag_tiles(wa), ba,
      _block_diag_tiles(wx), bx, decay_scale)


ATT_Q_TILE = 256


def _attn_body(q_ref, k_ref, v_ref, lv_ref, g_ref, o_ref, *, lam_init):
    lv = lv_ref[...]
    lam = (jnp.exp(jnp.sum(lv[0:1] * lv[1:2], keepdims=True))
           - jnp.exp(jnp.sum(lv[2:3] * lv[3:4], keepdims=True)) + lam_init)
    q = q_ref[0]
    k = k_ref[0]
    v = v_ref[0]
    outs = []
    for m in range(2):
        qm = q[:, m * ATT_HEAD_DIM:(m + 1) * ATT_HEAD_DIM]
        km = k[:, m * ATT_HEAD_DIM:(m + 1) * ATT_HEAD_DIM]
        s = lax.dot_general(qm, km, (((1,), (1,)), ((), ())), preferred_element_type=F32)
        p = jnp.exp(s - jnp.max(s, axis=-1, keepdims=True))
        denom = jnp.sum(p, axis=-1, keepdims=True)
        o = jnp.dot(p.astype(BF16), v, preferred_element_type=F32)
        outs.append(o / denom)
    o = outs[0] - lam * outs[1]
    y = o * lax.rsqrt(jnp.mean(o * o, axis=-1, keepdims=True) + LN_EPS) * g_ref[...] * (1.0 - lam_init)
    o_ref[0] = y.astype(o_ref.dtype)


def _diff_attention(qkv, lam_vecs, gain, lam_init, q_row0, n_q, n_k):
    bsz = qkv.shape[0]
    tq = ATT_Q_TILE
    q_blk0 = q_row0 // tq
    n_heads = ATT_HEADS
    return pl.pallas_call(
        partial(_attn_body, lam_init=lam_init),
        grid=(bsz, n_heads, n_q // tq),
        in_specs=[
            pl.BlockSpec((1, tq, ATT_V_DIM), lambda b, h, i: (b, q_blk0 + i, h)),
            pl.BlockSpec((1, n_k, ATT_V_DIM), lambda b, h, i: (b, 0, n_heads + h)),
            pl.BlockSpec((1, n_k, ATT_V_DIM), lambda b, h, i: (b, 0, 2 * n_heads + h)),
            pl.BlockSpec((4, ATT_HEAD_DIM), lambda b, h, i: (0, 0)),
            pl.BlockSpec((1, ATT_V_DIM), lambda b, h, i: (0, 0)),
        ],
        out_specs=pl.BlockSpec((1, tq, ATT_V_DIM), lambda b, h, i: (b, i, h)),
        out_shape=jax.ShapeDtypeStruct((bsz, n_q, ATT_WIDTH), BF16),
        compiler_params=pltpu.CompilerParams(
            dimension_semantics=("parallel", "parallel", "arbitrary"),
            vmem_limit_bytes=VMEM_LIMIT_BYTES),
        name="diff_attention",
    )(qkv, qkv, qkv, lam_vecs, gain.reshape(1, ATT_V_DIM))


def _axial_rope_tables(n):
    rows = n // GRID_W
    row = jnp.broadcast_to(jnp.arange(rows, dtype=F32)[:, None], (rows, GRID_W)).reshape(-1)
    col = jnp.broadcast_to(jnp.arange(GRID_W, dtype=F32)[None, :], (rows, GRID_W)).reshape(-1)
    inv = ROPE_THETA ** (-jnp.arange(0, ROPE_AXIS_DIM, 2, dtype=F32) / ROPE_AXIS_DIM)
    ang = jnp.stack([row[:, None] * inv, col[:, None] * inv], axis=1)
    return jnp.cos(ang), jnp.sin(ang)


def _apply_axial_rope(x, cos, sin):
    xs = x.reshape(x.shape[:-1] + (2, 2, ROPE_AXIS_DIM // 2))
    x1, x2 = xs[..., 0, :], xs[..., 1, :]
    cb, sb = cos[None, :, None, None], sin[None, :, None, None]
    out = jnp.stack([x1 * cb - x2 * sb, x2 * cb + x1 * sb], axis=-2)
    return out.reshape(x.shape)


def _diff_attention_mixer(u_att, n_ctx, lam_vecs, subln_gain, lam_init, need_ctx):
    bsz, n_all, _ = u_att.shape
    n_lat = n_all - n_ctx
    scale = ATT_HEAD_DIM ** -0.5
    u_lat = u_att[:, n_ctx:]
    u_ctx = u_att[:, :n_ctx]
    q_l = u_lat[..., :ATT_QK].reshape(bsz, n_lat, ATT_HEADS, 2, ATT_HEAD_DIM)
    k_l = u_lat[..., ATT_QK:2 * ATT_QK].reshape(bsz, n_lat, ATT_HEADS, 2, ATT_HEAD_DIM)
    cos, sin = _axial_rope_tables(n_lat)
    q_l = _apply_axial_rope(q_l, cos, sin) * scale
    k_l = _apply_axial_rope(k_l, cos, sin)
    lat = jnp.concatenate([q_l.reshape(bsz, n_lat, ATT_QK), k_l.reshape(bsz, n_lat, ATT_QK),
                           u_lat[..., 2 * ATT_QK:]], axis=-1).astype(BF16)
    cx = jnp.concatenate([u_ctx[..., :ATT_QK] * scale, u_ctx[..., ATT_QK:]], axis=-1).astype(BF16)
    qkv = jnp.concatenate([cx, lat], axis=1)
    y_lat = _diff_attention(qkv, lam_vecs, subln_gain, lam_init, n_ctx, n_lat, n_all)
    y_ctx = _diff_attention(qkv, lam_vecs, subln_gain, lam_init, 0, n_ctx, n_ctx) if need_ctx else None
    return y_lat, y_ctx


ROW_TILE = 256
IN_PROJ_COL_TILE = 1408
COMBINE_ROW_TILE = 128


def _ln_rows(x):
    mu = jnp.mean(x, axis=-1, keepdims=True)
    xc = x - mu
    return xc * lax.rsqrt(jnp.mean(xc * xc, axis=-1, keepdims=True) + LN_EPS)


def _segments(bsz, seg_rows, tile):
    ids = []
    for b in range(bsz):
        for rows, kind in seg_rows:
            assert rows % tile == 0
            ids += [bsz if kind == 'ctx' else b] * (rows // tile)
    return jnp.asarray(ids, jnp.int32)


def _ln_mm_body(seg_ref, x_ref, mod_ref, w_ref, o_ref):
    m = mod_ref[0]
    h = _ln_rows(x_ref[...]) * (1.0 + m[1:2]) + m[0:1]
    o_ref[...] = jnp.dot(h.astype(BF16), w_ref[...], preferred_element_type=F32)


def _ln_mod_matmul(x, mod, seg, w):
    t, d = x.shape
    n = w.shape[1]
    tm, tn = ROW_TILE, IN_PROJ_COL_TILE
    assert t % tm == 0 and n % tn == 0
    grid_spec = pltpu.PrefetchScalarGridSpec(
        num_scalar_prefetch=1,
        grid=(n // tn, t // tm),
        in_specs=[pl.BlockSpec((tm, d), lambda j, i, s: (i, 0)),
                  pl.BlockSpec((1, 2, d), lambda j, i, s: (s[i], 0, 0)),
                  pl.BlockSpec((d, tn), lambda j, i, s: (0, j))],
        out_specs=pl.BlockSpec((tm, tn), lambda j, i, s: (i, j)),
    )
    return pl.pallas_call(
        _ln_mm_body, grid_spec=grid_spec,
        out_shape=jax.ShapeDtypeStruct((t, n), F32),
        compiler_params=pltpu.CompilerParams(dimension_semantics=("parallel", "arbitrary"),
                                             vmem_limit_bytes=VMEM_LIMIT_BYTES),
        name="ln_mod_in_proj",
    )(seg, x, mod, w)


def _split_dot(a, b_hi, b_lo):
    a_hi = a.astype(BF16)
    a_lo = (a - a_hi.astype(F32)).astype(BF16)
    return (jnp.dot(a_hi, b_hi, preferred_element_type=F32) + jnp.dot(a_hi, b_lo, preferred_element_type=F32)
            + jnp.dot(a_lo, b_hi, preferred_element_type=F32))


def _out_ln_body(seg_ref, mixed_ref, x_ref, mod_ref, w_ref, lng_ref, rwh_ref, rwl_ref, rb_ref,
                 x1_ref, h2_ref, h2b_ref, ti_ref, tw_ref):
    m = mod_ref[0]
    mix = jnp.dot(mixed_ref[...], w_ref[...], preferred_element_type=F32)
    x1 = _ln_rows(DEEPNORM_ALPHA * x_ref[...] + m[0:1] * mix) * lng_ref[0:1] + lng_ref[1:2]
    x1_ref[...] = x1
    h2 = _ln_rows(x1) * (1.0 + m[2:3]) + m[1:2]
    h2_ref[...] = h2
    h2b_ref[...] = h2.astype(BF16)
    scores = jax.nn.sigmoid(_split_dot(h2, rwh_ref[...], rwl_ref[...]))
    lane = lax.broadcasted_iota(jnp.int32, scores.shape, 1).astype(F32)
    biased = jnp.where(lane < N_EXPERTS, scores + rb_ref[...], -jnp.inf)
    top_i = jnp.zeros(scores.shape, F32)
    top_s = jnp.zeros(scores.shape, F32)
    for k in range(TOP_K):
        best = jnp.max(biased, axis=-1, keepdims=True)
        pick = jnp.min(jnp.where(biased == best, lane, float(LANES)), axis=-1, keepdims=True)
        hit = lane == pick
        top_i = jnp.where(lane == k, pick, top_i)
        top_s = jnp.where(lane == k, jnp.sum(jnp.where(hit, scores, 0.0), axis=-1, keepdims=True), top_s)
        biased = jnp.where(hit, -jnp.inf, biased)
    ti_ref[...] = top_i.astype(jnp.int32)
    tw_ref[...] = top_s / jnp.sum(top_s, axis=-1, keepdims=True) * ROUTED_SCALE


def _out_proj_norm_route(mixed, x, mod, seg, w, ln_gb, router_w, router_b):
    t, d = x.shape
    tm = ROW_TILE
    rw = jnp.pad(router_w, ((0, 0), (0, LANES - N_EXPERTS)))
    rw_hi, rw_lo = _split_bf16(rw)
    rb = jnp.pad(router_b, (0, LANES - N_EXPERTS)).reshape(1, LANES)
    row = lambda width: pl.BlockSpec((tm, width), lambda i, s: (i, 0))
    const = lambda shape: pl.BlockSpec(shape, lambda i, s: (0,) * len(shape))
    grid_spec = pltpu.PrefetchScalarGridSpec(
        num_scalar_prefetch=1,
        grid=(t // tm,),
        in_specs=[row(d), row(d), pl.BlockSpec((1, 3, d), lambda i, s: (s[i], 0, 0)), const((d, d)),
                  const((2, d)), const((d, LANES)), const((d, LANES)), const((1, LANES))],
        out_specs=[row(d), row(d), row(d), row(LANES), row(LANES)],
    )
    x1, h2, h2b, top_i, top_w = pl.pallas_call(
        _out_ln_body, grid_spec=grid_spec,
        out_shape=[jax.ShapeDtypeStruct((t, d), F32), jax.ShapeDtypeStruct((t, d), F32),
                   jax.ShapeDtypeStruct((t, d), BF16), jax.ShapeDtypeStruct((t, LANES), jnp.int32),
                   jax.ShapeDtypeStruct((t, LANES), F32)],
        compiler_params=pltpu.CompilerParams(dimension_semantics=("parallel",),
                                             vmem_limit_bytes=VMEM_LIMIT_BYTES),
        name="out_proj_norm_route",
    )(seg, mixed, x, mod, w, ln_gb, rw_hi, rw_lo, rb)
    return x1, h2, h2b, top_i[:, :TOP_K], top_w[:, :TOP_K]


def _shared_body(h_ref, wgu_ref, wd_ref, o_ref):
    gu = jnp.dot(h_ref[...], wgu_ref[...], preferred_element_type=F32)
    f = wd_ref.shape[0]
    g, u = gu[:, :f], gu[:, f:]
    act = (g * jax.nn.sigmoid(g) * u).astype(BF16)
    o_ref[...] = jnp.dot(act, wd_ref[...], preferred_element_type=F32)


def _shared_expert(hb, w_gate, w_up, w_down):
    t, d = hb.shape
    f = w_gate.shape[1]
    tm = ROW_TILE
    wgu = jnp.concatenate([w_gate, w_up], axis=1).astype(BF16)
    return pl.pallas_call(
        _shared_body, grid=(t // tm,),
        in_specs=[pl.BlockSpec((tm, d), lambda i: (i, 0)), pl.BlockSpec((d, 2 * f), lambda i: (0, 0)),
                  pl.BlockSpec((f, d), lambda i: (0, 0))],
        out_specs=pl.BlockSpec((tm, d), lambda i: (i, 0)),
        out_shape=jax.ShapeDtypeStruct((t, d), F32),
        compiler_params=pltpu.CompilerParams(dimension_semantics=("parallel",),
                                             vmem_limit_bytes=VMEM_LIMIT_BYTES),
        name="shared_expert",
    )(hb, wgu, w_down.astype(BF16))


def _combine_body(seg_ref, *refs):
    y_refs, (sh_ref, x_ref, mod_ref, lng_ref, o_ref) = refs[:TOP_K], refs[TOP_K:]
    ffn = sh_ref[...]
    for y_ref in y_refs:
        ffn = ffn + y_ref[...]
    o_ref[...] = _ln_rows(DEEPNORM_ALPHA * x_ref[...] + mod_ref[0] * ffn) * lng_ref[0:1] + lng_ref[1:2]


def _combine_norm(y, shared, x1, gate, seg, ln_gb):
    t, d = x1.shape
    tm = COMBINE_ROW_TILE
    tiles = t // tm
    row = pl.BlockSpec((tm, d), lambda i, s: (i, 0))
    y_specs = [pl.BlockSpec((tm, d), lambda i, s, k=k: (k * tiles + i, 0)) for k in range(TOP_K)]
    grid_spec = pltpu.PrefetchScalarGridSpec(
        num_scalar_prefetch=1,
        grid=(tiles,),
        in_specs=y_specs + [row, row, pl.BlockSpec((1, 1, d), lambda i, s: (s[i], 0, 0)),
                            pl.BlockSpec((2, d), lambda i, s: (0, 0))],
        out_specs=row,
    )
    return pl.pallas_call(
        _combine_body, grid_spec=grid_spec,
        out_shape=jax.ShapeDtypeStruct((t, d), F32),
        compiler_params=pltpu.CompilerParams(dimension_semantics=("parallel",),
                                             vmem_limit_bytes=VMEM_LIMIT_BYTES),
        name="combine_norm",
    )(seg, *([y] * TOP_K), shared, x1, gate, ln_gb)


MOE_BLOCK_ROWS = 256
MOE_VMEM_LIMIT_BYTES = 56 * 1024 * 1024


def _moe_body(be_ref, nu_ref, idx_ref, idx_nxt_ref, wg_ref, wu_ref, wd_ref, rw_ref, h_hbm, o_hbm,
              xbuf, ybuf, wg_s, wu_s, wd_s, gsem, ssem, *, bm):
    i = pl.program_id(0)
    n_used = nu_ref[0]
    slot = i % 2

    def gather_rows(tok_ref, dst_slot):
        for r in range(bm):
            pltpu.make_async_copy(h_hbm.at[pl.ds(tok_ref[0, 0, r], 1)], xbuf.at[dst_slot, pl.ds(r, 1)],
                                  gsem.at[dst_slot]).start()

    def wait_gather(s):
        pltpu.make_async_copy(h_hbm.at[pl.ds(0, bm)], xbuf.at[s], gsem.at[s]).wait()

    def wait_scatter(s):
        pltpu.make_async_copy(ybuf.at[s], o_hbm.at[pl.ds(0, bm)], ssem.at[s]).wait()

    @pl.when(i == 0)
    def _():
        gather_rows(idx_ref, 0)

    e = be_ref[i]
    e_prev = be_ref[jnp.maximum(i - 1, 0)]

    @pl.when((i < n_used) & ((i == 0) | (e != e_prev)))
    def _():
        wg_s[...] = wg_ref[0, 0].astype(BF16)
        wu_s[...] = wu_ref[0, 0].astype(BF16)
        wd_s[...] = wd_ref[0, 0].astype(BF16)

    def block_step(s):
        wait_gather(s)

        @pl.when(i >= 2)
        def _():
            wait_scatter(s)

        gather_rows(idx_nxt_ref, 1 - s)

        xb = xbuf[s].astype(BF16)
        g = jnp.dot(xb, wg_s[...], preferred_element_type=F32)
        u = jnp.dot(xb, wu_s[...], preferred_element_type=F32)
        act = (g * jax.nn.sigmoid(g) * u).astype(BF16)
        ybuf[s] = jnp.dot(act, wd_s[...], preferred_element_type=F32) * rw_ref[...]
        for r in range(bm):
            pltpu.make_async_copy(ybuf.at[s, pl.ds(r, 1)], o_hbm.at[pl.ds(idx_ref[0, 0, bm + r], 1)],
                                  ssem.at[s]).start()

        @pl.when(i == n_used - 1)
        def _():
            @pl.when(i >= 1)
            def _():
                wait_scatter(1 - s)
            wait_scatter(s)
            wait_gather(1 - s)

    for s in range(2):
        pl.when((i < n_used) & (slot == s))(partial(block_step, s))

    @pl.when(i >= n_used)
    def _():
        xbuf[0] = jnp.zeros(xbuf.shape[1:], F32)
        fill = pltpu.make_async_copy(xbuf.at[0], o_hbm.at[pl.ds(pl.multiple_of(i * bm, bm), bm)], gsem.at[0])
        fill.start()
        fill.wait()


def _routed_experts(h, idx, row_w, block_e, n_used, w_gate, w_up, w_down, layer):
    bm = MOE_BLOCK_ROWS
    n_blocks = idx.shape[0]
    d = h.shape[1]
    f = w_gate.shape[-1]
    grid_spec = pltpu.PrefetchScalarGridSpec(
        num_scalar_prefetch=2,
        grid=(n_blocks,),
        in_specs=[
            pl.BlockSpec((1, 1, 2 * bm), lambda i, be, nu: (i, 0, 0), memory_space=pltpu.SMEM),
            pl.BlockSpec((1, 1, 2 * bm), lambda i, be, nu: (jnp.minimum(i + 1, n_blocks - 1), 0, 0),
                         memory_space=pltpu.SMEM),
            pl.BlockSpec((1, 1, d, f), lambda i, be, nu: (layer, be[i], 0, 0)),
            pl.BlockSpec((1, 1, d, f), lambda i, be, nu: (layer, be[i], 0, 0)),
            pl.BlockSpec((1, 1, f, d), lambda i, be, nu: (layer, be[i], 0, 0)),
            pl.BlockSpec((bm, 1), lambda i, be, nu: (i, 0)),
            pl.BlockSpec(memory_space=pl.ANY),
        ],
        out_specs=pl.BlockSpec(memory_space=pl.ANY),
        scratch_shapes=[pltpu.VMEM((2, bm, d), F32), pltpu.VMEM((2, bm, d), F32),
                        pltpu.VMEM((d, f), BF16), pltpu.VMEM((d, f), BF16), pltpu.VMEM((f, d), BF16),
                        pltpu.SemaphoreType.DMA((2,)), pltpu.SemaphoreType.DMA((2,))],
    )
    return pl.pallas_call(
        partial(_moe_body, bm=bm),
        grid_spec=grid_spec,
        out_shape=jax.ShapeDtypeStruct((n_blocks * bm, d), F32),
        compiler_params=pltpu.CompilerParams(
            dimension_semantics=("arbitrary",),
            vmem_limit_bytes=MOE_VMEM_LIMIT_BYTES),
        name="routed_experts",
    )(block_e, n_used, idx, idx, w_gate, w_up, w_down, row_w, h)


def _route(top_idx, top_w):
    bm = MOE_BLOCK_ROWS
    n_tok = top_idx.shape[0]
    n_assign = n_tok * TOP_K
    flat_e = top_idx.reshape(-1)
    flat_w = top_w.reshape(-1)
    order = jnp.argsort(flat_e).astype(jnp.int32)
    counts = jnp.sum(flat_e[None, :] == jnp.arange(N_EXPERTS)[:, None], axis=1).astype(jnp.int32)
    padded = (counts + bm - 1) // bm * bm
    padded_end = jnp.cumsum(padded)
    live_end = jnp.cumsum(counts)
    shift = padded_end - padded - (live_end - counts)
    n_blocks = -(-n_assign // bm) + N_EXPERTS
    block_row0 = jnp.arange(n_blocks, dtype=jnp.int32) * bm
    block_e = jnp.minimum(jnp.sum(padded_end[None, :] <= block_row0[:, None], axis=1),
                          N_EXPERTS - 1).astype(jnp.int32)
    n_used = (padded_end[-1:] // bm).astype(jnp.int32)
    rows = jnp.arange(n_blocks * bm, dtype=jnp.int32).reshape(n_blocks, bm)
    row_slot = rows - shift[block_e][:, None]
    row_live = row_slot < live_end[block_e][:, None]
    row_assign = order[jnp.where(row_live, row_slot, 0)]
    row_tok = jnp.where(row_live, row_assign // TOP_K, 0)
    row_dst = jnp.where(row_live, (row_assign % TOP_K) * n_tok + row_tok,
                        n_assign + rows - live_end[block_e][:, None])
    row_w = jnp.where(row_live, flat_w[row_assign], 0.0)
    idx = jnp.concatenate([row_tok, row_dst], axis=1).astype(jnp.int32)[:, None, :]
    return idx, row_w.reshape(-1, 1), block_e, n_used


def kernel(x, c, ctx, c_ctx, ada_w, ada_b, w_in, hy_conv_w, hy_conv_b, hy_w1, hy_b1, hy_w2, hy_b2, hy_w3, hy_freq, hy_skip, lru_conv_w, lru_conv_b, lru_wa, lru_ba, lru_wx, lru_bx, lru_lambda, att_lambda, att_subln, w_out, ln_g, ln_b, router_w, router_b, exp_w_gate, exp_w_up, exp_w_down, sh_w_gate, sh_w_up, sh_w_down):
    bsz, n_lat, d = x.shape
    n_ctx = ctx.shape[1]
    n_all = n_ctx + n_lat
    x_all = jnp.concatenate([ctx, x], axis=1).reshape(bsz * n_all, d)
    both = ((n_ctx, 'ctx'), (n_lat, 'lat'))
    seg_all = _segments(bsz, both, ROW_TILE)
    x_out = None

    for l in range(DEPTH):
        need_ctx = l < DEPTH - 1
        lam_init = 0.8 - 0.6 * math.exp(-0.3 * l)
        cond = jnp.concatenate([c, c_ctx[None, :]], axis=0)
        mod = jnp.dot(jax.nn.silu(cond), ada_w[l], precision=lax.Precision.HIGHEST) + ada_b[l]
        sh1, sc1, g1, sh2, sc2, g2 = jnp.split(mod, 6, axis=-1)
        hy_args = (hy_conv_w[l], hy_conv_b[l], hy_w1[l], hy_b1[l], hy_w2[l], hy_b2[l],
                   hy_w3[l], hy_freq[l], hy_skip[l])

        u_all = _ln_mod_matmul(x_all, jnp.stack([sh1, sc1], axis=1), seg_all,
                               w_in[l].astype(BF16)).reshape(bsz, n_all, -1)
        y_hy_l = _hyena_mixer(u_all[:, n_ctx:, :HY_IN], *hy_args)
        y_lru = _rglru_mixer(u_all, n_ctx, lru_conv_w[l], lru_conv_b[l], lru_wa[l], lru_ba[l],
                             lru_wx[l], lru_bx[l], lru_lambda[l])
        y_att_l, y_att_c = _diff_attention_mixer(u_all[..., HY_IN + LRU_IN:], n_ctx, att_lambda[l],
                                                 att_subln[l], lam_init, need_ctx)
        if need_ctx:
            y_hy = jnp.concatenate([_hyena_mixer(u_all[:, :n_ctx, :HY_IN], *hy_args), y_hy_l], axis=1)
            mixed = jnp.concatenate([y_hy.astype(BF16), y_lru, jnp.concatenate([y_att_c, y_att_l], axis=1)], axis=-1)
            x_res, seg, seg_c = x_all, seg_all, _segments(bsz, both, COMBINE_ROW_TILE)
        else:
            mixed = jnp.concatenate([y_hy_l.astype(BF16), y_lru[:, n_ctx:], y_att_l], axis=-1)
            x_res = x_all.reshape(bsz, n_all, d)[:, n_ctx:].reshape(bsz * n_lat, d)
            seg = _segments(bsz, ((n_lat, 'lat'),), ROW_TILE)
            seg_c = _segments(bsz, ((n_lat, 'lat'),), COMBINE_ROW_TILE)
        n_tok = x_res.shape[0]
        ln_gb1 = jnp.stack([ln_g[l, 0], ln_b[l, 0]])
        ln_gb2 = jnp.stack([ln_g[l, 1], ln_b[l, 1]])
        x1, h2, h2b, top_idx, top_w = _out_proj_norm_route(
            mixed.reshape(n_tok, d), x_res, jnp.stack([g1, sh2, sc2], axis=1), seg,
            w_out[l].astype(BF16), ln_gb1, router_w[l], router_b[l])

        idx, row_w, block_e, n_used = _route(top_idx, top_w)
        y = _routed_experts(h2, idx, row_w, block_e, n_used, exp_w_gate, exp_w_up, exp_w_down, l)
        shared = _shared_expert(h2b, sh_w_gate[l], sh_w_up[l], sh_w_down[l])
        x_out = _combine_norm(y, shared, x1, g2[:, None, :], seg_c, ln_gb2)
        x_all = x_out
    return x_out.reshape(bsz, n_lat, d)
```

```python
import math
from functools import lru_cache, partial

import numpy as np
import jax
import jax.numpy as jnp
from jax import lax
from jax.experimental import pallas as pl
from jax.experimental.pallas import tpu as pltpu

D_MODEL = 2048
DEPTH = 2
GRID_W = 64
F32 = jnp.float32
BF16 = jnp.bfloat16
LANES = 128
SUBLANES = 8

HY_WIDTH = D_MODEL // 4
LRU_WIDTH = D_MODEL // 4
ATT_WIDTH = D_MODEL // 2
HY_ORDER = 2
HY_CONV = 3
HY_BANDS = 8
HY_TARGET = 1e-2
HY_MAX_DECAY = math.log(HY_TARGET) / 0.3
HY_MIN_DECAY = math.log(HY_TARGET) / 1.5
HY_IN = (HY_ORDER + 1) * HY_WIDTH

LRU_HEADS = 8
LRU_HEAD_DIM = LRU_WIDTH // LRU_HEADS
LRU_CONV = 4
LRU_C = 8.0
LRU_IN = 2 * LRU_WIDTH

ATT_HEADS = 4
ATT_V_DIM = ATT_WIDTH // ATT_HEADS
ATT_HEAD_DIM = ATT_V_DIM // 2
ATT_QK = ATT_HEADS * 2 * ATT_HEAD_DIM
ROPE_THETA = 10000.0
ROPE_AXIS_DIM = ATT_HEAD_DIM // 2

N_EXPERTS = 64
TOP_K = 8
ROUTED_SCALE = 2.5

LN_EPS = 1e-5
DEEPNORM_ALPHA = (2 * DEPTH) ** 0.25

VMEM_LIMIT_BYTES = 48 * 1024 * 1024


def _depthwise_conv(x, w, b, pad_left, pad_right):
    y = lax.conv_general_dilated(
        x, w[:, None, :], window_strides=(1,),
        padding=[(pad_left, pad_right)], dimension_numbers=('NWC', 'WIO', 'NWC'),
        feature_group_count=x.shape[-1])
    return y + b


def _hyena_filter(n, w1, b1, w2, b2, w3, freq):
    pos = jnp.arange(n, dtype=F32)
    t = (pos / max(n - 1, 1))[:, None]
    bands = jnp.linspace(1e-4, HY_BANDS - 1, HY_BANDS, dtype=F32)
    ang = (2.0 * math.pi / n) * pos[:, None] * bands
    feats = jnp.concatenate([t, jnp.cos(ang), -jnp.sin(ang)], axis=-1)
    hp = lax.Precision.HIGHEST
    decay = jnp.abs(jnp.linspace(HY_MIN_DECAY, HY_MAX_DECAY, HY_WIDTH, dtype=F32))
    side_cols = HY_ORDER * HY_WIDTH

    def one_side(feats, t, w3_side):
        hid = jnp.sin(freq * (jnp.dot(feats, w1, precision=hp) + b1))
        hid = jnp.sin(freq * (jnp.dot(hid, w2, precision=hp) + b2))
        filt = jnp.dot(hid, w3_side, precision=hp).reshape(n, HY_ORDER, HY_WIDTH)
        return filt * jnp.exp(-t * decay)[:, None, :]

    fwd = one_side(feats, t, w3[:, :side_cols])
    bwd_flipped = one_side(feats[::-1], t[::-1], w3[:, side_cols:])
    two_sided = jnp.concatenate([fwd, jnp.zeros_like(fwd[:1]), bwd_flipped[:n - 1]], axis=0)
    return two_sided / jnp.sum(jnp.abs(two_sided), axis=0, keepdims=True)


DFT_BLOCK_BYTES = 4 * 1024 * 1024
DFT_MINOR_LEN = 128
DFT_SINGLE_LEVEL_MAX = 1024


def _bmm3_body(lh_ref, ll_ref, r_ref, o_ref, *, nb, shared):
    for b in range(nb):
        r = r_ref[b]
        r_hi = r.astype(BF16)
        r_lo = (r - r_hi.astype(F32)).astype(BF16)
        lh = lh_ref[0 if shared else b]
        ll = ll_ref[0 if shared else b]
        o_ref[b] = (jnp.dot(lh, r_hi, preferred_element_type=F32)
                    + jnp.dot(lh, r_lo, preferred_element_type=F32)
                    + jnp.dot(ll, r_hi, preferred_element_type=F32))


def _bmm3(l_hi, l_lo, r, nb):
    batch, k, c = r.shape
    m = l_hi.shape[1]
    shared = l_hi.shape[0] == 1
    nb = max(1, min(nb, batch, DFT_BLOCK_BYTES // (4 * c * max(k, m))))
    assert batch % nb == 0
    l_spec = pl.BlockSpec((1 if shared else nb, m, k), (lambda i: (0, 0, 0)) if shared else (lambda i: (i, 0, 0)))
    return pl.pallas_call(
        partial(_bmm3_body, nb=nb, shared=shared),
        grid=(batch // nb,),
        in_specs=[l_spec, l_spec, pl.BlockSpec((nb, k, c), lambda i: (i, 0, 0))],
        out_specs=pl.BlockSpec((nb, m, c), lambda i: (i, 0, 0)),
        out_shape=jax.ShapeDtypeStruct((batch, m, c), F32),
        compiler_params=pltpu.CompilerParams(dimension_semantics=("parallel",), vmem_limit_bytes=VMEM_LIMIT_BYTES),
        name="dft_matmul",
    )(l_hi, l_lo, r)


def _stack_complex(f):
    return np.concatenate([np.concatenate([f.real, -f.imag], axis=-1),
                           np.concatenate([f.imag, f.real], axis=-1)], axis=-2)


def _split_bf16(a):
    a = jnp.asarray(a, F32)
    hi = a.astype(BF16)
    return hi, (a - hi.astype(F32)).astype(BF16)


@lru_cache(maxsize=None)
def _dft_tables(n1_io, n1, n2):
    n = n1 * n2
    k1 = np.arange(n1)
    a = np.arange(n1_io)
    b = np.arange(n2)
    ang = -2.0 * np.pi * (((np.outer(k1, a) * n2)[None] + b[:, None, None] * k1[None, :, None]) % n) / n
    fa = np.exp(1j * ang)
    fb = np.exp(-2j * np.pi * (np.outer(b, b) % n2) / n2)
    return dict(
        fwd_a=_stack_complex(fa),
        fwd_a_real=np.concatenate([fa.real, fa.imag], axis=-2),
        fwd_b=_stack_complex(fb)[None],
        inv_b=_stack_complex(np.conj(fb).T)[None],
        inv_a=_stack_complex(np.conj(fa).transpose(0, 2, 1) / n),
    )


def _dft_split(length):
    if length <= DFT_SINGLE_LEVEL_MAX:
        return 1, length
    return length // DFT_MINOR_LEN, DFT_MINOR_LEN


def _swap_stage(x, n1, n2):
    c = x.shape[-1]
    return x.reshape(n2, 2, n1, c).transpose(2, 1, 0, 3).reshape(n1, 2 * n2, c)


def _dft_forward(rows, table, n1, n2, tabs):
    if n1 > 1:
        rows = _bmm3(*_split_bf16(tabs[table]), rows, 16)
    return _bmm3(*_split_bf16(tabs['fwd_b']), _swap_stage(rows, n1, n2), 8)


def _filter_spectrum(two_sided):
    length, c = two_sided.shape
    n1, n2 = _dft_split(length)
    tabs = _dft_tables(n1, n1, n2)
    rows = two_sided.reshape(n1, n2, c).transpose(1, 0, 2)
    if n1 == 1:
        rows = jnp.concatenate([rows, jnp.zeros_like(rows)], axis=1)
    return _dft_forward(rows, 'fwd_a_real', n1, n2, tabs)


def _long_conv_pair(z, spec):
    pair, n, c = z.shape
    assert pair == 2
    n1, n2 = _dft_split(2 * n)
    n1_io = max(n1 // 2, 1)
    tabs = _dft_tables(n1_io, n1, n2)
    if n1 > 1:
        rows = z.reshape(2, n1_io, n2, c).transpose(2, 0, 1, 3).reshape(n2, 2 * n1_io, c)
    else:
        rows = jnp.pad(z, ((0, 0), (0, n), (0, 0))).transpose(1, 0, 2)
    y = _dft_forward(rows, 'fwd_a', n1, n2, tabs)
    yr, yi = y[:, :n2], y[:, n2:]
    hr, hi = spec[:, :n2], spec[:, n2:]
    y = jnp.concatenate([yr * hr - yi * hi, yr * hi + yi * hr], axis=1)
    d = _bmm3(*_split_bf16(tabs['inv_b']), y, 8)
    d = _swap_stage(d, n2, n1)
    if n1 > 1:
        out = _bmm3(*_split_bf16(tabs['inv_a']), d, 16)
        return out.reshape(n2, 2, n1_io, c).transpose(1, 2, 0, 3).reshape(2, n, c)
    return (d.reshape(n2, 2, c).transpose(1, 0, 2) / (n1 * n2))[:, :n]


def _hyena_mixer(u, conv_w, conv_b, w1, b1, w2, b2, w3, freq, skip):
    n = u.shape[1]
    u = _depthwise_conv(u, conv_w, conv_b, HY_CONV // 2, HY_CONV - 1 - HY_CONV // 2)
    gates = (u[..., :HY_WIDTH], u[..., HY_WIDTH:2 * HY_WIDTH])
    z = u[..., 2 * HY_WIDTH:]
    two_sided = _hyena_filter(n, w1, b1, w2, b2, w3, freq)
    spec = _filter_spectrum(two_sided.reshape(2 * n, HY_ORDER * HY_WIDTH))
    for o in range(HY_ORDER):
        zc = _long_conv_pair(z, spec[..., o * HY_WIDTH:(o + 1) * HY_WIDTH])
        z = gates[o] * (zc + skip[o] * z)
    return z


LRU_CONV_MARGIN = SUBLANES
LRU_TIME_CHUNK = 1024


def _gelu_tanh(x):
    return 0.5 * x * (1.0 + jnp.tanh(math.sqrt(2.0 / math.pi) * (x + 0.044715 * (x * x * x))))


def _rglru_body(ug_ref, ur_ref, cw_ref, cb_ref, wa_ref, ba_ref, wx_ref, bx_ref, cl_ref, o_ref,
                pad_s, af_s, bf_s, ar_s, br_s, hf_s, hr_s, *, n_ctx, n_lat):
    a_s = (af_s, ar_s)
    b_s = (bf_s, br_s)
    mg = LRU_CONV_MARGIN
    cw = cw_ref[...]
    zeros_margin = jnp.zeros((mg, LANES), F32)
    for r0, n in ((0, n_ctx), (n_ctx, n_lat)):
        pad_s[0:mg, :] = zeros_margin
        pad_s[mg:mg + n, :] = ur_ref[0, r0:r0 + n, :]
        pad_s[mg + n:2 * mg + n, :] = zeros_margin
        chunk = min(n, LRU_TIME_CHUNK)
        for c0 in range(0, n, chunk):
            xr = cb_ref[...] + cw[0:1] * pad_s[mg + c0 - 2:mg + c0 - 2 + chunk, :]
            for j in range(1, LRU_CONV):
                xr = xr + cw[j:j + 1] * pad_s[mg + c0 + j - 2:mg + c0 + j - 2 + chunk, :]
            xb = xr.astype(BF16)
            for d in range(2):
                r = jax.nn.sigmoid(jnp.dot(xb, wa_ref[d, 0], preferred_element_type=F32) + ba_ref[d:d + 1, :])
                i = jax.nn.sigmoid(jnp.dot(xb, wx_ref[d, 0], preferred_element_type=F32) + bx_ref[d:d + 1, :])
                log_a = cl_ref[d:d + 1, :] * r
                a_s[d][r0 + c0:r0 + c0 + chunk, :] = jnp.exp(log_a)
                b_s[d][r0 + c0:r0 + c0 + chunk, :] = jnp.sqrt(1.0 - jnp.exp(2.0 * log_a)) * (i * xr)

    def scan_segment(r0, n, carry):
        def step(t, hc):
            hf, hr = hc
            tf = r0 + t
            tr = r0 + n - 1 - t
            hf = af_s[pl.ds(tf, 1), :] * hf + bf_s[pl.ds(tf, 1), :]
            hr = ar_s[pl.ds(tr, 1), :] * hr + br_s[pl.ds(tr, 1), :]
            hf_s[pl.ds(tf, 1), :] = hf
            hr_s[pl.ds(tr, 1), :] = hr
            return hf, hr
        return lax.fori_loop(0, n, step, carry, unroll=8)

    h0 = jnp.zeros((1, LANES), F32)
    carry = scan_segment(0, n_ctx, (h0, h0))
    scan_segment(n_ctx, n_lat, carry)
    o_ref[0] = ((hf_s[...] + hr_s[...]) * _gelu_tanh(ug_ref[0])).astype(o_ref.dtype)


def _block_diag_tiles(w):
    w = w.reshape(2, LRU_HEADS // 2, 2, LRU_HEAD_DIM, LRU_HEAD_DIM)
    z = jnp.zeros_like(w[:, :, 0])
    top = jnp.concatenate([w[:, :, 0], z], axis=-1)
    bot = jnp.concatenate([z, w[:, :, 1]], axis=-1)
    return jnp.concatenate([top, bot], axis=-2).astype(BF16)


def _rglru_mixer(u_all, n_ctx, conv_w, conv_b, wa, ba, wx, bx, lam):
    bsz, n_all, _ = u_all.shape
    n_lat = n_all - n_ctx
    n_tiles = LRU_WIDTH // LANES
    gate_blk0 = HY_IN // LANES
    rec_blk0 = (HY_IN + LRU_WIDTH) // LANES
    decay_scale = -LRU_C * jax.nn.softplus(-lam)
    seq = lambda col0: pl.BlockSpec((1, n_all, LANES), lambda b, j: (b, 0, col0 + j))
    vec = lambda rows: pl.BlockSpec((rows, LANES), lambda b, j: (0, j))
    mat = pl.BlockSpec((2, 1, LANES, LANES), lambda b, j: (0, j, 0, 0))
    full = pltpu.VMEM((n_all, LANES), F32)
    return pl.pallas_call(
        partial(_rglru_body, n_ctx=n_ctx, n_lat=n_lat),
        grid=(bsz, n_tiles),
        in_specs=[seq(gate_blk0), seq(rec_blk0), vec(LRU_CONV), vec(1), mat, vec(2), mat, vec(2), vec(2)],
        out_specs=pl.BlockSpec((1, n_all, LANES), lambda b, j: (b, 0, j)),
        out_shape=jax.ShapeDtypeStruct((bsz, n_all, LRU_WIDTH), BF16),
        scratch_shapes=[pltpu.VMEM((n_lat + 2 * LRU_CONV_MARGIN, LANES), F32),
                        full, full, full, full, full, full],
        compiler_params=pltpu.CompilerParams(
            dimension_semantics=("parallel", "parallel"),
            vmem_limit_bytes=VMEM_LIMIT_BYTES),
        name="rglru",
    )(u_all, u_all, conv_w, conv_b.reshape(1, LRU_WIDTH), _block_diag_tiles(wa), ba,
      _block_diag_tiles(wx), bx, decay_scale)


ATT_Q_TILE = 256
ATT_KV_CHUNK = 2048


def _attn_body(q_ref, k_ref, v_ref, lv_ref, g_ref, o_ref, *, lam_init):
    lv = lv_ref[...]
    lam = (jnp.exp(jnp.sum(lv[0:1] * lv[1:2], keepdims=True))
           - jnp.exp(jnp.sum(lv[2:3] * lv[3:4], keepdims=True)) + lam_init)
    q = q_ref[0]
    nk = k_ref.shape[1]
    work = [(m, c0, min(c0 + ATT_KV_CHUNK, nk)) for c0 in range(0, nk, ATT_KV_CHUNK) for m in range(2)]

    def scores(m, c0, c1):
        qm = q[:, m * ATT_HEAD_DIM:(m + 1) * ATT_HEAD_DIM]
        km = k_ref[0, c0:c1, m * ATT_HEAD_DIM:(m + 1) * ATT_HEAD_DIM]
        return lax.dot_general(qm, km, (((1,), (1,)), ((), ())), preferred_element_type=F32)

    state = [None, None]
    s_next = scores(*work[0])
    for j, (m, c0, c1) in enumerate(work):
        s = s_next
        if j + 1 < len(work):
            s_next = scores(*work[j + 1])
        chunk_max = jnp.max(s, axis=-1, keepdims=True)
        v = v_ref[0, c0:c1, :]
        if state[m] is None:
            p = jnp.exp(s - chunk_max)
            state[m] = (chunk_max, jnp.sum(p, axis=-1, keepdims=True),
                        jnp.dot(p.astype(BF16), v, preferred_element_type=F32))
        else:
            mx_old, sum_old, acc_old = state[m]
            mx = jnp.maximum(mx_old, chunk_max)
            rescale = jnp.exp(mx_old - mx)
            p = jnp.exp(s - mx)
            state[m] = (mx, rescale * sum_old + jnp.sum(p, axis=-1, keepdims=True),
                        rescale * acc_old + jnp.dot(p.astype(BF16), v, preferred_element_type=F32))
    o = state[0][2] / state[0][1] - lam * (state[1][2] / state[1][1])
    y = o * lax.rsqrt(jnp.mean(o * o, axis=-1, keepdims=True) + LN_EPS) * g_ref[...] * (1.0 - lam_init)
    o_ref[0] = y.astype(o_ref.dtype)


def _diff_attention(qkv, lam_vecs, gain, lam_init, q_row0, n_q, n_k):
    bsz = qkv.shape[0]
    tq = ATT_Q_TILE
    q_blk0 = q_row0 // tq
    n_heads = ATT_HEADS
    return pl.pallas_call(
        partial(_attn_body, lam_init=lam_init),
        grid=(bsz, n_heads, n_q // tq),
        in_specs=[
            pl.BlockSpec((1, tq, ATT_V_DIM), lambda b, h, i: (b, q_blk0 + i, h)),
            pl.BlockSpec((1, n_k, ATT_V_DIM), lambda b, h, i: (b, 0, n_heads + h)),
            pl.BlockSpec((1, n_k, ATT_V_DIM), lambda b, h, i: (b, 0, 2 * n_heads + h)),
            pl.BlockSpec((4, ATT_HEAD_DIM), lambda b, h, i: (0, 0)),
            pl.BlockSpec((1, ATT_V_DIM), lambda b, h, i: (0, 0)),
        ],
        out_specs=pl.BlockSpec((1, tq, ATT_V_DIM), lambda b, h, i: (b, i, h)),
        out_shape=jax.ShapeDtypeStruct((bsz, n_q, ATT_WIDTH), BF16),
        compiler_params=pltpu.CompilerParams(
            dimension_semantics=("parallel", "parallel", "arbitrary"),
            vmem_limit_bytes=VMEM_LIMIT_BYTES),
        name="diff_attention",
    )(qkv, qkv, qkv, lam_vecs, gain.reshape(1, ATT_V_DIM))


def _axial_rope_tables(n):
    rows = n // GRID_W
    row = jnp.broadcast_to(jnp.arange(rows, dtype=F32)[:, None], (rows, GRID_W)).reshape(-1)
    col = jnp.broadcast_to(jnp.arange(GRID_W, dtype=F32)[None, :], (rows, GRID_W)).reshape(-1)
    inv = ROPE_THETA ** (-jnp.arange(0, ROPE_AXIS_DIM, 2, dtype=F32) / ROPE_AXIS_DIM)
    ang = jnp.stack([row[:, None] * inv, col[:, None] * inv], axis=1)
    return jnp.cos(ang), jnp.sin(ang)


def _apply_axial_rope(x, cos, sin):
    xs = x.reshape(x.shape[:-1] + (2, 2, ROPE_AXIS_DIM // 2))
    x1, x2 = xs[..., 0, :], xs[..., 1, :]
    cb, sb = cos[None, :, None, None], sin[None, :, None, None]
    out = jnp.stack([x1 * cb - x2 * sb, x2 * cb + x1 * sb], axis=-2)
    return out.reshape(x.shape)


def _diff_attention_mixer(u_att, n_ctx, lam_vecs, subln_gain, lam_init, need_ctx):
    bsz, n_all, _ = u_att.shape
    n_lat = n_all - n_ctx
    scale = ATT_HEAD_DIM ** -0.5
    u_lat = u_att[:, n_ctx:]
    u_ctx = u_att[:, :n_ctx]
    q_l = u_lat[..., :ATT_QK].reshape(bsz, n_lat, ATT_HEADS, 2, ATT_HEAD_DIM)
    k_l = u_lat[..., ATT_QK:2 * ATT_QK].reshape(bsz, n_lat, ATT_HEADS, 2, ATT_HEAD_DIM)
    cos, sin = _axial_rope_tables(n_lat)
    q_l = _apply_axial_rope(q_l, cos, sin) * scale
    k_l = _apply_axial_rope(k_l, cos, sin)
    lat = jnp.concatenate([q_l.reshape(bsz, n_lat, ATT_QK), k_l.reshape(bsz, n_lat, ATT_QK),
                           u_lat[..., 2 * ATT_QK:]], axis=-1).astype(BF16)
    cx = jnp.concatenate([u_ctx[..., :ATT_QK] * scale, u_ctx[..., ATT_QK:]], axis=-1).astype(BF16)
    qkv = jnp.concatenate([cx, lat], axis=1)
    y_lat = _diff_attention(qkv, lam_vecs, subln_gain, lam_init, n_ctx, n_lat, n_all)
    y_ctx = _diff_attention(qkv, lam_vecs, subln_gain, lam_init, 0, n_ctx, n_ctx) if need_ctx else None
    return y_lat, y_ctx


ROW_TILE = 256
IN_PROJ_COL_TILE = 1408
COMBINE_ROW_TILE = 128


def _ln_rows(x):
    mu = jnp.mean(x, axis=-1, keepdims=True)
    xc = x - mu
    return xc * lax.rsqrt(jnp.mean(xc * xc, axis=-1, keepdims=True) + LN_EPS)


def _segments(bsz, seg_rows, tile):
    ids = []
    for b in range(bsz):
        for rows, kind in seg_rows:
            assert rows % tile == 0
            ids += [bsz if kind == 'ctx' else b] * (rows // tile)
    return jnp.asarray(ids, jnp.int32)


def _ln_mm_body(seg_ref, x_ref, mod_ref, w_ref, o_ref):
    m = mod_ref[0]
    h = _ln_rows(x_ref[...]) * (1.0 + m[1:2]) + m[0:1]
    o_ref[...] = jnp.dot(h.astype(BF16), w_ref[...], preferred_element_type=F32)


def _ln_mod_matmul(x, mod, seg, w):
    t, d = x.shape
    n = w.shape[1]
    tm, tn = ROW_TILE, IN_PROJ_COL_TILE
    assert t % tm == 0 and n % tn == 0
    grid_spec = pltpu.PrefetchScalarGridSpec(
        num_scalar_prefetch=1,
        grid=(n // tn, t // tm),
        in_specs=[pl.BlockSpec((tm, d), lambda j, i, s: (i, 0)),
                  pl.BlockSpec((1, 2, d), lambda j, i, s: (s[i], 0, 0)),
                  pl.BlockSpec((d, tn), lambda j, i, s: (0, j))],
        out_specs=pl.BlockSpec((tm, tn), lambda j, i, s: (i, j)),
    )
    return pl.pallas_call(
        _ln_mm_body, grid_spec=grid_spec,
        out_shape=jax.ShapeDtypeStruct((t, n), F32),
        compiler_params=pltpu.CompilerParams(dimension_semantics=("parallel", "arbitrary"),
                                             vmem_limit_bytes=VMEM_LIMIT_BYTES),
        name="ln_mod_in_proj",
    )(seg, x, mod, w)


def _split_dot(a, b_hi, b_lo):
    a_hi = a.astype(BF16)
    a_lo = (a - a_hi.astype(F32)).astype(BF16)
    return (jnp.dot(a_hi, b_hi, preferred_element_type=F32) + jnp.dot(a_hi, b_lo, preferred_element_type=F32)
            + jnp.dot(a_lo, b_hi, preferred_element_type=F32))


def _out_ln_body(seg_ref, mixed_ref, x_ref, mod_ref, w_ref, lng_ref, rwh_ref, rwl_ref, rb_ref,
                 x1_ref, h2_ref, h2b_ref, ti_ref, tw_ref):
    m = mod_ref[0]
    mix = jnp.dot(mixed_ref[...], w_ref[...], preferred_element_type=F32)
    x1 = _ln_rows(DEEPNORM_ALPHA * x_ref[...] + m[0:1] * mix) * lng_ref[0:1] + lng_ref[1:2]
    x1_ref[...] = x1
    h2 = _ln_rows(x1) * (1.0 + m[2:3]) + m[1:2]
    h2_ref[...] = h2
    h2b_ref[...] = h2.astype(BF16)
    scores = jax.nn.sigmoid(_split_dot(h2, rwh_ref[...], rwl_ref[...]))
    lane = lax.broadcasted_iota(jnp.int32, scores.shape, 1).astype(F32)
    biased = jnp.where(lane < N_EXPERTS, scores + rb_ref[...], -jnp.inf)
    top_i = jnp.zeros(scores.shape, F32)
    top_s = jnp.zeros(scores.shape, F32)
    for k in range(TOP_K):
        best = jnp.max(biased, axis=-1, keepdims=True)
        pick = jnp.min(jnp.where(biased == best, lane, float(LANES)), axis=-1, keepdims=True)
        hit = lane == pick
        top_i = jnp.where(lane == k, pick, top_i)
        top_s = jnp.where(lane == k, jnp.sum(jnp.where(hit, scores, 0.0), axis=-1, keepdims=True), top_s)
        biased = jnp.where(hit, -jnp.inf, biased)
    ti_ref[...] = top_i.astype(jnp.int32)
    tw_ref[...] = top_s / jnp.sum(top_s, axis=-1, keepdims=True) * ROUTED_SCALE


def _out_proj_norm_route(mixed, x, mod, seg, w, ln_gb, router_w, router_b):
    t, d = x.shape
    tm = ROW_TILE
    rw = jnp.pad(router_w, ((0, 0), (0, LANES - N_EXPERTS)))
    rw_hi, rw_lo = _split_bf16(rw)
    rb = jnp.pad(router_b, (0, LANES - N_EXPERTS)).reshape(1, LANES)
    row = lambda width: pl.BlockSpec((tm, width), lambda i, s: (i, 0))
    const = lambda shape: pl.BlockSpec(shape, lambda i, s: (0,) * len(shape))
    grid_spec = pltpu.PrefetchScalarGridSpec(
        num_scalar_prefetch=1,
        grid=(t // tm,),
        in_specs=[row(d), row(d), pl.BlockSpec((1, 3, d), lambda i, s: (s[i], 0, 0)), const((d, d)),
                  const((2, d)), const((d, LANES)), const((d, LANES)), const((1, LANES))],
        out_specs=[row(d), row(d), row(d), row(LANES), row(LANES)],
    )
    x1, h2, h2b, top_i, top_w = pl.pallas_call(
        _out_ln_body, grid_spec=grid_spec,
        out_shape=[jax.ShapeDtypeStruct((t, d), F32), jax.ShapeDtypeStruct((t, d), F32),
                   jax.ShapeDtypeStruct((t, d), BF16), jax.ShapeDtypeStruct((t, LANES), jnp.int32),
                   jax.ShapeDtypeStruct((t, LANES), F32)],
        compiler_params=pltpu.CompilerParams(dimension_semantics=("parallel",),
                                             vmem_limit_bytes=VMEM_LIMIT_BYTES),
        name="out_proj_norm_route",
    )(seg, mixed, x, mod, w, ln_gb, rw_hi, rw_lo, rb)
    return x1, h2, h2b, top_i[:, :TOP_K], top_w[:, :TOP_K]


def _shared_body(h_ref, wgu_ref, wd_ref, o_ref):
    gu = jnp.dot(h_ref[...], wgu_ref[...], preferred_element_type=F32)
    f = wd_ref.shape[0]
    g, u = gu[:, :f], gu[:, f:]
    act = (g * jax.nn.sigmoid(g) * u).astype(BF16)
    o_ref[...] = jnp.dot(act, wd_ref[...], preferred_element_type=F32)


def _shared_expert(hb, w_gate, w_up, w_down):
    t, d = hb.shape
    f = w_gate.shape[1]
    tm = ROW_TILE
    wgu = jnp.concatenate([w_gate, w_up], axis=1).astype(BF16)
    return pl.pallas_call(
        _shared_body, grid=(t // tm,),
        in_specs=[pl.BlockSpec((tm, d), lambda i: (i, 0)), pl.BlockSpec((d, 2 * f), lambda i: (0, 0)),
                  pl.BlockSpec((f, d), lambda i: (0, 0))],
        out_specs=pl.BlockSpec((tm, d), lambda i: (i, 0)),
        out_shape=jax.ShapeDtypeStruct((t, d), F32),
        compiler_params=pltpu.CompilerParams(dimension_semantics=("parallel",),
                                             vmem_limit_bytes=VMEM_LIMIT_BYTES),
        name="shared_expert",
    )(hb, wgu, w_down.astype(BF16))


def _combine_body(seg_ref, *refs):
    y_refs, (sh_ref, x_ref, mod_ref, lng_ref, o_ref) = refs[:TOP_K], refs[TOP_K:]
    ffn = sh_ref[...]
    for y_ref in y_refs:
        ffn = ffn + y_ref[...]
    o_ref[...] = _ln_rows(DEEPNORM_ALPHA * x_ref[...] + mod_ref[0] * ffn) * lng_ref[0:1] + lng_ref[1:2]


def _combine_norm(y, shared, x1, gate, seg, ln_gb):
    t, d = x1.shape
    tm = COMBINE_ROW_TILE
    tiles = t // tm
    row = pl.BlockSpec((tm, d), lambda i, s: (i, 0))
    y_specs = [pl.BlockSpec((tm, d), lambda i, s, k=k: (k * tiles + i, 0)) for k in range(TOP_K)]
    grid_spec = pltpu.PrefetchScalarGridSpec(
        num_scalar_prefetch=1,
        grid=(tiles,),
        in_specs=y_specs + [row, row, pl.BlockSpec((1, 1, d), lambda i, s: (s[i], 0, 0)),
                            pl.BlockSpec((2, d), lambda i, s: (0, 0))],
        out_specs=row,
    )
    return pl.pallas_call(
        _combine_body, grid_spec=grid_spec,
        out_shape=jax.ShapeDtypeStruct((t, d), F32),
        compiler_params=pltpu.CompilerParams(dimension_semantics=("parallel",),
                                             vmem_limit_bytes=VMEM_LIMIT_BYTES),
        name="combine_norm",
    )(seg, *([y] * TOP_K), shared, x1, gate, ln_gb)


MOE_BLOCK_ROWS = 256
MOE_VMEM_LIMIT_BYTES = 56 * 1024 * 1024


def _moe_body(be_ref, nu_ref, ne_ref, idx_ref, idx_nxt_ref, rw_ref, wg_hbm, wu_hbm, wd_hbm, h_hbm, o_hbm,
              xbuf, ybuf, wg_f, wu_f, wd_f, wg_s, wu_s, wd_s, gsem, ssem, wsem, *, bm, layer):
    i = pl.program_id(0)
    n_used = nu_ref[0]
    slot = i % 2

    def weight_copies(expert):
        return (pltpu.make_async_copy(wg_hbm.at[layer, expert], wg_f, wsem.at[0]),
                pltpu.make_async_copy(wu_hbm.at[layer, expert], wu_f, wsem.at[1]),
                pltpu.make_async_copy(wd_hbm.at[layer, expert], wd_f, wsem.at[2]))

    def gather_rows(tok_ref, dst_slot):
        for r in range(bm):
            pltpu.make_async_copy(h_hbm.at[pl.ds(tok_ref[0, 0, r], 1)], xbuf.at[dst_slot, pl.ds(r, 1)],
                                  gsem.at[dst_slot]).start()

    def wait_gather(s):
        pltpu.make_async_copy(h_hbm.at[pl.ds(0, bm)], xbuf.at[s], gsem.at[s]).wait()

    def wait_scatter(s):
        pltpu.make_async_copy(ybuf.at[s], o_hbm.at[pl.ds(0, bm)], ssem.at[s]).wait()

    e = be_ref[i]
    e_prev = be_ref[jnp.maximum(i - 1, 0)]

    @pl.when(i == 0)
    def _():
        gather_rows(idx_ref, 0)
        for copy in weight_copies(e):
            copy.start()

    @pl.when((i < n_used) & ((i == 0) | (e != e_prev)))
    def _():
        for copy in weight_copies(e):
            copy.wait()
        wg_s[...] = wg_f[...].astype(BF16)
        wu_s[...] = wu_f[...].astype(BF16)
        wd_s[...] = wd_f[...].astype(BF16)

        @pl.when(ne_ref[i] >= 0)
        def _():
            for copy in weight_copies(ne_ref[i]):
                copy.start()

    def block_step(s):
        wait_gather(s)

        @pl.when(i >= 2)
        def _():
            wait_scatter(s)

        gather_rows(idx_nxt_ref, 1 - s)

        xb = xbuf[s].astype(BF16)
        g = jnp.dot(xb, wg_s[...], preferred_element_type=F32)
        u = jnp.dot(xb, wu_s[...], preferred_element_type=F32)
        act = (g * jax.nn.sigmoid(g) * u).astype(BF16)
        ybuf[s] = jnp.dot(act, wd_s[...], preferred_element_type=F32) * rw_ref[...]
        for r in range(bm):
            pltpu.make_async_copy(ybuf.at[s, pl.ds(r, 1)], o_hbm.at[pl.ds(idx_ref[0, 0, bm + r], 1)],
                                  ssem.at[s]).start()

        @pl.when(i == n_used - 1)
        def _():
            @pl.when(i >= 1)
            def _():
                wait_scatter(1 - s)
            wait_scatter(s)
            wait_gather(1 - s)

    for s in range(2):
        pl.when((i < n_used) & (slot == s))(partial(block_step, s))

    @pl.when(i >= n_used)
    def _():
        xbuf[0] = jnp.zeros(xbuf.shape[1:], F32)
        fill = pltpu.make_async_copy(xbuf.at[0], o_hbm.at[pl.ds(pl.multiple_of(i * bm, bm), bm)], gsem.at[0])
        fill.start()
        fill.wait()


def _routed_experts(h, idx, row_w, block_e, n_used, next_e, w_gate, w_up, w_down, layer):
    bm = MOE_BLOCK_ROWS
    n_blocks = idx.shape[0]
    d = h.shape[1]
    f = w_gate.shape[-1]
    anywhere = pl.BlockSpec(memory_space=pl.ANY)
    grid_spec = pltpu.PrefetchScalarGridSpec(
        num_scalar_prefetch=3,
        grid=(n_blocks,),
        in_specs=[
            pl.BlockSpec((1, 1, 2 * bm), lambda i, be, nu, ne: (i, 0, 0), memory_space=pltpu.SMEM),
            pl.BlockSpec((1, 1, 2 * bm), lambda i, be, nu, ne: (jnp.minimum(i + 1, n_blocks - 1), 0, 0),
                         memory_space=pltpu.SMEM),
            pl.BlockSpec((bm, 1), lambda i, be, nu, ne: (i, 0)),
            anywhere, anywhere, anywhere, anywhere,
        ],
        out_specs=anywhere,
        scratch_shapes=[pltpu.VMEM((2, bm, d), F32), pltpu.VMEM((2, bm, d), F32),
                        pltpu.VMEM((d, f), F32), pltpu.VMEM((d, f), F32), pltpu.VMEM((f, d), F32),
                        pltpu.VMEM((d, f), BF16), pltpu.VMEM((d, f), BF16), pltpu.VMEM((f, d), BF16),
                        pltpu.SemaphoreType.DMA((2,)), pltpu.SemaphoreType.DMA((2,)),
                        pltpu.SemaphoreType.DMA((3,))],
    )
    return pl.pallas_call(
        partial(_moe_body, bm=bm, layer=layer),
        grid_spec=grid_spec,
        out_shape=jax.ShapeDtypeStruct((n_blocks * bm, d), F32),
        compiler_params=pltpu.CompilerParams(
            dimension_semantics=("arbitrary",),
            vmem_limit_bytes=MOE_VMEM_LIMIT_BYTES),
        name="routed_experts",
    )(block_e, n_used, next_e, idx, idx, row_w, w_gate, w_up, w_down, h)


def _route(top_idx, top_w):
    bm = MOE_BLOCK_ROWS
    n_tok = top_idx.shape[0]
    n_assign = n_tok * TOP_K
    flat_e = top_idx.reshape(-1)
    flat_w = top_w.reshape(-1)
    order = jnp.argsort(flat_e).astype(jnp.int32)
    counts = jnp.sum(flat_e[None, :] == jnp.arange(N_EXPERTS)[:, None], axis=1).astype(jnp.int32)
    padded = (counts + bm - 1) // bm * bm
    padded_end = jnp.cumsum(padded)
    live_end = jnp.cumsum(counts)
    shift = padded_end - padded - (live_end - counts)
    n_blocks = -(-n_assign // bm) + N_EXPERTS
    block_row0 = jnp.arange(n_blocks, dtype=jnp.int32) * bm
    block_e = jnp.minimum(jnp.sum(padded_end[None, :] <= block_row0[:, None], axis=1),
                          N_EXPERTS - 1).astype(jnp.int32)
    n_used = (padded_end[-1:] // bm).astype(jnp.int32)
    experts = jnp.arange(N_EXPERTS, dtype=jnp.int32)
    later = jnp.where((experts[None, :] > experts[:, None]) & (counts[None, :] > 0), experts[None, :], N_EXPERTS)
    next_live = jnp.min(later, axis=1)
    next_e = jnp.where(next_live < N_EXPERTS, next_live, -1)[block_e].astype(jnp.int32)
    rows = jnp.arange(n_blocks * bm, dtype=jnp.int32).reshape(n_blocks, bm)
    row_slot = rows - shift[block_e][:, None]
    row_live = row_slot < live_end[block_e][:, None]
    row_assign = order[jnp.where(row_live, row_slot, 0)]
    row_tok = jnp.where(row_live, row_assign // TOP_K, 0)
    row_dst = jnp.where(row_live, (row_assign % TOP_K) * n_tok + row_tok,
                        n_assign + rows - live_end[block_e][:, None])
    row_w = jnp.where(row_live, flat_w[row_assign], 0.0)
    idx = jnp.concatenate([row_tok, row_dst], axis=1).astype(jnp.int32)[:, None, :]
    return idx, row_w.reshape(-1, 1), block_e, n_used, next_e


def kernel(x, c, ctx, c_ctx, ada_w, ada_b, w_in, hy_conv_w, hy_conv_b, hy_w1, hy_b1, hy_w2, hy_b2, hy_w3, hy_freq, hy_skip, lru_conv_w, lru_conv_b, lru_wa, lru_ba, lru_wx, lru_bx, lru_lambda, att_lambda, att_subln, w_out, ln_g, ln_b, router_w, router_b, exp_w_gate, exp_w_up, exp_w_down, sh_w_gate, sh_w_up, sh_w_down):
    bsz, n_lat, d = x.shape
    n_ctx = ctx.shape[1]
    n_all = n_ctx + n_lat
    x_all = jnp.concatenate([ctx, x], axis=1).reshape(bsz * n_all, d)
    both = ((n_ctx, 'ctx'), (n_lat, 'lat'))
    seg_all = _segments(bsz, both, ROW_TILE)
    x_out = None

    for l in range(DEPTH):
        need_ctx = l < DEPTH - 1
        lam_init = 0.8 - 0.6 * math.exp(-0.3 * l)
        cond = jnp.concatenate([c, c_ctx[None, :]], axis=0)
        mod = jnp.dot(jax.nn.silu(cond), ada_w[l], precision=lax.Precision.HIGHEST) + ada_b[l]
        sh1, sc1, g1, sh2, sc2, g2 = jnp.split(mod, 6, axis=-1)
        hy_args = (hy_conv_w[l], hy_conv_b[l], hy_w1[l], hy_b1[l], hy_w2[l], hy_b2[l],
                   hy_w3[l], hy_freq[l], hy_skip[l])

        u_all = _ln_mod_matmul(x_all, jnp.stack([sh1, sc1], axis=1), seg_all,
                               w_in[l].astype(BF16)).reshape(bsz, n_all, -1)
        y_hy_l = _hyena_mixer(u_all[:, n_ctx:, :HY_IN], *hy_args)
        y_lru = _rglru_mixer(u_all, n_ctx, lru_conv_w[l], lru_conv_b[l], lru_wa[l], lru_ba[l],
                             lru_wx[l], lru_bx[l], lru_lambda[l])
        y_att_l, y_att_c = _diff_attention_mixer(u_all[..., HY_IN + LRU_IN:], n_ctx, att_lambda[l],
                                                 att_subln[l], lam_init, need_ctx)
        if need_ctx:
            y_hy = jnp.concatenate([_hyena_mixer(u_all[:, :n_ctx, :HY_IN], *hy_args), y_hy_l], axis=1)
            mixed = jnp.concatenate([y_hy.astype(BF16), y_lru, jnp.concatenate([y_att_c, y_att_l], axis=1)], axis=-1)
            x_res, seg, seg_c = x_all, seg_all, _segments(bsz, both, COMBINE_ROW_TILE)
        else:
            mixed = jnp.concatenate([y_hy_l.astype(BF16), y_lru[:, n_ctx:], y_att_l], axis=-1)
            x_res = x_all.reshape(bsz, n_all, d)[:, n_ctx:].reshape(bsz * n_lat, d)
            seg = _segments(bsz, ((n_lat, 'lat'),), ROW_TILE)
            seg_c = _segments(bsz, ((n_lat, 'lat'),), COMBINE_ROW_TILE)
        n_tok = x_res.shape[0]
        ln_gb1 = jnp.stack([ln_g[l, 0], ln_b[l, 0]])
        ln_gb2 = jnp.stack([ln_g[l, 1], ln_b[l, 1]])
        x1, h2, h2b, top_idx, top_w = _out_proj_norm_route(
            mixed.reshape(n_tok, d), x_res, jnp.stack([g1, sh2, sc2], axis=1), seg,
            w_out[l].astype(BF16), ln_gb1, router_w[l], router_b[l])

        idx, row_w, block_e, n_used, next_e = _route(top_idx, top_w)
        y = _routed_experts(h2, idx, row_w, block_e, n_used, next_e, exp_w_gate, exp_w_up, exp_w_down, l)
        shared = _shared_expert(h2b, sh_w_gate[l], sh_w_up[l], sh_w_down[l])
        x_out = _combine_norm(y, shared, x1, g2[:, None, :], seg_c, ln_gb2)
        x_all = x_out
    return x_out.reshape(bsz, n_lat, d)
```

```python
import math
from functools import lru_cache, partial

import numpy as np
import jax
import jax.numpy as jnp
from jax import lax
from jax.experimental import pallas as pl
from jax.experimental.pallas import tpu as pltpu

D_MODEL = 2048
DEPTH = 2
GRID_W = 64
F32 = jnp.float32
BF16 = jnp.bfloat16
LANES = 128
SUBLANES = 8

HY_WIDTH = D_MODEL // 4
LRU_WIDTH = D_MODEL // 4
ATT_WIDTH = D_MODEL // 2
HY_ORDER = 2
HY_CONV = 3
HY_BANDS = 8
HY_TARGET = 1e-2
HY_MAX_DECAY = math.log(HY_TARGET) / 0.3
HY_MIN_DECAY = math.log(HY_TARGET) / 1.5
HY_IN = (HY_ORDER + 1) * HY_WIDTH

LRU_HEADS = 8
LRU_HEAD_DIM = LRU_WIDTH // LRU_HEADS
LRU_CONV = 4
LRU_C = 8.0
LRU_IN = 2 * LRU_WIDTH

ATT_HEADS = 4
ATT_V_DIM = ATT_WIDTH // ATT_HEADS
ATT_HEAD_DIM = ATT_V_DIM // 2
ATT_QK = ATT_HEADS * 2 * ATT_HEAD_DIM
ROPE_THETA = 10000.0
ROPE_AXIS_DIM = ATT_HEAD_DIM // 2

N_EXPERTS = 64
TOP_K = 8
ROUTED_SCALE = 2.5

LN_EPS = 1e-5
DEEPNORM_ALPHA = (2 * DEPTH) ** 0.25

VMEM_LIMIT_BYTES = 48 * 1024 * 1024


def _depthwise_conv(x, w, b, pad_left, pad_right):
    y = lax.conv_general_dilated(
        x, w[:, None, :], window_strides=(1,),
        padding=[(pad_left, pad_right)], dimension_numbers=('NWC', 'WIO', 'NWC'),
        feature_group_count=x.shape[-1])
    return y + b


def _hyena_filter(n, w1, b1, w2, b2, w3, freq):
    pos = jnp.arange(n, dtype=F32)
    t = (pos / max(n - 1, 1))[:, None]
    bands = jnp.linspace(1e-4, HY_BANDS - 1, HY_BANDS, dtype=F32)
    ang = (2.0 * math.pi / n) * pos[:, None] * bands
    feats = jnp.concatenate([t, jnp.cos(ang), -jnp.sin(ang)], axis=-1)
    hp = lax.Precision.HIGHEST
    decay = jnp.abs(jnp.linspace(HY_MIN_DECAY, HY_MAX_DECAY, HY_WIDTH, dtype=F32))
    side_cols = HY_ORDER * HY_WIDTH

    def one_side(feats, t, w3_side):
        hid = jnp.sin(freq * (jnp.dot(feats, w1, precision=hp) + b1))
        hid = jnp.sin(freq * (jnp.dot(hid, w2, precision=hp) + b2))
        filt = jnp.dot(hid, w3_side, precision=hp).reshape(n, HY_ORDER, HY_WIDTH)
        return filt * jnp.exp(-t * decay)[:, None, :]

    fwd = one_side(feats, t, w3[:, :side_cols])
    bwd_flipped = one_side(feats[::-1], t[::-1], w3[:, side_cols:])
    two_sided = jnp.concatenate([fwd, jnp.zeros_like(fwd[:1]), bwd_flipped[:n - 1]], axis=0)
    return two_sided / jnp.sum(jnp.abs(two_sided), axis=0, keepdims=True)


DFT_BLOCK_BYTES = 4 * 1024 * 1024
DFT_MINOR_LEN = 128
DFT_SINGLE_LEVEL_MAX = 1024


def _bmm3_body(lh_ref, ll_ref, r_ref, o_ref, *, nb, shared):
    for b in range(nb):
        r = r_ref[b]
        r_hi = r.astype(BF16)
        r_lo = (r - r_hi.astype(F32)).astype(BF16)
        lh = lh_ref[0 if shared else b]
        ll = ll_ref[0 if shared else b]
        o_ref[b] = (jnp.dot(lh, r_hi, preferred_element_type=F32)
                    + jnp.dot(lh, r_lo, preferred_element_type=F32)
                    + jnp.dot(ll, r_hi, preferred_element_type=F32))


def _bmm3(l_hi, l_lo, r, nb):
    batch, k, c = r.shape
    m = l_hi.shape[1]
    shared = l_hi.shape[0] == 1
    nb = max(1, min(nb, batch, DFT_BLOCK_BYTES // (4 * c * max(k, m))))
    assert batch % nb == 0
    l_spec = pl.BlockSpec((1 if shared else nb, m, k), (lambda i: (0, 0, 0)) if shared else (lambda i: (i, 0, 0)))
    return pl.pallas_call(
        partial(_bmm3_body, nb=nb, shared=shared),
        grid=(batch // nb,),
        in_specs=[l_spec, l_spec, pl.BlockSpec((nb, k, c), lambda i: (i, 0, 0))],
        out_specs=pl.BlockSpec((nb, m, c), lambda i: (i, 0, 0)),
        out_shape=jax.ShapeDtypeStruct((batch, m, c), F32),
        compiler_params=pltpu.CompilerParams(dimension_semantics=("parallel",), vmem_limit_bytes=VMEM_LIMIT_BYTES),
        name="dft_matmul",
    )(l_hi, l_lo, r)


def _stack_complex(f):
    return np.concatenate([np.concatenate([f.real, -f.imag], axis=-1),
                           np.concatenate([f.imag, f.real], axis=-1)], axis=-2)


def _split_bf16(a):
    a = jnp.asarray(a, F32)
    hi = a.astype(BF16)
    return hi, (a - hi.astype(F32)).astype(BF16)


@lru_cache(maxsize=None)
def _dft_tables(n1_io, n1, n2):
    n = n1 * n2
    k1 = np.arange(n1)
    a = np.arange(n1_io)
    b = np.arange(n2)
    ang = -2.0 * np.pi * (((np.outer(k1, a) * n2)[None] + b[:, None, None] * k1[None, :, None]) % n) / n
    fa = np.exp(1j * ang)
    fb = np.exp(-2j * np.pi * (np.outer(b, b) % n2) / n2)
    return dict(
        fwd_a=_stack_complex(fa),
        fwd_a_real=np.concatenate([fa.real, fa.imag], axis=-2),
        fwd_b=_stack_complex(fb)[None],
        inv_b=_stack_complex(np.conj(fb).T)[None],
        inv_a=_stack_complex(np.conj(fa).transpose(0, 2, 1) / n),
    )


def _dft_split(length):
    if length <= DFT_SINGLE_LEVEL_MAX:
        return 1, length
    return length // DFT_MINOR_LEN, DFT_MINOR_LEN


def _swap_stage(x, n1, n2):
    c = x.shape[-1]
    return x.reshape(n2, 2, n1, c).transpose(2, 1, 0, 3).reshape(n1, 2 * n2, c)


def _dft_forward(rows, table, n1, n2, tabs):
    if n1 > 1:
        rows = _bmm3(*_split_bf16(tabs[table]), rows, 16)
    return _bmm3(*_split_bf16(tabs['fwd_b']), _swap_stage(rows, n1, n2), 8)


def _filter_spectrum(two_sided):
    length, c = two_sided.shape
    n1, n2 = _dft_split(length)
    tabs = _dft_tables(n1, n1, n2)
    rows = two_sided.reshape(n1, n2, c).transpose(1, 0, 2)
    if n1 == 1:
        rows = jnp.concatenate([rows, jnp.zeros_like(rows)], axis=1)
    return _dft_forward(rows, 'fwd_a_real', n1, n2, tabs)


def _long_conv_pair(z, spec):
    pair, n, c = z.shape
    assert pair == 2
    n1, n2 = _dft_split(2 * n)
    n1_io = max(n1 // 2, 1)
    tabs = _dft_tables(n1_io, n1, n2)
    if n1 > 1:
        rows = z.reshape(2, n1_io, n2, c).transpose(2, 0, 1, 3).reshape(n2, 2 * n1_io, c)
    else:
        rows = jnp.pad(z, ((0, 0), (0, n), (0, 0))).transpose(1, 0, 2)
    y = _dft_forward(rows, 'fwd_a', n1, n2, tabs)
    yr, yi = y[:, :n2], y[:, n2:]
    hr, hi = spec[:, :n2], spec[:, n2:]
    y = jnp.concatenate([yr * hr - yi * hi, yr * hi + yi * hr], axis=1)
    d = _bmm3(*_split_bf16(tabs['inv_b']), y, 8)
    d = _swap_stage(d, n2, n1)
    if n1 > 1:
        out = _bmm3(*_split_bf16(tabs['inv_a']), d, 16)
        return out.reshape(n2, 2, n1_io, c).transpose(1, 2, 0, 3).reshape(2, n, c)
    return (d.reshape(n2, 2, c).transpose(1, 0, 2) / (n1 * n2))[:, :n]


def _hyena_mixer(u, conv_w, conv_b, w1, b1, w2, b2, w3, freq, skip):
    n = u.shape[1]
    u = _depthwise_conv(u, conv_w, conv_b, HY_CONV // 2, HY_CONV - 1 - HY_CONV // 2)
    gates = (u[..., :HY_WIDTH], u[..., HY_WIDTH:2 * HY_WIDTH])
    z = u[..., 2 * HY_WIDTH:]
    two_sided = _hyena_filter(n, w1, b1, w2, b2, w3, freq)
    spec = _filter_spectrum(two_sided.reshape(2 * n, HY_ORDER * HY_WIDTH))
    for o in range(HY_ORDER):
        zc = _long_conv_pair(z, spec[..., o * HY_WIDTH:(o + 1) * HY_WIDTH])
        z = gates[o] * (zc + skip[o] * z)
    return z


LRU_CONV_MARGIN = SUBLANES
LRU_TIME_CHUNK = 1024


def _gelu_tanh(x):
    return 0.5 * x * (1.0 + jnp.tanh(math.sqrt(2.0 / math.pi) * (x + 0.044715 * (x * x * x))))


def _rglru_body(ug_ref, ur_ref, cw_ref, cb_ref, wa_ref, ba_ref, wx_ref, bx_ref, cl_ref, o_ref,
                pad_s, af_s, bf_s, ar_s, br_s, hf_s, hr_s, *, n_ctx, n_lat):
    a_s = (af_s, ar_s)
    b_s = (bf_s, br_s)
    mg = LRU_CONV_MARGIN
    cw = cw_ref[...]
    zeros_margin = jnp.zeros((mg, LANES), F32)
    for r0, n in ((0, n_ctx), (n_ctx, n_lat)):
        pad_s[0:mg, :] = zeros_margin
        pad_s[mg:mg + n, :] = ur_ref[0, r0:r0 + n, :]
        pad_s[mg + n:2 * mg + n, :] = zeros_margin
        chunk = min(n, LRU_TIME_CHUNK)
        for c0 in range(0, n, chunk):
            xr = cb_ref[...] + cw[0:1] * pad_s[mg + c0 - 2:mg + c0 - 2 + chunk, :]
            for j in range(1, LRU_CONV):
                xr = xr + cw[j:j + 1] * pad_s[mg + c0 + j - 2:mg + c0 + j - 2 + chunk, :]
            xb = xr.astype(BF16)
            for d in range(2):
                r = jax.nn.sigmoid(jnp.dot(xb, wa_ref[d, 0], preferred_element_type=F32) + ba_ref[d:d + 1, :])
                i = jax.nn.sigmoid(jnp.dot(xb, wx_ref[d, 0], preferred_element_type=F32) + bx_ref[d:d + 1, :])
                log_a = cl_ref[d:d + 1, :] * r
                a_s[d][r0 + c0:r0 + c0 + chunk, :] = jnp.exp(log_a)
                b_s[d][r0 + c0:r0 + c0 + chunk, :] = jnp.sqrt(1.0 - jnp.exp(2.0 * log_a)) * (i * xr)

    def scan_segment(r0, n, carry):
        def step(t, hc):
            hf, hr = hc
            tf = r0 + t
            tr = r0 + n - 1 - t
            hf = af_s[pl.ds(tf, 1), :] * hf + bf_s[pl.ds(tf, 1), :]
            hr = ar_s[pl.ds(tr, 1), :] * hr + br_s[pl.ds(tr, 1), :]
            hf_s[pl.ds(tf, 1), :] = hf
            hr_s[pl.ds(tr, 1), :] = hr
            return hf, hr
        return lax.fori_loop(0, n, step, carry, unroll=8)

    h0 = jnp.zeros((1, LANES), F32)
    carry = scan_segment(0, n_ctx, (h0, h0))
    scan_segment(n_ctx, n_lat, carry)
    o_ref[0] = ((hf_s[...] + hr_s[...]) * _gelu_tanh(ug_ref[0])).astype(o_ref.dtype)


def _block_diag_tiles(w):
    w = w.reshape(2, LRU_HEADS // 2, 2, LRU_HEAD_DIM, LRU_HEAD_DIM)
    z = jnp.zeros_like(w[:, :, 0])
    top = jnp.concatenate([w[:, :, 0], z], axis=-1)
    bot = jnp.concatenate([z, w[:, :, 1]], axis=-1)
    return jnp.concatenate([top, bot], axis=-2).astype(BF16)


def _rglru_mixer(u_all, n_ctx, conv_w, conv_b, wa, ba, wx, bx, lam):
    bsz, n_all, _ = u_all.shape
    n_lat = n_all - n_ctx
    n_tiles = LRU_WIDTH // LANES
    gate_blk0 = HY_IN // LANES
    rec_blk0 = (HY_IN + LRU_WIDTH) // LANES
    decay_scale = -LRU_C * jax.nn.softplus(-lam)
    seq = lambda col0: pl.BlockSpec((1, n_all, LANES), lambda b, j: (b, 0, col0 + j))
    vec = lambda rows: pl.BlockSpec((rows, LANES), lambda b, j: (0, j))
    mat = pl.BlockSpec((2, 1, LANES, LANES), lambda b, j: (0, j, 0, 0))
    full = pltpu.VMEM((n_all, LANES), F32)
    return pl.pallas_call(
        partial(_rglru_body, n_ctx=n_ctx, n_lat=n_lat),
        grid=(bsz, n_tiles),
        in_specs=[seq(gate_blk0), seq(rec_blk0), vec(LRU_CONV), vec(1), mat, vec(2), mat, vec(2), vec(2)],
        out_specs=pl.BlockSpec((1, n_all, LANES), lambda b, j: (b, 0, j)),
        out_shape=jax.ShapeDtypeStruct((bsz, n_all, LRU_WIDTH), BF16),
        scratch_shapes=[pltpu.VMEM((n_lat + 2 * LRU_CONV_MARGIN, LANES), F32),
                        full, full, full, full, full, full],
        compiler_params=pltpu.CompilerParams(
            dimension_semantics=("parallel", "parallel"),
            vmem_limit_bytes=VMEM_LIMIT_BYTES),
        name="rglru",
    )(u_all, u_all, conv_w, conv_b.reshape(1, LRU_WIDTH), _block_diag_tiles(wa), ba,
      _block_diag_tiles(wx), bx, decay_scale)


ATT_Q_TILE = 256
ATT_KV_CHUNK = 2048


def _attn_body(q_ref, k_ref, v_ref, lv_ref, g_ref, o_ref, *, lam_init):
    lv = lv_ref[...]
    lam = (jnp.exp(jnp.sum(lv[0:1] * lv[1:2], keepdims=True))
           - jnp.exp(jnp.sum(lv[2:3] * lv[3:4], keepdims=True)) + lam_init)
    q = q_ref[0]
    nk = k_ref.shape[1]
    work = [(m, c0, min(c0 + ATT_KV_CHUNK, nk)) for c0 in range(0, nk, ATT_KV_CHUNK) for m in range(2)]

    def scores(m, c0, c1):
        qm = q[:, m * ATT_HEAD_DIM:(m + 1) * ATT_HEAD_DIM]
        km = k_ref[0, c0:c1, m * ATT_HEAD_DIM:(m + 1) * ATT_HEAD_DIM]
        return lax.dot_general(qm, km, (((1,), (1,)), ((), ())), preferred_element_type=F32)

    state = [None, None]
    s_next = scores(*work[0])
    for j, (m, c0, c1) in enumerate(work):
        s = s_next
        if j + 1 < len(work):
            s_next = scores(*work[j + 1])
        chunk_max = jnp.max(s, axis=-1, keepdims=True)
        v = v_ref[0, c0:c1, :]
        if state[m] is None:
            p = jnp.exp(s - chunk_max)
            state[m] = (chunk_max, jnp.sum(p, axis=-1, keepdims=True),
                        jnp.dot(p.astype(BF16), v, preferred_element_type=F32))
        else:
            mx_old, sum_old, acc_old = state[m]
            mx = jnp.maximum(mx_old, chunk_max)
            rescale = jnp.exp(mx_old - mx)
            p = jnp.exp(s - mx)
            state[m] = (mx, rescale * sum_old + jnp.sum(p, axis=-1, keepdims=True),
                        rescale * acc_old + jnp.dot(p.astype(BF16), v, preferred_element_type=F32))
    o = state[0][2] / state[0][1] - lam * (state[1][2] / state[1][1])
    y = o * lax.rsqrt(jnp.mean(o * o, axis=-1, keepdims=True) + LN_EPS) * g_ref[...] * (1.0 - lam_init)
    o_ref[0] = y.astype(o_ref.dtype)


def _diff_attention(qkv, lam_vecs, gain, lam_init, q_row0, n_q, n_k):
    bsz = qkv.shape[0]
    tq = ATT_Q_TILE
    q_blk0 = q_row0 // tq
    n_heads = ATT_HEADS
    return pl.pallas_call(
        partial(_attn_body, lam_init=lam_init),
        grid=(bsz, n_heads, n_q // tq),
        in_specs=[
            pl.BlockSpec((1, tq, ATT_V_DIM), lambda b, h, i: (b, q_blk0 + i, h)),
            pl.BlockSpec((1, n_k, ATT_V_DIM), lambda b, h, i: (b, 0, n_heads + h)),
            pl.BlockSpec((1, n_k, ATT_V_DIM), lambda b, h, i: (b, 0, 2 * n_heads + h)),
            pl.BlockSpec((4, ATT_HEAD_DIM), lambda b, h, i: (0, 0)),
            pl.BlockSpec((1, ATT_V_DIM), lambda b, h, i: (0, 0)),
        ],
        out_specs=pl.BlockSpec((1, tq, ATT_V_DIM), lambda b, h, i: (b, i, h)),
        out_shape=jax.ShapeDtypeStruct((bsz, n_q, ATT_WIDTH), BF16),
        compiler_params=pltpu.CompilerParams(
            dimension_semantics=("parallel", "parallel", "arbitrary"),
            vmem_limit_bytes=VMEM_LIMIT_BYTES),
        name="diff_attention",
    )(qkv, qkv, qkv, lam_vecs, gain.reshape(1, ATT_V_DIM))


def _axial_rope_tables(n):
    rows = n // GRID_W
    row = jnp.broadcast_to(jnp.arange(rows, dtype=F32)[:, None], (rows, GRID_W)).reshape(-1)
    col = jnp.broadcast_to(jnp.arange(GRID_W, dtype=F32)[None, :], (rows, GRID_W)).reshape(-1)
    inv = ROPE_THETA ** (-jnp.arange(0, ROPE_AXIS_DIM, 2, dtype=F32) / ROPE_AXIS_DIM)
    ang = jnp.stack([row[:, None] * inv, col[:, None] * inv], axis=1)
    return jnp.cos(ang), jnp.sin(ang)


def _apply_axial_rope(x, cos, sin):
    xs = x.reshape(x.shape[:-1] + (2, 2, ROPE_AXIS_DIM // 2))
    x1, x2 = xs[..., 0, :], xs[..., 1, :]
    cb, sb = cos[None, :, None, None], sin[None, :, None, None]
    out = jnp.stack([x1 * cb - x2 * sb, x2 * cb + x1 * sb], axis=-2)
    return out.reshape(x.shape)


def _diff_attention_mixer(u_att, n_ctx, lam_vecs, subln_gain, lam_init, need_ctx):
    bsz, n_all, _ = u_att.shape
    n_lat = n_all - n_ctx
    scale = ATT_HEAD_DIM ** -0.5
    u_lat = u_att[:, n_ctx:]
    u_ctx = u_att[:, :n_ctx]
    q_l = u_lat[..., :ATT_QK].reshape(bsz, n_lat, ATT_HEADS, 2, ATT_HEAD_DIM)
    k_l = u_lat[..., ATT_QK:2 * ATT_QK].reshape(bsz, n_lat, ATT_HEADS, 2, ATT_HEAD_DIM)
    cos, sin = _axial_rope_tables(n_lat)
    q_l = _apply_axial_rope(q_l, cos, sin) * scale
    k_l = _apply_axial_rope(k_l, cos, sin)
    lat = jnp.concatenate([q_l.reshape(bsz, n_lat, ATT_QK), k_l.reshape(bsz, n_lat, ATT_QK),
                           u_lat[..., 2 * ATT_QK:]], axis=-1).astype(BF16)
    cx = jnp.concatenate([u_ctx[..., :ATT_QK] * scale, u_ctx[..., ATT_QK:]], axis=-1).astype(BF16)
    qkv = jnp.concatenate([cx, lat], axis=1)
    y_lat = _diff_attention(qkv, lam_vecs, subln_gain, lam_init, n_ctx, n_lat, n_all)
    y_ctx = _diff_attention(qkv, lam_vecs, subln_gain, lam_init, 0, n_ctx, n_ctx) if need_ctx else None
    return y_lat, y_ctx


ROW_TILE = 256
IN_PROJ_COL_TILE = 1408
COMBINE_ROW_TILE = 128


def _ln_rows(x):
    mu = jnp.mean(x, axis=-1, keepdims=True)
    xc = x - mu
    return xc * lax.rsqrt(jnp.mean(xc * xc, axis=-1, keepdims=True) + LN_EPS)


def _segments(bsz, seg_rows, tile):
    ids = []
    for b in range(bsz):
        for rows, kind in seg_rows:
            assert rows % tile == 0
            ids += [bsz if kind == 'ctx' else b] * (rows // tile)
    return jnp.asarray(ids, jnp.int32)


def _ln_mm_body(seg_ref, x_ref, mod_ref, w_ref, o_ref):
    m = mod_ref[0]
    h = _ln_rows(x_ref[...]) * (1.0 + m[1:2]) + m[0:1]
    o_ref[...] = jnp.dot(h.astype(BF16), w_ref[...], preferred_element_type=F32)


def _ln_mod_matmul(x, mod, seg, w):
    t, d = x.shape
    n = w.shape[1]
    tm, tn = ROW_TILE, IN_PROJ_COL_TILE
    assert t % tm == 0 and n % tn == 0
    grid_spec = pltpu.PrefetchScalarGridSpec(
        num_scalar_prefetch=1,
        grid=(n // tn, t // tm),
        in_specs=[pl.BlockSpec((tm, d), lambda j, i, s: (i, 0)),
                  pl.BlockSpec((1, 2, d), lambda j, i, s: (s[i], 0, 0)),
                  pl.BlockSpec((d, tn), lambda j, i, s: (0, j))],
        out_specs=pl.BlockSpec((tm, tn), lambda j, i, s: (i, j)),
    )
    return pl.pallas_call(
        _ln_mm_body, grid_spec=grid_spec,
        out_shape=jax.ShapeDtypeStruct((t, n), F32),
        compiler_params=pltpu.CompilerParams(dimension_semantics=("parallel", "arbitrary"),
                                             vmem_limit_bytes=VMEM_LIMIT_BYTES),
        name="ln_mod_in_proj",
    )(seg, x, mod, w)


def _split_dot(a, b_hi, b_lo):
    a_hi = a.astype(BF16)
    a_lo = (a - a_hi.astype(F32)).astype(BF16)
    return (jnp.dot(a_hi, b_hi, preferred_element_type=F32) + jnp.dot(a_hi, b_lo, preferred_element_type=F32)
            + jnp.dot(a_lo, b_hi, preferred_element_type=F32))


def _out_ln_body(seg_ref, mixed_ref, x_ref, mod_ref, w_ref, lng_ref, rwh_ref, rwl_ref, rb_ref,
                 x1_ref, h2_ref, h2b_ref, ti_ref, tw_ref):
    m = mod_ref[0]
    mix = jnp.dot(mixed_ref[...], w_ref[...], preferred_element_type=F32)
    x1 = _ln_rows(DEEPNORM_ALPHA * x_ref[...] + m[0:1] * mix) * lng_ref[0:1] + lng_ref[1:2]
    x1_ref[...] = x1
    h2 = _ln_rows(x1) * (1.0 + m[2:3]) + m[1:2]
    h2_ref[...] = h2
    h2b_ref[...] = h2.astype(BF16)
    scores = jax.nn.sigmoid(_split_dot(h2, rwh_ref[...], rwl_ref[...]))
    lane = lax.broadcasted_iota(jnp.int32, scores.shape, 1).astype(F32)
    biased = jnp.where(lane < N_EXPERTS, scores + rb_ref[...], -jnp.inf)
    top_i = jnp.zeros(scores.shape, F32)
    top_s = jnp.zeros(scores.shape, F32)
    for k in range(TOP_K):
        best = jnp.max(biased, axis=-1, keepdims=True)
        pick = jnp.min(jnp.where(biased == best, lane, float(LANES)), axis=-1, keepdims=True)
        hit = lane == pick
        top_i = jnp.where(lane == k, pick, top_i)
        top_s = jnp.where(lane == k, jnp.sum(jnp.where(hit, scores, 0.0), axis=-1, keepdims=True), top_s)
        biased = jnp.where(hit, -jnp.inf, biased)
    ti_ref[...] = top_i.astype(jnp.int32)
    tw_ref[...] = top_s / jnp.sum(top_s, axis=-1, keepdims=True) * ROUTED_SCALE


def _out_proj_norm_route(mixed, x, mod, seg, w, ln_gb, router_w, router_b):
    t, d = x.shape
    tm = ROW_TILE
    rw = jnp.pad(router_w, ((0, 0), (0, LANES - N_EXPERTS)))
    rw_hi, rw_lo = _split_bf16(rw)
    rb = jnp.pad(router_b, (0, LANES - N_EXPERTS)).reshape(1, LANES)
    row = lambda width: pl.BlockSpec((tm, width), lambda i, s: (i, 0))
    const = lambda shape: pl.BlockSpec(shape, lambda i, s: (0,) * len(shape))
    grid_spec = pltpu.PrefetchScalarGridSpec(
        num_scalar_prefetch=1,
        grid=(t // tm,),
        in_specs=[row(d), row(d), pl.BlockSpec((1, 3, d), lambda i, s: (s[i], 0, 0)), const((d, d)),
                  const((2, d)), const((d, LANES)), const((d, LANES)), const((1, LANES))],
        out_specs=[row(d), row(d), row(d), row(LANES), row(LANES)],
    )
    x1, h2, h2b, top_i, top_w = pl.pallas_call(
        _out_ln_body, grid_spec=grid_spec,
        out_shape=[jax.ShapeDtypeStruct((t, d), F32), jax.ShapeDtypeStruct((t, d), F32),
                   jax.ShapeDtypeStruct((t, d), BF16), jax.ShapeDtypeStruct((t, LANES), jnp.int32),
                   jax.ShapeDtypeStruct((t, LANES), F32)],
        compiler_params=pltpu.CompilerParams(dimension_semantics=("parallel",),
                                             vmem_limit_bytes=VMEM_LIMIT_BYTES),
        name="out_proj_norm_route",
    )(seg, mixed, x, mod, w, ln_gb, rw_hi, rw_lo, rb)
    return x1, h2, h2b, top_i[:, :TOP_K], top_w[:, :TOP_K]


def _shared_body(h_ref, wgu_ref, wd_ref, o_ref):
    gu = jnp.dot(h_ref[...], wgu_ref[...], preferred_element_type=F32)
    f = wd_ref.shape[0]
    g, u = gu[:, :f], gu[:, f:]
    act = (g * jax.nn.sigmoid(g) * u).astype(BF16)
    o_ref[...] = jnp.dot(act, wd_ref[...], preferred_element_type=F32)


def _shared_expert(hb, w_gate, w_up, w_down):
    t, d = hb.shape
    f = w_gate.shape[1]
    tm = ROW_TILE
    wgu = jnp.concatenate([w_gate, w_up], axis=1).astype(BF16)
    return pl.pallas_call(
        _shared_body, grid=(t // tm,),
        in_specs=[pl.BlockSpec((tm, d), lambda i: (i, 0)), pl.BlockSpec((d, 2 * f), lambda i: (0, 0)),
                  pl.BlockSpec((f, d), lambda i: (0, 0))],
        out_specs=pl.BlockSpec((tm, d), lambda i: (i, 0)),
        out_shape=jax.ShapeDtypeStruct((t, d), F32),
        compiler_params=pltpu.CompilerParams(dimension_semantics=("parallel",),
                                             vmem_limit_bytes=VMEM_LIMIT_BYTES),
        name="shared_expert",
    )(hb, wgu, w_down.astype(BF16))


def _combine_body(seg_ref, *refs):
    y_refs, (sh_ref, x_ref, mod_ref, lng_ref, o_ref) = refs[:TOP_K], refs[TOP_K:]
    ffn = sh_ref[...]
    for y_ref in y_refs:
        ffn = ffn + y_ref[...]
    o_ref[...] = _ln_rows(DEEPNORM_ALPHA * x_ref[...] + mod_ref[0] * ffn) * lng_ref[0:1] + lng_ref[1:2]


def _combine_norm(y, shared, x1, gate, seg, ln_gb):
    t, d = x1.shape
    tm = COMBINE_ROW_TILE
    tiles = t // tm
    row = pl.BlockSpec((tm, d), lambda i, s: (i, 0))
    y_specs = [pl.BlockSpec((tm, d), lambda i, s, k=k: (k * tiles + i, 0)) for k in range(TOP_K)]
    grid_spec = pltpu.PrefetchScalarGridSpec(
        num_scalar_prefetch=1,
        grid=(tiles,),
        in_specs=y_specs + [row, row, pl.BlockSpec((1, 1, d), lambda i, s: (s[i], 0, 0)),
                            pl.BlockSpec((2, d), lambda i, s: (0, 0))],
        out_specs=row,
    )
    return pl.pallas_call(
        _combine_body, grid_spec=grid_spec,
        out_shape=jax.ShapeDtypeStruct((t, d), F32),
        compiler_params=pltpu.CompilerParams(dimension_semantics=("parallel",),
                                             vmem_limit_bytes=VMEM_LIMIT_BYTES),
        name="combine_norm",
    )(seg, *([y] * TOP_K), shared, x1, gate, ln_gb)


MOE_BLOCK_ROWS = 256
MOE_VMEM_LIMIT_BYTES = 56 * 1024 * 1024
MOE_OUTBOUND_DMA_PRIORITY = 1
MOE_WEIGHT_DMA_PRIORITY = 1


def _moe_body(be_ref, nu_ref, ne_ref, idx_ref, idx_nxt_ref, rw_ref, wg_hbm, wu_hbm, wd_hbm, h_hbm, o_hbm,
              xbuf, ybuf, wg_f, wu_f, wd_f, wg_s, wu_s, wd_s, gsem, ssem, wsem, *, bm, layer):
    i = pl.program_id(0)
    n_used = nu_ref[0]
    slot = i % 2

    def weight_copies(expert):
        return (pltpu.make_async_copy(wg_hbm.at[layer, expert], wg_f, wsem.at[0]),
                pltpu.make_async_copy(wu_hbm.at[layer, expert], wu_f, wsem.at[1]),
                pltpu.make_async_copy(wd_hbm.at[layer, expert], wd_f, wsem.at[2]))

    def gather_rows(tok_ref, dst_slot):
        for r in range(bm):
            pltpu.make_async_copy(h_hbm.at[pl.ds(tok_ref[0, 0, r], 1)], xbuf.at[dst_slot, pl.ds(r, 1)],
                                  gsem.at[dst_slot]).start()

    def wait_gather(s):
        pltpu.make_async_copy(h_hbm.at[pl.ds(0, bm)], xbuf.at[s], gsem.at[s]).wait()

    def wait_scatter(s):
        pltpu.make_async_copy(ybuf.at[s], o_hbm.at[pl.ds(0, bm)], ssem.at[s]).wait()

    e = be_ref[i]
    e_prev = be_ref[jnp.maximum(i - 1, 0)]

    @pl.when(i == 0)
    def _():
        gather_rows(idx_ref, 0)
        for copy in weight_copies(e):
            copy.start(priority=MOE_WEIGHT_DMA_PRIORITY)

    @pl.when((i < n_used) & ((i == 0) | (e != e_prev)))
    def _():
        for copy in weight_copies(e):
            copy.wait()
        wg_s[...] = wg_f[...].astype(BF16)
        wu_s[...] = wu_f[...].astype(BF16)
        wd_s[...] = wd_f[...].astype(BF16)

        @pl.when(ne_ref[i] >= 0)
        def _():
            for copy in weight_copies(ne_ref[i]):
                copy.start(priority=MOE_WEIGHT_DMA_PRIORITY)

    def block_step(s):
        wait_gather(s)

        @pl.when(i >= 2)
        def _():
            wait_scatter(s)

        gather_rows(idx_nxt_ref, 1 - s)

        xb = xbuf[s].astype(BF16)
        g = jnp.dot(xb, wg_s[...], preferred_element_type=F32)
        u = jnp.dot(xb, wu_s[...], preferred_element_type=F32)
        act = (g * jax.nn.sigmoid(g) * u).astype(BF16)
        ybuf[s] = jnp.dot(act, wd_s[...], preferred_element_type=F32) * rw_ref[...]
        for r in range(bm):
            pltpu.make_async_copy(ybuf.at[s, pl.ds(r, 1)], o_hbm.at[pl.ds(idx_ref[0, 0, bm + r], 1)],
                                  ssem.at[s]).start(priority=MOE_OUTBOUND_DMA_PRIORITY)

        @pl.when(i == n_used - 1)
        def _():
            @pl.when(i >= 1)
            def _():
                wait_scatter(1 - s)
            wait_scatter(s)
            wait_gather(1 - s)

    for s in range(2):
        pl.when((i < n_used) & (slot == s))(partial(block_step, s))

    @pl.when(i >= n_used)
    def _():
        xbuf[0] = jnp.zeros(xbuf.shape[1:], F32)
        fill = pltpu.make_async_copy(xbuf.at[0], o_hbm.at[pl.ds(pl.multiple_of(i * bm, bm), bm)], gsem.at[0])
        fill.start()
        fill.wait()


def _routed_experts(h, idx, row_w, block_e, n_used, next_e, w_gate, w_up, w_down, layer):
    bm = MOE_BLOCK_ROWS
    n_blocks = idx.shape[0]
    d = h.shape[1]
    f = w_gate.shape[-1]
    anywhere = pl.BlockSpec(memory_space=pl.ANY)
    grid_spec = pltpu.PrefetchScalarGridSpec(
        num_scalar_prefetch=3,
        grid=(n_blocks,),
        in_specs=[
            pl.BlockSpec((1, 1, 2 * bm), lambda i, be, nu, ne: (i, 0, 0), memory_space=pltpu.SMEM),
            pl.BlockSpec((1, 1, 2 * bm), lambda i, be, nu, ne: (jnp.minimum(i + 1, n_blocks - 1), 0, 0),
                         memory_space=pltpu.SMEM),
            pl.BlockSpec((bm, 1), lambda i, be, nu, ne: (i, 0)),
            anywhere, anywhere, anywhere, anywhere,
        ],
        out_specs=anywhere,
        scratch_shapes=[pltpu.VMEM((2, bm, d), F32), pltpu.VMEM((2, bm, d), F32),
                        pltpu.VMEM((d, f), F32), pltpu.VMEM((d, f), F32), pltpu.VMEM((f, d), F32),
                        pltpu.VMEM((d, f), BF16), pltpu.VMEM((d, f), BF16), pltpu.VMEM((f, d), BF16),
                        pltpu.SemaphoreType.DMA((2,)), pltpu.SemaphoreType.DMA((2,)),
                        pltpu.SemaphoreType.DMA((3,))],
    )
    return pl.pallas_call(
        partial(_moe_body, bm=bm, layer=layer),
        grid_spec=grid_spec,
        out_shape=jax.ShapeDtypeStruct((n_blocks * bm, d), F32),
        compiler_params=pltpu.CompilerParams(
            dimension_semantics=("arbitrary",),
            vmem_limit_bytes=MOE_VMEM_LIMIT_BYTES),
        name="routed_experts",
    )(block_e, n_used, next_e, idx, idx, row_w, w_gate, w_up, w_down, h)


def _route(top_idx, top_w):
    bm = MOE_BLOCK_ROWS
    n_tok = top_idx.shape[0]
    n_assign = n_tok * TOP_K
    flat_e = top_idx.reshape(-1)
    flat_w = top_w.reshape(-1)
    order = jnp.argsort(flat_e).astype(jnp.int32)
    counts = jnp.sum(flat_e[None, :] == jnp.arange(N_EXPERTS)[:, None], axis=1).astype(jnp.int32)
    padded = (counts + bm - 1) // bm * bm
    padded_end = jnp.cumsum(padded)
    live_end = jnp.cumsum(counts)
    shift = padded_end - padded - (live_end - counts)
    n_blocks = -(-n_assign // bm) + N_EXPERTS
    block_row0 = jnp.arange(n_blocks, dtype=jnp.int32) * bm
    block_e = jnp.minimum(jnp.sum(padded_end[None, :] <= block_row0[:, None], axis=1),
                          N_EXPERTS - 1).astype(jnp.int32)
    n_used = (padded_end[-1:] // bm).astype(jnp.int32)
    experts = jnp.arange(N_EXPERTS, dtype=jnp.int32)
    later = jnp.where((experts[None, :] > experts[:, None]) & (counts[None, :] > 0), experts[None, :], N_EXPERTS)
    next_live = jnp.min(later, axis=1)
    next_e = jnp.where(next_live < N_EXPERTS, next_live, -1)[block_e].astype(jnp.int32)
    rows = jnp.arange(n_blocks * bm, dtype=jnp.int32).reshape(n_blocks, bm)
    row_slot = rows - shift[block_e][:, None]
    row_live = row_slot < live_end[block_e][:, None]
    row_assign = order[jnp.where(row_live, row_slot, 0)]
    row_tok = jnp.where(row_live, row_assign // TOP_K, 0)
    row_dst = jnp.where(row_live, (row_assign % TOP_K) * n_tok + row_tok,
                        n_assign + rows - live_end[block_e][:, None])
    row_w = jnp.where(row_live, flat_w[row_assign], 0.0)
    idx = jnp.concatenate([row_tok, row_dst], axis=1).astype(jnp.int32)[:, None, :]
    return idx, row_w.reshape(-1, 1), block_e, n_used, next_e


def kernel(x, c, ctx, c_ctx, ada_w, ada_b, w_in, hy_conv_w, hy_conv_b, hy_w1, hy_b1, hy_w2, hy_b2, hy_w3, hy_freq, hy_skip, lru_conv_w, lru_conv_b, lru_wa, lru_ba, lru_wx, lru_bx, lru_lambda, att_lambda, att_subln, w_out, ln_g, ln_b, router_w, router_b, exp_w_gate, exp_w_up, exp_w_down, sh_w_gate, sh_w_up, sh_w_down):
    bsz, n_lat, d = x.shape
    n_ctx = ctx.shape[1]
    n_all = n_ctx + n_lat
    x_all = jnp.concatenate([ctx, x], axis=1).reshape(bsz * n_all, d)
    both = ((n_ctx, 'ctx'), (n_lat, 'lat'))
    seg_all = _segments(bsz, both, ROW_TILE)
    x_out = None

    for l in range(DEPTH):
        need_ctx = l < DEPTH - 1
        lam_init = 0.8 - 0.6 * math.exp(-0.3 * l)
        cond = jnp.concatenate([c, c_ctx[None, :]], axis=0)
        mod = jnp.dot(jax.nn.silu(cond), ada_w[l], precision=lax.Precision.HIGHEST) + ada_b[l]
        sh1, sc1, g1, sh2, sc2, g2 = jnp.split(mod, 6, axis=-1)
        hy_args = (hy_conv_w[l], hy_conv_b[l], hy_w1[l], hy_b1[l], hy_w2[l], hy_b2[l],
                   hy_w3[l], hy_freq[l], hy_skip[l])

        u_all = _ln_mod_matmul(x_all, jnp.stack([sh1, sc1], axis=1), seg_all,
                               w_in[l].astype(BF16)).reshape(bsz, n_all, -1)
        y_hy_l = _hyena_mixer(u_all[:, n_ctx:, :HY_IN], *hy_args)
        y_lru = _rglru_mixer(u_all, n_ctx, lru_conv_w[l], lru_conv_b[l], lru_wa[l], lru_ba[l],
                             lru_wx[l], lru_bx[l], lru_lambda[l])
        y_att_l, y_att_c = _diff_attention_mixer(u_all[..., HY_IN + LRU_IN:], n_ctx, att_lambda[l],
                                                 att_subln[l], lam_init, need_ctx)
        if need_ctx:
            y_hy = jnp.concatenate([_hyena_mixer(u_all[:, :n_ctx, :HY_IN], *hy_args), y_hy_l], axis=1)
            mixed = jnp.concatenate([y_hy.astype(BF16), y_lru, jnp.concatenate([y_att_c, y_att_l], axis=1)], axis=-1)
            x_res, seg, seg_c = x_all, seg_all, _segments(bsz, both, COMBINE_ROW_TILE)
        else:
            mixed = jnp.concatenate([y_hy_l.astype(BF16), y_lru[:, n_ctx:], y_att_l], axis=-1)
            x_res = x_all.reshape(bsz, n_all, d)[:, n_ctx:].reshape(bsz * n_lat, d)
            seg = _segments(bsz, ((n_lat, 'lat'),), ROW_TILE)
            seg_c = _segments(bsz, ((n_lat, 'lat'),), COMBINE_ROW_TILE)
        n_tok = x_res.shape[0]
        ln_gb1 = jnp.stack([ln_g[l, 0], ln_b[l, 0]])
        ln_gb2 = jnp.stack([ln_g[l, 1], ln_b[l, 1]])
        x1, h2, h2b, top_idx, top_w = _out_proj_norm_route(
            mixed.reshape(n_tok, d), x_res, jnp.stack([g1, sh2, sc2], axis=1), seg,
            w_out[l].astype(BF16), ln_gb1, router_w[l], router_b[l])

        idx, row_w, block_e, n_used, next_e = _route(top_idx, top_w)
        y = _routed_experts(h2, idx, row_w, block_e, n_used, next_e, exp_w_gate, exp_w_up, exp_w_down, l)
        shared = _shared_expert(h2b, sh_w_gate[l], sh_w_up[l], sh_w_down[l])
        x_out = _combine_norm(y, shared, x1, g2[:, None, :], seg_c, ln_gb2)
        x_all = x_out
    return x_out.reshape(bsz, n_lat, d)
```

```python
import math
from functools import lru_cache, partial

import numpy as np
import jax
import jax.numpy as jnp
from jax import lax
from jax.experimental import pallas as pl
from jax.experimental.pallas import tpu as pltpu

D_MODEL = 2048
DEPTH = 2
GRID_W = 64
F32 = jnp.float32
BF16 = jnp.bfloat16
LANES = 128
SUBLANES = 8

HY_WIDTH = D_MODEL // 4
LRU_WIDTH = D_MODEL // 4
ATT_WIDTH = D_MODEL // 2
HY_ORDER = 2
HY_CONV = 3
HY_BANDS = 8
HY_TARGET = 1e-2
HY_MAX_DECAY = math.log(HY_TARGET) / 0.3
HY_MIN_DECAY = math.log(HY_TARGET) / 1.5
HY_IN = (HY_ORDER + 1) * HY_WIDTH

LRU_HEADS = 8
LRU_HEAD_DIM = LRU_WIDTH // LRU_HEADS
LRU_CONV = 4
LRU_C = 8.0
LRU_IN = 2 * LRU_WIDTH

ATT_HEADS = 4
ATT_V_DIM = ATT_WIDTH // ATT_HEADS
ATT_HEAD_DIM = ATT_V_DIM // 2
ATT_QK = ATT_HEADS * 2 * ATT_HEAD_DIM
ROPE_THETA = 10000.0
ROPE_AXIS_DIM = ATT_HEAD_DIM // 2

N_EXPERTS = 64
TOP_K = 8
ROUTED_SCALE = 2.5

LN_EPS = 1e-5
DEEPNORM_ALPHA = (2 * DEPTH) ** 0.25

VMEM_LIMIT_BYTES = 48 * 1024 * 1024


def _depthwise_conv(x, w, b, pad_left, pad_right):
    y = lax.conv_general_dilated(
        x, w[:, None, :], window_strides=(1,),
        padding=[(pad_left, pad_right)], dimension_numbers=('NWC', 'WIO', 'NWC'),
        feature_group_count=x.shape[-1])
    return y + b


def _hyena_filter(n, w1, b1, w2, b2, w3, freq):
    pos = jnp.arange(n, dtype=F32)
    t = (pos / max(n - 1, 1))[:, None]
    bands = jnp.linspace(1e-4, HY_BANDS - 1, HY_BANDS, dtype=F32)
    ang = (2.0 * math.pi / n) * pos[:, None] * bands
    feats = jnp.concatenate([t, jnp.cos(ang), -jnp.sin(ang)], axis=-1)
    hp = lax.Precision.HIGHEST
    decay = jnp.abs(jnp.linspace(HY_MIN_DECAY, HY_MAX_DECAY, HY_WIDTH, dtype=F32))
    side_cols = HY_ORDER * HY_WIDTH

    def one_side(feats, t, w3_side):
        hid = jnp.sin(freq * (jnp.dot(feats, w1, precision=hp) + b1))
        hid = jnp.sin(freq * (jnp.dot(hid, w2, precision=hp) + b2))
        filt = jnp.dot(hid, w3_side, precision=hp).reshape(n, HY_ORDER, HY_WIDTH)
        return filt * jnp.exp(-t * decay)[:, None, :]

    fwd = one_side(feats, t, w3[:, :side_cols])
    bwd_flipped = one_side(feats[::-1], t[::-1], w3[:, side_cols:])
    two_sided = jnp.concatenate([fwd, jnp.zeros_like(fwd[:1]), bwd_flipped[:n - 1]], axis=0)
    return two_sided / jnp.sum(jnp.abs(two_sided), axis=0, keepdims=True)


DFT_BLOCK_BYTES = 4 * 1024 * 1024
DFT_MINOR_LEN = 128
DFT_SINGLE_LEVEL_MAX = 1024


def _bmm3_body(lh_ref, ll_ref, r_ref, o_ref, *, nb, shared):
    for b in range(nb):
        r = r_ref[b]
        r_hi = r.astype(BF16)
        r_lo = (r - r_hi.astype(F32)).astype(BF16)
        lh = lh_ref[0 if shared else b]
        ll = ll_ref[0 if shared else b]
        o_ref[b] = (jnp.dot(lh, r_hi, preferred_element_type=F32)
                    + jnp.dot(lh, r_lo, preferred_element_type=F32)
                    + jnp.dot(ll, r_hi, preferred_element_type=F32))


def _bmm3(l_hi, l_lo, r, nb):
    batch, k, c = r.shape
    m = l_hi.shape[1]
    shared = l_hi.shape[0] == 1
    nb = max(1, min(nb, batch, DFT_BLOCK_BYTES // (4 * c * max(k, m))))
    assert batch % nb == 0
    l_spec = pl.BlockSpec((1 if shared else nb, m, k), (lambda i: (0, 0, 0)) if shared else (lambda i: (i, 0, 0)))
    return pl.pallas_call(
        partial(_bmm3_body, nb=nb, shared=shared),
        grid=(batch // nb,),
        in_specs=[l_spec, l_spec, pl.BlockSpec((nb, k, c), lambda i: (i, 0, 0))],
        out_specs=pl.BlockSpec((nb, m, c), lambda i: (i, 0, 0)),
        out_shape=jax.ShapeDtypeStruct((batch, m, c), F32),
        compiler_params=pltpu.CompilerParams(dimension_semantics=("parallel",), vmem_limit_bytes=VMEM_LIMIT_BYTES),
        name="dft_matmul",
    )(l_hi, l_lo, r)


def _stack_complex(f):
    return np.concatenate([np.concatenate([f.real, -f.imag], axis=-1),
                           np.concatenate([f.imag, f.real], axis=-1)], axis=-2)


def _split_bf16(a):
    a = jnp.asarray(a, F32)
    hi = a.astype(BF16)
    return hi, (a - hi.astype(F32)).astype(BF16)


@lru_cache(maxsize=None)
def _dft_tables(n1_io, n1, n2):
    n = n1 * n2
    k1 = np.arange(n1)
    a = np.arange(n1_io)
    b = np.arange(n2)
    ang = -2.0 * np.pi * (((np.outer(k1, a) * n2)[None] + b[:, None, None] * k1[None, :, None]) % n) / n
    fa = np.exp(1j * ang)
    fb = np.exp(-2j * np.pi * (np.outer(b, b) % n2) / n2)
    return dict(
        fwd_a=_stack_complex(fa),
        fwd_a_real=np.concatenate([fa.real, fa.imag], axis=-2),
        fwd_b=_stack_complex(fb)[None],
        inv_b=_stack_complex(np.conj(fb).T)[None],
        inv_a=_stack_complex(np.conj(fa).transpose(0, 2, 1) / n),
    )


def _dft_split(length):
    if length <= DFT_SINGLE_LEVEL_MAX:
        return 1, length
    return length // DFT_MINOR_LEN, DFT_MINOR_LEN


def _swap_stage(x, n1, n2):
    c = x.shape[-1]
    return x.reshape(n2, 2, n1, c).transpose(2, 1, 0, 3).reshape(n1, 2 * n2, c)


def _dft_forward(rows, table, n1, n2, tabs):
    if n1 > 1:
        rows = _bmm3(*_split_bf16(tabs[table]), rows, 16)
    return _bmm3(*_split_bf16(tabs['fwd_b']), _swap_stage(rows, n1, n2), 8)


def _filter_spectrum(two_sided):
    length, c = two_sided.shape
    n1, n2 = _dft_split(length)
    tabs = _dft_tables(n1, n1, n2)
    rows = two_sided.reshape(n1, n2, c).transpose(1, 0, 2)
    if n1 == 1:
        rows = jnp.concatenate([rows, jnp.zeros_like(rows)], axis=1)
    return _dft_forward(rows, 'fwd_a_real', n1, n2, tabs)


def _long_conv_pair(z, spec):
    pair, n, c = z.shape
    assert pair == 2
    n1, n2 = _dft_split(2 * n)
    n1_io = max(n1 // 2, 1)
    tabs = _dft_tables(n1_io, n1, n2)
    if n1 > 1:
        rows = z.reshape(2, n1_io, n2, c).transpose(2, 0, 1, 3).reshape(n2, 2 * n1_io, c)
    else:
        rows = jnp.pad(z, ((0, 0), (0, n), (0, 0))).transpose(1, 0, 2)
    y = _dft_forward(rows, 'fwd_a', n1, n2, tabs)
    yr, yi = y[:, :n2], y[:, n2:]
    hr, hi = spec[:, :n2], spec[:, n2:]
    y = jnp.concatenate([yr * hr - yi * hi, yr * hi + yi * hr], axis=1)
    d = _bmm3(*_split_bf16(tabs['inv_b']), y, 8)
    d = _swap_stage(d, n2, n1)
    if n1 > 1:
        out = _bmm3(*_split_bf16(tabs['inv_a']), d, 16)
        return out.reshape(n2, 2, n1_io, c).transpose(1, 2, 0, 3).reshape(2, n, c)
    return (d.reshape(n2, 2, c).transpose(1, 0, 2) / (n1 * n2))[:, :n]


def _hyena_mixer(u, conv_w, conv_b, w1, b1, w2, b2, w3, freq, skip):
    n = u.shape[1]
    u = _depthwise_conv(u, conv_w, conv_b, HY_CONV // 2, HY_CONV - 1 - HY_CONV // 2)
    gates = (u[..., :HY_WIDTH], u[..., HY_WIDTH:2 * HY_WIDTH])
    z = u[..., 2 * HY_WIDTH:]
    two_sided = _hyena_filter(n, w1, b1, w2, b2, w3, freq)
    spec = _filter_spectrum(two_sided.reshape(2 * n, HY_ORDER * HY_WIDTH))
    for o in range(HY_ORDER):
        zc = _long_conv_pair(z, spec[..., o * HY_WIDTH:(o + 1) * HY_WIDTH])
        z = gates[o] * (zc + skip[o] * z)
    return z


LRU_CONV_MARGIN = SUBLANES
LRU_TIME_CHUNK = 1024


def _gelu_tanh(x):
    return 0.5 * x * (1.0 + jnp.tanh(math.sqrt(2.0 / math.pi) * (x + 0.044715 * (x * x * x))))


def _rglru_body(ug_ref, ur_ref, cw_ref, cb_ref, wa_ref, ba_ref, wx_ref, bx_ref, cl_ref, o_ref,
                pad_s, af_s, bf_s, ar_s, br_s, hf_s, hr_s, *, n_ctx, n_lat):
    a_s = (af_s, ar_s)
    b_s = (bf_s, br_s)
    mg = LRU_CONV_MARGIN
    cw = cw_ref[...]
    zeros_margin = jnp.zeros((mg, LANES), F32)
    for r0, n in ((0, n_ctx), (n_ctx, n_lat)):
        pad_s[0:mg, :] = zeros_margin
        pad_s[mg:mg + n, :] = ur_ref[0, r0:r0 + n, :]
        pad_s[mg + n:2 * mg + n, :] = zeros_margin
        chunk = min(n, LRU_TIME_CHUNK)
        for c0 in range(0, n, chunk):
            xr = cb_ref[...] + cw[0:1] * pad_s[mg + c0 - 2:mg + c0 - 2 + chunk, :]
            for j in range(1, LRU_CONV):
                xr = xr + cw[j:j + 1] * pad_s[mg + c0 + j - 2:mg + c0 + j - 2 + chunk, :]
            xb = xr.astype(BF16)
            for d in range(2):
                r = jax.nn.sigmoid(jnp.dot(xb, wa_ref[d, 0], preferred_element_type=F32) + ba_ref[d:d + 1, :])
                i = jax.nn.sigmoid(jnp.dot(xb, wx_ref[d, 0], preferred_element_type=F32) + bx_ref[d:d + 1, :])
                log_a = cl_ref[d:d + 1, :] * r
                a_s[d][r0 + c0:r0 + c0 + chunk, :] = jnp.exp(log_a)
                b_s[d][r0 + c0:r0 + c0 + chunk, :] = jnp.sqrt(1.0 - jnp.exp(2.0 * log_a)) * (i * xr)

    def scan_segment(r0, n, carry):
        def step(t, hc):
            hf, hr = hc
            tf = r0 + t
            tr = r0 + n - 1 - t
            hf = af_s[pl.ds(tf, 1), :] * hf + bf_s[pl.ds(tf, 1), :]
            hr = ar_s[pl.ds(tr, 1), :] * hr + br_s[pl.ds(tr, 1), :]
            hf_s[pl.ds(tf, 1), :] = hf
            hr_s[pl.ds(tr, 1), :] = hr
            return hf, hr
        return lax.fori_loop(0, n, step, carry, unroll=8)

    h0 = jnp.zeros((1, LANES), F32)
    carry = scan_segment(0, n_ctx, (h0, h0))
    scan_segment(n_ctx, n_lat, carry)
    o_ref[0] = ((hf_s[...] + hr_s[...]) * _gelu_tanh(ug_ref[0])).astype(o_ref.dtype)


def _block_diag_tiles(w):
    w = w.reshape(2, LRU_HEADS // 2, 2, LRU_HEAD_DIM, LRU_HEAD_DIM)
    z = jnp.zeros_like(w[:, :, 0])
    top = jnp.concatenate([w[:, :, 0], z], axis=-1)
    bot = jnp.concatenate([z, w[:, :, 1]], axis=-1)
    return jnp.concatenate([top, bot], axis=-2).astype(BF16)


def _rglru_mixer(u_all, n_ctx, conv_w, conv_b, wa, ba, wx, bx, lam):
    bsz, n_all, _ = u_all.shape
    n_lat = n_all - n_ctx
    n_tiles = LRU_WIDTH // LANES
    gate_blk0 = HY_IN // LANES
    rec_blk0 = (HY_IN + LRU_WIDTH) // LANES
    decay_scale = -LRU_C * jax.nn.softplus(-lam)
    seq = lambda col0: pl.BlockSpec((1, n_all, LANES), lambda b, j: (b, 0, col0 + j))
    vec = lambda rows: pl.BlockSpec((rows, LANES), lambda b, j: (0, j))
    mat = pl.BlockSpec((2, 1, LANES, LANES), lambda b, j: (0, j, 0, 0))
    full = pltpu.VMEM((n_all, LANES), F32)
    return pl.pallas_call(
        partial(_rglru_body, n_ctx=n_ctx, n_lat=n_lat),
        grid=(bsz, n_tiles),
        in_specs=[seq(gate_blk0), seq(rec_blk0), vec(LRU_CONV), vec(1), mat, vec(2), mat, vec(2), vec(2)],
        out_specs=pl.BlockSpec((1, n_all, LANES), lambda b, j: (b, 0, j)),
        out_shape=jax.ShapeDtypeStruct((bsz, n_all, LRU_WIDTH), BF16),
        scratch_shapes=[pltpu.VMEM((n_lat + 2 * LRU_CONV_MARGIN, LANES), F32),
                        full, full, full, full, full, full],
        compiler_params=pltpu.CompilerParams(
            dimension_semantics=("parallel", "parallel"),
            vmem_limit_bytes=VMEM_LIMIT_BYTES),
        name="rglru",
    )(u_all, u_all, conv_w, conv_b.reshape(1, LRU_WIDTH), _block_diag_tiles(wa), ba,
      _block_diag_tiles(wx), bx, decay_scale)


ATT_Q_TILE = 256
ATT_KV_CHUNK = 2048


def _attn_body(q_ref, k_ref, v_ref, lv_ref, g_ref, o_ref, *, lam_init):
    lv = lv_ref[...]
    lam = (jnp.exp(jnp.sum(lv[0:1] * lv[1:2], keepdims=True))
           - jnp.exp(jnp.sum(lv[2:3] * lv[3:4], keepdims=True)) + lam_init)
    q = q_ref[0]
    nk = k_ref.shape[1]
    work = [(m, c0, min(c0 + ATT_KV_CHUNK, nk)) for c0 in range(0, nk, ATT_KV_CHUNK) for m in range(2)]

    def scores(m, c0, c1):
        qm = q[:, m * ATT_HEAD_DIM:(m + 1) * ATT_HEAD_DIM]
        km = k_ref[0, c0:c1, m * ATT_HEAD_DIM:(m + 1) * ATT_HEAD_DIM]
        return lax.dot_general(qm, km, (((1,), (1,)), ((), ())), preferred_element_type=F32)

    state = [None, None]
    s_next = scores(*work[0])
    for j, (m, c0, c1) in enumerate(work):
        s = s_next
        if j + 1 < len(work):
            s_next = scores(*work[j + 1])
        chunk_max = jnp.max(s, axis=-1, keepdims=True)
        v = v_ref[0, c0:c1, :]
        if state[m] is None:
            p = jnp.exp(s - chunk_max)
            state[m] = (chunk_max, jnp.sum(p, axis=-1, keepdims=True),
                        jnp.dot(p.astype(BF16), v, preferred_element_type=F32))
        else:
            mx_old, sum_old, acc_old = state[m]
            mx = jnp.maximum(mx_old, chunk_max)
            rescale = jnp.exp(mx_old - mx)
            p = jnp.exp(s - mx)
            state[m] = (mx, rescale * sum_old + jnp.sum(p, axis=-1, keepdims=True),
                        rescale * acc_old + jnp.dot(p.astype(BF16), v, preferred_element_type=F32))
    o = state[0][2] / state[0][1] - lam * (state[1][2] / state[1][1])
    y = o * lax.rsqrt(jnp.mean(o * o, axis=-1, keepdims=True) + LN_EPS) * g_ref[...] * (1.0 - lam_init)
    o_ref[0] = y.astype(o_ref.dtype)


def _diff_attention(qkv, lam_vecs, gain, lam_init, q_row0, n_q, n_k):
    bsz = qkv.shape[0]
    tq = ATT_Q_TILE
    q_blk0 = q_row0 // tq
    n_heads = ATT_HEADS
    return pl.pallas_call(
        partial(_attn_body, lam_init=lam_init),
        grid=(bsz, n_heads, n_q // tq),
        in_specs=[
            pl.BlockSpec((1, tq, ATT_V_DIM), lambda b, h, i: (b, q_blk0 + i, h)),
            pl.BlockSpec((1, n_k, ATT_V_DIM), lambda b, h, i: (b, 0, n_heads + h)),
            pl.BlockSpec((1, n_k, ATT_V_DIM), lambda b, h, i: (b, 0, 2 * n_heads + h)),
            pl.BlockSpec((4, ATT_HEAD_DIM), lambda b, h, i: (0, 0)),
            pl.BlockSpec((1, ATT_V_DIM), lambda b, h, i: (0, 0)),
        ],
        out_specs=pl.BlockSpec((1, tq, ATT_V_DIM), lambda b, h, i: (b, i, h)),
        out_shape=jax.ShapeDtypeStruct((bsz, n_q, ATT_WIDTH), BF16),
        compiler_params=pltpu.CompilerParams(
            dimension_semantics=("parallel", "parallel", "arbitrary"),
            vmem_limit_bytes=VMEM_LIMIT_BYTES),
        name="diff_attention",
    )(qkv, qkv, qkv, lam_vecs, gain.reshape(1, ATT_V_DIM))


def _axial_rope_tables(n):
    rows = n // GRID_W
    row = jnp.broadcast_to(jnp.arange(rows, dtype=F32)[:, None], (rows, GRID_W)).reshape(-1)
    col = jnp.broadcast_to(jnp.arange(GRID_W, dtype=F32)[None, :], (rows, GRID_W)).reshape(-1)
    inv = ROPE_THETA ** (-jnp.arange(0, ROPE_AXIS_DIM, 2, dtype=F32) / ROPE_AXIS_DIM)
    ang = jnp.stack([row[:, None] * inv, col[:, None] * inv], axis=1)
    return jnp.cos(ang), jnp.sin(ang)


def _apply_axial_rope(x, cos, sin):
    xs = x.reshape(x.shape[:-1] + (2, 2, ROPE_AXIS_DIM // 2))
    x1, x2 = xs[..., 0, :], xs[..., 1, :]
    cb, sb = cos[None, :, None, None], sin[None, :, None, None]
    out = jnp.stack([x1 * cb - x2 * sb, x2 * cb + x1 * sb], axis=-2)
    return out.reshape(x.shape)


def _diff_attention_mixer(u_att, n_ctx, lam_vecs, subln_gain, lam_init, need_ctx):
    bsz, n_all, _ = u_att.shape
    n_lat = n_all - n_ctx
    scale = ATT_HEAD_DIM ** -0.5
    u_lat = u_att[:, n_ctx:]
    u_ctx = u_att[:, :n_ctx]
    q_l = u_lat[..., :ATT_QK].reshape(bsz, n_lat, ATT_HEADS, 2, ATT_HEAD_DIM)
    k_l = u_lat[..., ATT_QK:2 * ATT_QK].reshape(bsz, n_lat, ATT_HEADS, 2, ATT_HEAD_DIM)
    cos, sin = _axial_rope_tables(n_lat)
    q_l = _apply_axial_rope(q_l, cos, sin) * scale
    k_l = _apply_axial_rope(k_l, cos, sin)
    lat = jnp.concatenate([q_l.reshape(bsz, n_lat, ATT_QK), k_l.reshape(bsz, n_lat, ATT_QK),
                           u_lat[..., 2 * ATT_QK:]], axis=-1).astype(BF16)
    cx = jnp.concatenate([u_ctx[..., :ATT_QK] * scale, u_ctx[..., ATT_QK:]], axis=-1).astype(BF16)
    qkv = jnp.concatenate([cx, lat], axis=1)
    y_lat = _diff_attention(qkv, lam_vecs, subln_gain, lam_init, n_ctx, n_lat, n_all)
    y_ctx = _diff_attention(qkv, lam_vecs, subln_gain, lam_init, 0, n_ctx, n_ctx) if need_ctx else None
    return y_lat, y_ctx


ROW_TILE = 256
IN_PROJ_COL_TILE = 1408
ADA_COL_TILE = 1536
COMBINE_ROW_TILE = 128


def _ln_rows(x):
    mu = jnp.mean(x, axis=-1, keepdims=True)
    xc = x - mu
    return xc * lax.rsqrt(jnp.mean(xc * xc, axis=-1, keepdims=True) + LN_EPS)


def _segments(bsz, seg_rows, tile):
    ids = []
    for b in range(bsz):
        for rows, kind in seg_rows:
            assert rows % tile == 0
            ids += [bsz if kind == 'ctx' else b] * (rows // tile)
    return jnp.asarray(ids, jnp.int32)


def _ada_body(c_ref, w_ref, b_ref, o_ref):
    c = c_ref[...]
    w = w_ref[0]
    w_hi = w.astype(BF16)
    w_lo = (w - w_hi.astype(F32)).astype(BF16)
    o_ref[...] = _split_dot(c * jax.nn.sigmoid(c), w_hi, w_lo) + b_ref[0]


def _ada_modulation(cond, ada_w, ada_b, layer):
    s, d = cond.shape
    n = ada_w.shape[-1]
    rows = -(-s // SUBLANES) * SUBLANES
    tn = ADA_COL_TILE
    assert n % tn == 0
    out = pl.pallas_call(
        _ada_body, grid=(n // tn,),
        in_specs=[pl.BlockSpec((rows, d), lambda j: (0, 0)),
                  pl.BlockSpec((1, d, tn), lambda j: (layer, 0, j)),
                  pl.BlockSpec((1, 1, tn), lambda j: (layer, 0, j))],
        out_specs=pl.BlockSpec((rows, tn), lambda j: (0, j)),
        out_shape=jax.ShapeDtypeStruct((rows, n), F32),
        compiler_params=pltpu.CompilerParams(dimension_semantics=("parallel",),
                                             vmem_limit_bytes=VMEM_LIMIT_BYTES),
        name="ada_modulation",
    )(jnp.pad(cond, ((0, rows - s), (0, 0))), ada_w, ada_b.reshape(ada_b.shape[0], 1, n))
    return out[:s]


def _ln_mm_body(seg_ref, x_ref, mod_ref, w_ref, o_ref, wb_s):
    @pl.when(pl.program_id(1) == 0)
    def _():
        wb_s[...] = w_ref[0].astype(BF16)

    m = mod_ref[0]
    h = _ln_rows(x_ref[...]) * (1.0 + m[1:2]) + m[0:1]
    o_ref[...] = jnp.dot(h.astype(BF16), wb_s[...], preferred_element_type=F32)


def _ln_mod_matmul(x, mod, seg, w, layer):
    t, d = x.shape
    n = w.shape[-1]
    tm, tn = ROW_TILE, IN_PROJ_COL_TILE
    assert t % tm == 0 and n % tn == 0
    grid_spec = pltpu.PrefetchScalarGridSpec(
        num_scalar_prefetch=1,
        grid=(n // tn, t // tm),
        in_specs=[pl.BlockSpec((tm, d), lambda j, i, s: (i, 0)),
                  pl.BlockSpec((1, 2, d), lambda j, i, s: (s[i], 0, 0)),
                  pl.BlockSpec((1, d, tn), lambda j, i, s: (layer, 0, j))],
        out_specs=pl.BlockSpec((tm, tn), lambda j, i, s: (i, j)),
        scratch_shapes=[pltpu.VMEM((d, tn), BF16)],
    )
    return pl.pallas_call(
        _ln_mm_body, grid_spec=grid_spec,
        out_shape=jax.ShapeDtypeStruct((t, n), F32),
        compiler_params=pltpu.CompilerParams(dimension_semantics=("arbitrary", "arbitrary"),
                                             vmem_limit_bytes=VMEM_LIMIT_BYTES),
        name="ln_mod_in_proj",
    )(seg, x, mod, w)


def _split_dot(a, b_hi, b_lo):
    a_hi = a.astype(BF16)
    a_lo = (a - a_hi.astype(F32)).astype(BF16)
    return (jnp.dot(a_hi, b_hi, preferred_element_type=F32) + jnp.dot(a_hi, b_lo, preferred_element_type=F32)
            + jnp.dot(a_lo, b_hi, preferred_element_type=F32))


def _out_ln_body(seg_ref, mixed_ref, x_ref, mod_ref, w_ref, lng_ref, rwh_ref, rwl_ref, rb_ref,
                 x1_ref, h2_ref, h2b_ref, ti_ref, tw_ref):
    m = mod_ref[0]
    mix = jnp.dot(mixed_ref[...], w_ref[...], preferred_element_type=F32)
    x1 = _ln_rows(DEEPNORM_ALPHA * x_ref[...] + m[0:1] * mix) * lng_ref[0:1] + lng_ref[1:2]
    x1_ref[...] = x1
    h2 = _ln_rows(x1) * (1.0 + m[2:3]) + m[1:2]
    h2_ref[...] = h2
    h2b_ref[...] = h2.astype(BF16)
    scores = jax.nn.sigmoid(_split_dot(h2, rwh_ref[...], rwl_ref[...]))
    lane = lax.broadcasted_iota(jnp.int32, scores.shape, 1).astype(F32)
    biased = jnp.where(lane < N_EXPERTS, scores + rb_ref[...], -jnp.inf)
    top_i = jnp.zeros(scores.shape, F32)
    top_s = jnp.zeros(scores.shape, F32)
    for k in range(TOP_K):
        best = jnp.max(biased, axis=-1, keepdims=True)
        pick = jnp.min(jnp.where(biased == best, lane, float(LANES)), axis=-1, keepdims=True)
        hit = lane == pick
        top_i = jnp.where(lane == k, pick, top_i)
        top_s = jnp.where(lane == k, jnp.sum(jnp.where(hit, scores, 0.0), axis=-1, keepdims=True), top_s)
        biased = jnp.where(hit, -jnp.inf, biased)
    ti_ref[...] = top_i.astype(jnp.int32)
    tw_ref[...] = top_s / jnp.sum(top_s, axis=-1, keepdims=True) * ROUTED_SCALE


def _out_proj_norm_route(mixed, x, mod, seg, w, ln_gb, router_w, router_b):
    t, d = x.shape
    tm = ROW_TILE
    rw = jnp.pad(router_w, ((0, 0), (0, LANES - N_EXPERTS)))
    rw_hi, rw_lo = _split_bf16(rw)
    rb = jnp.pad(router_b, (0, LANES - N_EXPERTS)).reshape(1, LANES)
    row = lambda width: pl.BlockSpec((tm, width), lambda i, s: (i, 0))
    const = lambda shape: pl.BlockSpec(shape, lambda i, s: (0,) * len(shape))
    grid_spec = pltpu.PrefetchScalarGridSpec(
        num_scalar_prefetch=1,
        grid=(t // tm,),
        in_specs=[row(d), row(d), pl.BlockSpec((1, 3, d), lambda i, s: (s[i], 0, 0)), const((d, d)),
                  const((2, d)), const((d, LANES)), const((d, LANES)), const((1, LANES))],
        out_specs=[row(d), row(d), row(d), row(LANES), row(LANES)],
    )
    x1, h2, h2b, top_i, top_w = pl.pallas_call(
        _out_ln_body, grid_spec=grid_spec,
        out_shape=[jax.ShapeDtypeStruct((t, d), F32), jax.ShapeDtypeStruct((t, d), F32),
                   jax.ShapeDtypeStruct((t, d), BF16), jax.ShapeDtypeStruct((t, LANES), jnp.int32),
                   jax.ShapeDtypeStruct((t, LANES), F32)],
        compiler_params=pltpu.CompilerParams(dimension_semantics=("parallel",),
                                             vmem_limit_bytes=VMEM_LIMIT_BYTES),
        name="out_proj_norm_route",
    )(seg, mixed, x, mod, w, ln_gb, rw_hi, rw_lo, rb)
    return x1, h2, h2b, top_i[:, :TOP_K], top_w[:, :TOP_K]


def _shared_body(h_ref, wgu_ref, wd_ref, o_ref):
    gu = jnp.dot(h_ref[...], wgu_ref[...], preferred_element_type=F32)
    f = wd_ref.shape[0]
    g, u = gu[:, :f], gu[:, f:]
    act = (g * jax.nn.sigmoid(g) * u).astype(BF16)
    o_ref[...] = jnp.dot(act, wd_ref[...], preferred_element_type=F32)


def _shared_expert(hb, w_gate, w_up, w_down):
    t, d = hb.shape
    f = w_gate.shape[1]
    tm = ROW_TILE
    wgu = jnp.concatenate([w_gate, w_up], axis=1).astype(BF16)
    return pl.pallas_call(
        _shared_body, grid=(t // tm,),
        in_specs=[pl.BlockSpec((tm, d), lambda i: (i, 0)), pl.BlockSpec((d, 2 * f), lambda i: (0, 0)),
                  pl.BlockSpec((f, d), lambda i: (0, 0))],
        out_specs=pl.BlockSpec((tm, d), lambda i: (i, 0)),
        out_shape=jax.ShapeDtypeStruct((t, d), F32),
        compiler_params=pltpu.CompilerParams(dimension_semantics=("parallel",),
                                             vmem_limit_bytes=VMEM_LIMIT_BYTES),
        name="shared_expert",
    )(hb, wgu, w_down.astype(BF16))


def _combine_body(seg_ref, *refs):
    y_refs, (sh_ref, x_ref, mod_ref, lng_ref, o_ref) = refs[:TOP_K], refs[TOP_K:]
    ffn = sh_ref[...]
    for y_ref in y_refs:
        ffn = ffn + y_ref[...]
    o_ref[...] = _ln_rows(DEEPNORM_ALPHA * x_ref[...] + mod_ref[0] * ffn) * lng_ref[0:1] + lng_ref[1:2]


def _combine_norm(y, shared, x1, gate, seg, ln_gb):
    t, d = x1.shape
    tm = COMBINE_ROW_TILE
    tiles = t // tm
    row = pl.BlockSpec((tm, d), lambda i, s: (i, 0))
    y_specs = [pl.BlockSpec((tm, d), lambda i, s, k=k: (k * tiles + i, 0)) for k in range(TOP_K)]
    grid_spec = pltpu.PrefetchScalarGridSpec(
        num_scalar_prefetch=1,
        grid=(tiles,),
        in_specs=y_specs + [row, row, pl.BlockSpec((1, 1, d), lambda i, s: (s[i], 0, 0)),
                            pl.BlockSpec((2, d), lambda i, s: (0, 0))],
        out_specs=row,
    )
    return pl.pallas_call(
        _combine_body, grid_spec=grid_spec,
        out_shape=jax.ShapeDtypeStruct((t, d), F32),
        compiler_params=pltpu.CompilerParams(dimension_semantics=("parallel",),
                                             vmem_limit_bytes=VMEM_LIMIT_BYTES),
        name="combine_norm",
    )(seg, *([y] * TOP_K), shared, x1, gate, ln_gb)


MOE_BLOCK_ROWS = 256
MOE_VMEM_LIMIT_BYTES = 56 * 1024 * 1024
MOE_DMA_THREADS = 2
MOE_WEIGHT_DMA_PRIORITY = 1


def _moe_body(be_ref, nu_ref, ne_ref, idx_ref, idx_nxt_ref, rw_ref, wg_hbm, wu_hbm, wd_hbm, h_hbm, o_hbm,
              xbuf, ybuf, wg_f, wu_f, wd_f, wg_s, wu_s, wd_s, gsem, ssem, wsem, *, bm, layer):
    i = pl.program_id(0)
    n_used = nu_ref[0]
    slot = i % 2

    def weight_copies(expert):
        return (pltpu.make_async_copy(wg_hbm.at[layer, expert], wg_f, wsem.at[0]),
                pltpu.make_async_copy(wu_hbm.at[layer, expert], wu_f, wsem.at[1]),
                pltpu.make_async_copy(wd_hbm.at[layer, expert], wd_f, wsem.at[2]))

    def gather_rows(tok_ref, dst_slot):
        for r in range(bm):
            pltpu.make_async_copy(h_hbm.at[pl.ds(tok_ref[0, 0, r], 1)], xbuf.at[dst_slot, pl.ds(r, 1)],
                                  gsem.at[dst_slot]).start(priority=r % MOE_DMA_THREADS)

    def wait_gather(s):
        pltpu.make_async_copy(h_hbm.at[pl.ds(0, bm)], xbuf.at[s], gsem.at[s]).wait()

    def wait_scatter(s):
        pltpu.make_async_copy(ybuf.at[s], o_hbm.at[pl.ds(0, bm)], ssem.at[s]).wait()

    e = be_ref[i]
    e_prev = be_ref[jnp.maximum(i - 1, 0)]

    @pl.when(i == 0)
    def _():
        gather_rows(idx_ref, 0)
        for copy in weight_copies(e):
            copy.start(priority=MOE_WEIGHT_DMA_PRIORITY)

    @pl.when((i < n_used) & ((i == 0) | (e != e_prev)))
    def _():
        for copy in weight_copies(e):
            copy.wait()
        wg_s[...] = wg_f[...].astype(BF16)
        wu_s[...] = wu_f[...].astype(BF16)
        wd_s[...] = wd_f[...].astype(BF16)

        @pl.when(ne_ref[i] >= 0)
        def _():
            for copy in weight_copies(ne_ref[i]):
                copy.start(priority=MOE_WEIGHT_DMA_PRIORITY)

    def block_step(s):
        wait_gather(s)

        @pl.when(i >= 2)
        def _():
            wait_scatter(s)

        gather_rows(idx_nxt_ref, 1 - s)

        xb = xbuf[s].astype(BF16)
        g = jnp.dot(xb, wg_s[...], preferred_element_type=F32)
        u = jnp.dot(xb, wu_s[...], preferred_element_type=F32)
        act = (g * jax.nn.sigmoid(g) * u).astype(BF16)
        ybuf[s] = jnp.dot(act, wd_s[...], preferred_element_type=F32) * rw_ref[...]
        for r in range(bm):
            pltpu.make_async_copy(ybuf.at[s, pl.ds(r, 1)], o_hbm.at[pl.ds(idx_ref[0, 0, bm + r], 1)],
                                  ssem.at[s]).start(priority=r % MOE_DMA_THREADS)

        @pl.when(i == n_used - 1)
        def _():
            @pl.when(i >= 1)
            def _():
                wait_scatter(1 - s)
            wait_scatter(s)
            wait_gather(1 - s)

    for s in range(2):
        pl.when((i < n_used) & (slot == s))(partial(block_step, s))

    @pl.when(i >= n_used)
    def _():
        xbuf[0] = jnp.zeros(xbuf.shape[1:], F32)
        fill = pltpu.make_async_copy(xbuf.at[0], o_hbm.at[pl.ds(pl.multiple_of(i * bm, bm), bm)], gsem.at[0])
        fill.start()
        fill.wait()


def _routed_experts(h, idx, row_w, block_e, n_used, next_e, w_gate, w_up, w_down, layer):
    bm = MOE_BLOCK_ROWS
    n_blocks = idx.shape[0]
    d = h.shape[1]
    f = w_gate.shape[-1]
    anywhere = pl.BlockSpec(memory_space=pl.ANY)
    grid_spec = pltpu.PrefetchScalarGridSpec(
        num_scalar_prefetch=3,
        grid=(n_blocks,),
        in_specs=[
            pl.BlockSpec((1, 1, 2 * bm), lambda i, be, nu, ne: (i, 0, 0), memory_space=pltpu.SMEM),
            pl.BlockSpec((1, 1, 2 * bm), lambda i, be, nu, ne: (jnp.minimum(i + 1, n_blocks - 1), 0, 0),
                         memory_space=pltpu.SMEM),
            pl.BlockSpec((bm, 1), lambda i, be, nu, ne: (i, 0)),
            anywhere, anywhere, anywhere, anywhere,
        ],
        out_specs=anywhere,
        scratch_shapes=[pltpu.VMEM((2, bm, d), F32), pltpu.VMEM((2, bm, d), F32),
                        pltpu.VMEM((d, f), F32), pltpu.VMEM((d, f), F32), pltpu.VMEM((f, d), F32),
                        pltpu.VMEM((d, f), BF16), pltpu.VMEM((d, f), BF16), pltpu.VMEM((f, d), BF16),
                        pltpu.SemaphoreType.DMA((2,)), pltpu.SemaphoreType.DMA((2,)),
                        pltpu.SemaphoreType.DMA((3,))],
    )
    return pl.pallas_call(
        partial(_moe_body, bm=bm, layer=layer),
        grid_spec=grid_spec,
        out_shape=jax.ShapeDtypeStruct((n_blocks * bm, d), F32),
        compiler_params=pltpu.CompilerParams(
            dimension_semantics=("arbitrary",),
            vmem_limit_bytes=MOE_VMEM_LIMIT_BYTES),
        name="routed_experts",
    )(block_e, n_used, next_e, idx, idx, row_w, w_gate, w_up, w_down, h)


def _route(top_idx, top_w):
    bm = MOE_BLOCK_ROWS
    n_tok = top_idx.shape[0]
    n_assign = n_tok * TOP_K
    flat_e = top_idx.reshape(-1)
    flat_w = top_w.reshape(-1)
    order = jnp.argsort(flat_e).astype(jnp.int32)
    counts = jnp.sum(flat_e[None, :] == jnp.arange(N_EXPERTS)[:, None], axis=1).astype(jnp.int32)
    padded = (counts + bm - 1) // bm * bm
    padded_end = jnp.cumsum(padded)
    live_end = jnp.cumsum(counts)
    shift = padded_end - padded - (live_end - counts)
    n_blocks = -(-n_assign // bm) + N_EXPERTS
    block_row0 = jnp.arange(n_blocks, dtype=jnp.int32) * bm
    block_e = jnp.minimum(jnp.sum(padded_end[None, :] <= block_row0[:, None], axis=1),
                          N_EXPERTS - 1).astype(jnp.int32)
    n_used = (padded_end[-1:] // bm).astype(jnp.int32)
    experts = jnp.arange(N_EXPERTS, dtype=jnp.int32)
    later = jnp.where((experts[None, :] > experts[:, None]) & (counts[None, :] > 0), experts[None, :], N_EXPERTS)
    next_live = jnp.min(later, axis=1)
    next_e = jnp.where(next_live < N_EXPERTS, next_live, -1)[block_e].astype(jnp.int32)
    rows = jnp.arange(n_blocks * bm, dtype=jnp.int32).reshape(n_blocks, bm)
    row_slot = rows - shift[block_e][:, None]
    row_live = row_slot < live_end[block_e][:, None]
    row_assign = order[jnp.where(row_live, row_slot, 0)]
    row_tok = jnp.where(row_live, row_assign // TOP_K, 0)
    row_dst = jnp.where(row_live, (row_assign % TOP_K) * n_tok + row_tok,
                        n_assign + rows - live_end[block_e][:, None])
    row_w = jnp.where(row_live, flat_w[row_assign], 0.0)
    idx = jnp.concatenate([row_tok, row_dst], axis=1).astype(jnp.int32)[:, None, :]
    return idx, row_w.reshape(-1, 1), block_e, n_used, next_e


def kernel(x, c, ctx, c_ctx, ada_w, ada_b, w_in, hy_conv_w, hy_conv_b, hy_w1, hy_b1, hy_w2, hy_b2, hy_w3, hy_freq, hy_skip, lru_conv_w, lru_conv_b, lru_wa, lru_ba, lru_wx, lru_bx, lru_lambda, att_lambda, att_subln, w_out, ln_g, ln_b, router_w, router_b, exp_w_gate, exp_w_up, exp_w_down, sh_w_gate, sh_w_up, sh_w_down):
    bsz, n_lat, d = x.shape
    n_ctx = ctx.shape[1]
    n_all = n_ctx + n_lat
    x_all = jnp.concatenate([ctx, x], axis=1).reshape(bsz * n_all, d)
    both = ((n_ctx, 'ctx'), (n_lat, 'lat'))
    seg_all = _segments(bsz, both, ROW_TILE)
    x_out = None

    for l in range(DEPTH):
        need_ctx = l < DEPTH - 1
        lam_init = 0.8 - 0.6 * math.exp(-0.3 * l)
        cond = jnp.concatenate([c, c_ctx[None, :]], axis=0)
        mod = _ada_modulation(cond, ada_w, ada_b, l)
        sh1, sc1, g1, sh2, sc2, g2 = jnp.split(mod, 6, axis=-1)
        hy_args = (hy_conv_w[l], hy_conv_b[l], hy_w1[l], hy_b1[l], hy_w2[l], hy_b2[l],
                   hy_w3[l], hy_freq[l], hy_skip[l])

        u_all = _ln_mod_matmul(x_all, jnp.stack([sh1, sc1], axis=1), seg_all, w_in, l).reshape(bsz, n_all, -1)
        y_hy_l = _hyena_mixer(u_all[:, n_ctx:, :HY_IN], *hy_args)
        y_lru = _rglru_mixer(u_all, n_ctx, lru_conv_w[l], lru_conv_b[l], lru_wa[l], lru_ba[l],
                             lru_wx[l], lru_bx[l], lru_lambda[l])
        y_att_l, y_att_c = _diff_attention_mixer(u_all[..., HY_IN + LRU_IN:], n_ctx, att_lambda[l],
                                                 att_subln[l], lam_init, need_ctx)
        if need_ctx:
            y_hy = jnp.concatenate([_hyena_mixer(u_all[:, :n_ctx, :HY_IN], *hy_args), y_hy_l], axis=1)
            mixed = jnp.concatenate([y_hy.astype(BF16), y_lru, jnp.concatenate([y_att_c, y_att_l], axis=1)], axis=-1)
            x_res, seg, seg_c = x_all, seg_all, _segments(bsz, both, COMBINE_ROW_TILE)
        else:
            mixed = jnp.concatenate([y_hy_l.astype(BF16), y_lru[:, n_ctx:], y_att_l], axis=-1)
            x_res = x_all.reshape(bsz, n_all, d)[:, n_ctx:].reshape(bsz * n_lat, d)
            seg = _segments(bsz, ((n_lat, 'lat'),), ROW_TILE)
            seg_c = _segments(bsz, ((n_lat, 'lat'),), COMBINE_ROW_TILE)
        n_tok = x_res.shape[0]
        ln_gb1 = jnp.stack([ln_g[l, 0], ln_b[l, 0]])
        ln_gb2 = jnp.stack([ln_g[l, 1], ln_b[l, 1]])
        x1, h2, h2b, top_idx, top_w = _out_proj_norm_route(
            mixed.reshape(n_tok, d), x_res, jnp.stack([g1, sh2, sc2], axis=1), seg,
            w_out[l].astype(BF16), ln_gb1, router_w[l], router_b[l])

        idx, row_w, block_e, n_used, next_e = _route(top_idx, top_w)
        y = _routed_experts(h2, idx, row_w, block_e, n_used, next_e, exp_w_gate, exp_w_up, exp_w_down, l)
        shared = _shared_expert(h2b, sh_w_gate[l], sh_w_up[l], sh_w_down[l])
        x_out = _combine_norm(y, shared, x1, g2[:, None, :], seg_c, ln_gb2)
        x_all = x_out
    return x_out.reshape(bsz, n_lat, d)
```

```python
import math
from functools import lru_cache, partial

import numpy as np
import jax
import jax.numpy as jnp
from jax import lax
from jax.experimental import pallas as pl
from jax.experimental.pallas import tpu as pltpu

D_MODEL = 2048
DEPTH = 2
GRID_W = 64
F32 = jnp.float32
BF16 = jnp.bfloat16
LANES = 128
SUBLANES = 8

HY_WIDTH = D_MODEL // 4
LRU_WIDTH = D_MODEL // 4
ATT_WIDTH = D_MODEL // 2
HY_ORDER = 2
HY_CONV = 3
HY_BANDS = 8
HY_TARGET = 1e-2
HY_MAX_DECAY = math.log(HY_TARGET) / 0.3
HY_MIN_DECAY = math.log(HY_TARGET) / 1.5
HY_IN = (HY_ORDER + 1) * HY_WIDTH

LRU_HEADS = 8
LRU_HEAD_DIM = LRU_WIDTH // LRU_HEADS
LRU_CONV = 4
LRU_C = 8.0
LRU_IN = 2 * LRU_WIDTH

ATT_HEADS = 4
ATT_V_DIM = ATT_WIDTH // ATT_HEADS
ATT_HEAD_DIM = ATT_V_DIM // 2
ATT_QK = ATT_HEADS * 2 * ATT_HEAD_DIM
ROPE_THETA = 10000.0
ROPE_AXIS_DIM = ATT_HEAD_DIM // 2

N_EXPERTS = 64
TOP_K = 8
ROUTED_SCALE = 2.5

LN_EPS = 1e-5
DEEPNORM_ALPHA = (2 * DEPTH) ** 0.25

VMEM_LIMIT_BYTES = 48 * 1024 * 1024


def _depthwise_conv(x, w, b, pad_left, pad_right):
    y = lax.conv_general_dilated(
        x, w[:, None, :], window_strides=(1,),
        padding=[(pad_left, pad_right)], dimension_numbers=('NWC', 'WIO', 'NWC'),
        feature_group_count=x.shape[-1])
    return y + b


def _hyena_filter(n, w1, b1, w2, b2, w3, freq):
    pos = jnp.arange(n, dtype=F32)
    t = (pos / max(n - 1, 1))[:, None]
    bands = jnp.linspace(1e-4, HY_BANDS - 1, HY_BANDS, dtype=F32)
    ang = (2.0 * math.pi / n) * pos[:, None] * bands
    feats = jnp.concatenate([t, jnp.cos(ang), -jnp.sin(ang)], axis=-1)
    hp = lax.Precision.HIGHEST
    decay = jnp.abs(jnp.linspace(HY_MIN_DECAY, HY_MAX_DECAY, HY_WIDTH, dtype=F32))
    side_cols = HY_ORDER * HY_WIDTH

    def one_side(feats, t, w3_side):
        hid = jnp.sin(freq * (jnp.dot(feats, w1, precision=hp) + b1))
        hid = jnp.sin(freq * (jnp.dot(hid, w2, precision=hp) + b2))
        filt = jnp.dot(hid, w3_side, precision=hp).reshape(n, HY_ORDER, HY_WIDTH)
        return filt * jnp.exp(-t * decay)[:, None, :]

    fwd = one_side(feats, t, w3[:, :side_cols])
    bwd_flipped = one_side(feats[::-1], t[::-1], w3[:, side_cols:])
    two_sided = jnp.concatenate([fwd, jnp.zeros_like(fwd[:1]), bwd_flipped[:n - 1]], axis=0)
    return two_sided / jnp.sum(jnp.abs(two_sided), axis=0, keepdims=True)


DFT_BLOCK_BYTES = 4 * 1024 * 1024
DFT_MINOR_LEN = 128
DFT_SINGLE_LEVEL_MAX = 1024


def _bmm3_body(lh_ref, ll_ref, r_ref, o_ref, *, nb, shared):
    for b in range(nb):
        r = r_ref[b]
        r_hi = r.astype(BF16)
        r_lo = (r - r_hi.astype(F32)).astype(BF16)
        lh = lh_ref[0 if shared else b]
        ll = ll_ref[0 if shared else b]
        o_ref[b] = (jnp.dot(lh, r_hi, preferred_element_type=F32)
                    + jnp.dot(lh, r_lo, preferred_element_type=F32)
                    + jnp.dot(ll, r_hi, preferred_element_type=F32))


def _bmm3(l_hi, l_lo, r, nb):
    batch, k, c = r.shape
    m = l_hi.shape[1]
    shared = l_hi.shape[0] == 1
    nb = max(1, min(nb, batch, DFT_BLOCK_BYTES // (4 * c * max(k, m))))
    assert batch % nb == 0
    l_spec = pl.BlockSpec((1 if shared else nb, m, k), (lambda i: (0, 0, 0)) if shared else (lambda i: (i, 0, 0)))
    return pl.pallas_call(
        partial(_bmm3_body, nb=nb, shared=shared),
        grid=(batch // nb,),
        in_specs=[l_spec, l_spec, pl.BlockSpec((nb, k, c), lambda i: (i, 0, 0))],
        out_specs=pl.BlockSpec((nb, m, c), lambda i: (i, 0, 0)),
        out_shape=jax.ShapeDtypeStruct((batch, m, c), F32),
        compiler_params=pltpu.CompilerParams(dimension_semantics=("parallel",), vmem_limit_bytes=VMEM_LIMIT_BYTES),
        name="dft_matmul",
    )(l_hi, l_lo, r)


def _stack_complex(f):
    return np.concatenate([np.concatenate([f.real, -f.imag], axis=-1),
                           np.concatenate([f.imag, f.real], axis=-1)], axis=-2)


def _split_bf16(a):
    a = jnp.asarray(a, F32)
    hi = a.astype(BF16)
    return hi, (a - hi.astype(F32)).astype(BF16)


@lru_cache(maxsize=None)
def _dft_tables(n1_io, n1, n2):
    n = n1 * n2
    k1 = np.arange(n1)
    a = np.arange(n1_io)
    b = np.arange(n2)
    ang = -2.0 * np.pi * (((np.outer(k1, a) * n2)[None] + b[:, None, None] * k1[None, :, None]) % n) / n
    fa = np.exp(1j * ang)
    fb = np.exp(-2j * np.pi * (np.outer(b, b) % n2) / n2)
    return dict(
        fwd_a=_stack_complex(fa),
        fwd_a_real=np.concatenate([fa.real, fa.imag], axis=-2),
        fwd_b=_stack_complex(fb)[None],
        inv_b=_stack_complex(np.conj(fb).T)[None],
        inv_a=_stack_complex(np.conj(fa).transpose(0, 2, 1) / n),
    )


def _dft_split(length):
    if length <= DFT_SINGLE_LEVEL_MAX:
        return 1, length
    return length // DFT_MINOR_LEN, DFT_MINOR_LEN


def _swap_stage(x, n1, n2):
    c = x.shape[-1]
    return x.reshape(n2, 2, n1, c).transpose(2, 1, 0, 3).reshape(n1, 2 * n2, c)


def _dft_forward(rows, table, n1, n2, tabs):
    if n1 > 1:
        rows = _bmm3(*_split_bf16(tabs[table]), rows, 16)
    return _bmm3(*_split_bf16(tabs['fwd_b']), _swap_stage(rows, n1, n2), 8)


def _filter_spectrum(two_sided):
    length, c = two_sided.shape
    n1, n2 = _dft_split(length)
    tabs = _dft_tables(n1, n1, n2)
    rows = two_sided.reshape(n1, n2, c).transpose(1, 0, 2)
    if n1 == 1:
        rows = jnp.concatenate([rows, jnp.zeros_like(rows)], axis=1)
    return _dft_forward(rows, 'fwd_a_real', n1, n2, tabs)


def _long_conv_pair(z, spec):
    pair, n, c = z.shape
    assert pair == 2
    n1, n2 = _dft_split(2 * n)
    n1_io = max(n1 // 2, 1)
    tabs = _dft_tables(n1_io, n1, n2)
    if n1 > 1:
        rows = z.reshape(2, n1_io, n2, c).transpose(2, 0, 1, 3).reshape(n2, 2 * n1_io, c)
    else:
        rows = jnp.pad(z, ((0, 0), (0, n), (0, 0))).transpose(1, 0, 2)
    y = _dft_forward(rows, 'fwd_a', n1, n2, tabs)
    yr, yi = y[:, :n2], y[:, n2:]
    hr, hi = spec[:, :n2], spec[:, n2:]
    y = jnp.concatenate([yr * hr - yi * hi, yr * hi + yi * hr], axis=1)
    d = _bmm3(*_split_bf16(tabs['inv_b']), y, 8)
    d = _swap_stage(d, n2, n1)
    if n1 > 1:
        out = _bmm3(*_split_bf16(tabs['inv_a']), d, 16)
        return out.reshape(n2, 2, n1_io, c).transpose(1, 2, 0, 3).reshape(2, n, c)
    return (d.reshape(n2, 2, c).transpose(1, 0, 2) / (n1 * n2))[:, :n]


def _hyena_mixer(u, conv_w, conv_b, w1, b1, w2, b2, w3, freq, skip):
    n = u.shape[1]
    u = _depthwise_conv(u, conv_w, conv_b, HY_CONV // 2, HY_CONV - 1 - HY_CONV // 2)
    gates = (u[..., :HY_WIDTH], u[..., HY_WIDTH:2 * HY_WIDTH])
    z = u[..., 2 * HY_WIDTH:]
    two_sided = _hyena_filter(n, w1, b1, w2, b2, w3, freq)
    spec = _filter_spectrum(two_sided.reshape(2 * n, HY_ORDER * HY_WIDTH))
    for o in range(HY_ORDER):
        zc = _long_conv_pair(z, spec[..., o * HY_WIDTH:(o + 1) * HY_WIDTH])
        z = gates[o] * (zc + skip[o] * z)
    return z


LRU_CONV_MARGIN = SUBLANES
LRU_TIME_CHUNK = 1024


def _gelu_tanh(x):
    return 0.5 * x * (1.0 + jnp.tanh(math.sqrt(2.0 / math.pi) * (x + 0.044715 * (x * x * x))))


def _rglru_body(ug_ref, ur_ref, cw_ref, cb_ref, wa_ref, ba_ref, wx_ref, bx_ref, cl_ref, o_ref,
                pad_s, af_s, bf_s, ar_s, br_s, hf_s, hr_s, *, n_ctx, n_lat):
    a_s = (af_s, ar_s)
    b_s = (bf_s, br_s)
    mg = LRU_CONV_MARGIN
    cw = cw_ref[...]
    zeros_margin = jnp.zeros((mg, LANES), F32)
    for r0, n in ((0, n_ctx), (n_ctx, n_lat)):
        pad_s[0:mg, :] = zeros_margin
        pad_s[mg:mg + n, :] = ur_ref[0, r0:r0 + n, :]
        pad_s[mg + n:2 * mg + n, :] = zeros_margin
        chunk = min(n, LRU_TIME_CHUNK)
        for c0 in range(0, n, chunk):
            xr = cb_ref[...] + cw[0:1] * pad_s[mg + c0 - 2:mg + c0 - 2 + chunk, :]
            for j in range(1, LRU_CONV):
                xr = xr + cw[j:j + 1] * pad_s[mg + c0 + j - 2:mg + c0 + j - 2 + chunk, :]
            xb = xr.astype(BF16)
            for d in range(2):
                r = jax.nn.sigmoid(jnp.dot(xb, wa_ref[d, 0], preferred_element_type=F32) + ba_ref[d:d + 1, :])
                i = jax.nn.sigmoid(jnp.dot(xb, wx_ref[d, 0], preferred_element_type=F32) + bx_ref[d:d + 1, :])
                log_a = cl_ref[d:d + 1, :] * r
                a_s[d][r0 + c0:r0 + c0 + chunk, :] = jnp.exp(log_a)
                b_s[d][r0 + c0:r0 + c0 + chunk, :] = jnp.sqrt(1.0 - jnp.exp(2.0 * log_a)) * (i * xr)

    def scan_segment(r0, n, carry):
        def step(t, hc):
            hf, hr = hc
            tf = r0 + t
            tr = r0 + n - 1 - t
            hf = af_s[pl.ds(tf, 1), :] * hf + bf_s[pl.ds(tf, 1), :]
            hr = ar_s[pl.ds(tr, 1), :] * hr + br_s[pl.ds(tr, 1), :]
            hf_s[pl.ds(tf, 1), :] = hf
            hr_s[pl.ds(tr, 1), :] = hr
            return hf, hr
        return lax.fori_loop(0, n, step, carry, unroll=8)

    h0 = jnp.zeros((1, LANES), F32)
    carry = scan_segment(0, n_ctx, (h0, h0))
    scan_segment(n_ctx, n_lat, carry)
    o_ref[0] = ((hf_s[...] + hr_s[...]) * _gelu_tanh(ug_ref[0])).astype(o_ref.dtype)


def _block_diag_tiles(w):
    w = w.reshape(2, LRU_HEADS // 2, 2, LRU_HEAD_DIM, LRU_HEAD_DIM)
    z = jnp.zeros_like(w[:, :, 0])
    top = jnp.concatenate([w[:, :, 0], z], axis=-1)
    bot = jnp.concatenate([z, w[:, :, 1]], axis=-1)
    return jnp.concatenate([top, bot], axis=-2).astype(BF16)


def _rglru_mixer(u_all, n_ctx, conv_w, conv_b, wa, ba, wx, bx, lam):
    bsz, n_all, _ = u_all.shape
    n_lat = n_all - n_ctx
    n_tiles = LRU_WIDTH // LANES
    gate_blk0 = HY_IN // LANES
    rec_blk0 = (HY_IN + LRU_WIDTH) // LANES
    decay_scale = -LRU_C * jax.nn.softplus(-lam)
    seq = lambda col0: pl.BlockSpec((1, n_all, LANES), lambda b, j: (b, 0, col0 + j))
    vec = lambda rows: pl.BlockSpec((rows, LANES), lambda b, j: (0, j))
    mat = pl.BlockSpec((2, 1, LANES, LANES), lambda b, j: (0, j, 0, 0))
    full = pltpu.VMEM((n_all, LANES), F32)
    return pl.pallas_call(
        partial(_rglru_body, n_ctx=n_ctx, n_lat=n_lat),
        grid=(bsz, n_tiles),
        in_specs=[seq(gate_blk0), seq(rec_blk0), vec(LRU_CONV), vec(1), mat, vec(2), mat, vec(2), vec(2)],
        out_specs=pl.BlockSpec((1, n_all, LANES), lambda b, j: (b, 0, j)),
        out_shape=jax.ShapeDtypeStruct((bsz, n_all, LRU_WIDTH), BF16),
        scratch_shapes=[pltpu.VMEM((n_lat + 2 * LRU_CONV_MARGIN, LANES), F32),
                        full, full, full, full, full, full],
        compiler_params=pltpu.CompilerParams(
            dimension_semantics=("parallel", "parallel"),
            vmem_limit_bytes=VMEM_LIMIT_BYTES),
        name="rglru",
    )(u_all, u_all, conv_w, conv_b.reshape(1, LRU_WIDTH), _block_diag_tiles(wa), ba,
      _block_diag_tiles(wx), bx, decay_scale)


ATT_Q_TILE = 256
ATT_KV_CHUNK = 2048


def _attn_body(q_ref, k_ref, v_ref, lv_ref, g_ref, o_ref, *, lam_init):
    lv = lv_ref[...]
    lam = (jnp.exp(jnp.sum(lv[0:1] * lv[1:2], keepdims=True))
           - jnp.exp(jnp.sum(lv[2:3] * lv[3:4], keepdims=True)) + lam_init)
    q = q_ref[0]
    nk = k_ref.shape[1]
    work = [(m, c0, min(c0 + ATT_KV_CHUNK, nk)) for c0 in range(0, nk, ATT_KV_CHUNK) for m in range(2)]

    def scores(m, c0, c1):
        qm = q[:, m * ATT_HEAD_DIM:(m + 1) * ATT_HEAD_DIM]
        km = k_ref[0, c0:c1, m * ATT_HEAD_DIM:(m + 1) * ATT_HEAD_DIM]
        return lax.dot_general(qm, km, (((1,), (1,)), ((), ())), preferred_element_type=F32)

    state = [None, None]
    s_next = scores(*work[0])
    for j, (m, c0, c1) in enumerate(work):
        s = s_next
        if j + 1 < len(work):
            s_next = scores(*work[j + 1])
        chunk_max = jnp.max(s, axis=-1, keepdims=True)
        v = v_ref[0, c0:c1, :]
        if state[m] is None:
            p = jnp.exp(s - chunk_max)
            state[m] = (chunk_max, jnp.sum(p, axis=-1, keepdims=True),
                        jnp.dot(p.astype(BF16), v, preferred_element_type=F32))
        else:
            mx_old, sum_old, acc_old = state[m]
            mx = jnp.maximum(mx_old, chunk_max)
            rescale = jnp.exp(mx_old - mx)
            p = jnp.exp(s - mx)
            state[m] = (mx, rescale * sum_old + jnp.sum(p, axis=-1, keepdims=True),
                        rescale * acc_old + jnp.dot(p.astype(BF16), v, preferred_element_type=F32))
    o = state[0][2] / state[0][1] - lam * (state[1][2] / state[1][1])
    y = o * lax.rsqrt(jnp.mean(o * o, axis=-1, keepdims=True) + LN_EPS) * g_ref[...] * (1.0 - lam_init)
    o_ref[0] = y.astype(o_ref.dtype)


def _diff_attention(qkv, lam_vecs, gain, lam_init, q_row0, n_q, n_k):
    bsz = qkv.shape[0]
    tq = ATT_Q_TILE
    q_blk0 = q_row0 // tq
    n_heads = ATT_HEADS
    return pl.pallas_call(
        partial(_attn_body, lam_init=lam_init),
        grid=(bsz, n_heads, n_q // tq),
        in_specs=[
            pl.BlockSpec((1, tq, ATT_V_DIM), lambda b, h, i: (b, q_blk0 + i, h)),
            pl.BlockSpec((1, n_k, ATT_V_DIM), lambda b, h, i: (b, 0, n_heads + h)),
            pl.BlockSpec((1, n_k, ATT_V_DIM), lambda b, h, i: (b, 0, 2 * n_heads + h)),
            pl.BlockSpec((4, ATT_HEAD_DIM), lambda b, h, i: (0, 0)),
            pl.BlockSpec((1, ATT_V_DIM), lambda b, h, i: (0, 0)),
        ],
        out_specs=pl.BlockSpec((1, tq, ATT_V_DIM), lambda b, h, i: (b, i, h)),
        out_shape=jax.ShapeDtypeStruct((bsz, n_q, ATT_WIDTH), BF16),
        compiler_params=pltpu.CompilerParams(
            dimension_semantics=("parallel", "parallel", "arbitrary"),
            vmem_limit_bytes=VMEM_LIMIT_BYTES),
        name="diff_attention",
    )(qkv, qkv, qkv, lam_vecs, gain.reshape(1, ATT_V_DIM))


def _axial_rope_tables(n):
    rows = n // GRID_W
    row = jnp.broadcast_to(jnp.arange(rows, dtype=F32)[:, None], (rows, GRID_W)).reshape(-1)
    col = jnp.broadcast_to(jnp.arange(GRID_W, dtype=F32)[None, :], (rows, GRID_W)).reshape(-1)
    inv = ROPE_THETA ** (-jnp.arange(0, ROPE_AXIS_DIM, 2, dtype=F32) / ROPE_AXIS_DIM)
    ang = jnp.stack([row[:, None] * inv, col[:, None] * inv], axis=1)
    return jnp.cos(ang), jnp.sin(ang)


def _apply_axial_rope(x, cos, sin):
    xs = x.reshape(x.shape[:-1] + (2, 2, ROPE_AXIS_DIM // 2))
    x1, x2 = xs[..., 0, :], xs[..., 1, :]
    cb, sb = cos[None, :, None, None], sin[None, :, None, None]
    out = jnp.stack([x1 * cb - x2 * sb, x2 * cb + x1 * sb], axis=-2)
    return out.reshape(x.shape)


def _diff_attention_mixer(u_att, n_ctx, lam_vecs, subln_gain, lam_init, need_ctx):
    bsz, n_all, _ = u_att.shape
    n_lat = n_all - n_ctx
    scale = ATT_HEAD_DIM ** -0.5
    u_lat = u_att[:, n_ctx:]
    u_ctx = u_att[:, :n_ctx]
    q_l = u_lat[..., :ATT_QK].reshape(bsz, n_lat, ATT_HEADS, 2, ATT_HEAD_DIM)
    k_l = u_lat[..., ATT_QK:2 * ATT_QK].reshape(bsz, n_lat, ATT_HEADS, 2, ATT_HEAD_DIM)
    cos, sin = _axial_rope_tables(n_lat)
    q_l = _apply_axial_rope(q_l, cos, sin) * scale
    k_l = _apply_axial_rope(k_l, cos, sin)
    lat = jnp.concatenate([q_l.reshape(bsz, n_lat, ATT_QK), k_l.reshape(bsz, n_lat, ATT_QK),
                           u_lat[..., 2 * ATT_QK:]], axis=-1).astype(BF16)
    cx = jnp.concatenate([u_ctx[..., :ATT_QK] * scale, u_ctx[..., ATT_QK:]], axis=-1).astype(BF16)
    qkv = jnp.concatenate([cx, lat], axis=1)
    y_lat = _diff_attention(qkv, lam_vecs, subln_gain, lam_init, n_ctx, n_lat, n_all)
    y_ctx = _diff_attention(qkv, lam_vecs, subln_gain, lam_init, 0, n_ctx, n_ctx) if need_ctx else None
    return y_lat, y_ctx


ROW_TILE = 256
IN_PROJ_COL_TILE = 1408
ADA_COL_TILE = 1536
COMBINE_ROW_TILE = 128


def _ln_rows(x):
    mu = jnp.mean(x, axis=-1, keepdims=True)
    xc = x - mu
    return xc * lax.rsqrt(jnp.mean(xc * xc, axis=-1, keepdims=True) + LN_EPS)


def _segments(bsz, seg_rows, tile):
    ids = []
    for b in range(bsz):
        for rows, kind in seg_rows:
            assert rows % tile == 0
            ids += [bsz if kind == 'ctx' else b] * (rows // tile)
    return jnp.asarray(ids, jnp.int32)


def _ada_body(c_ref, w_ref, b_ref, o_ref):
    c = c_ref[...]
    w = w_ref[0]
    w_hi = w.astype(BF16)
    w_lo = (w - w_hi.astype(F32)).astype(BF16)
    o_ref[...] = _split_dot(c * jax.nn.sigmoid(c), w_hi, w_lo) + b_ref[0]


def _ada_modulation(cond, ada_w, ada_b, layer):
    s, d = cond.shape
    n = ada_w.shape[-1]
    rows = -(-s // SUBLANES) * SUBLANES
    tn = ADA_COL_TILE
    assert n % tn == 0
    out = pl.pallas_call(
        _ada_body, grid=(n // tn,),
        in_specs=[pl.BlockSpec((rows, d), lambda j: (0, 0)),
                  pl.BlockSpec((1, d, tn), lambda j: (layer, 0, j)),
                  pl.BlockSpec((1, 1, tn), lambda j: (layer, 0, j))],
        out_specs=pl.BlockSpec((rows, tn), lambda j: (0, j)),
        out_shape=jax.ShapeDtypeStruct((rows, n), F32),
        compiler_params=pltpu.CompilerParams(dimension_semantics=("parallel",),
                                             vmem_limit_bytes=VMEM_LIMIT_BYTES),
        name="ada_modulation",
    )(jnp.pad(cond, ((0, rows - s), (0, 0))), ada_w, ada_b.reshape(ada_b.shape[0], 1, n))
    return out[:s]


def _ln_mm_body(seg_ref, x_ref, mod_ref, w_ref, o_ref, wb_s):
    @pl.when(pl.program_id(1) == 0)
    def _():
        wb_s[...] = w_ref[0].astype(BF16)

    m = mod_ref[0]
    h = _ln_rows(x_ref[...]) * (1.0 + m[1:2]) + m[0:1]
    o_ref[...] = jnp.dot(h.astype(BF16), wb_s[...], preferred_element_type=F32)


def _ln_mod_matmul(x, mod, seg, w, layer):
    t, d = x.shape
    n = w.shape[-1]
    tm, tn = ROW_TILE, IN_PROJ_COL_TILE
    assert t % tm == 0 and n % tn == 0
    grid_spec = pltpu.PrefetchScalarGridSpec(
        num_scalar_prefetch=1,
        grid=(n // tn, t // tm),
        in_specs=[pl.BlockSpec((tm, d), lambda j, i, s: (i, 0)),
                  pl.BlockSpec((1, 2, d), lambda j, i, s: (s[i], 0, 0)),
                  pl.BlockSpec((1, d, tn), lambda j, i, s: (layer, 0, j))],
        out_specs=pl.BlockSpec((tm, tn), lambda j, i, s: (i, j)),
        scratch_shapes=[pltpu.VMEM((d, tn), BF16)],
    )
    return pl.pallas_call(
        _ln_mm_body, grid_spec=grid_spec,
        out_shape=jax.ShapeDtypeStruct((t, n), F32),
        compiler_params=pltpu.CompilerParams(dimension_semantics=("arbitrary", "arbitrary"),
                                             vmem_limit_bytes=VMEM_LIMIT_BYTES),
        name="ln_mod_in_proj",
    )(seg, x, mod, w)


def _split_dot(a, b_hi, b_lo):
    a_hi = a.astype(BF16)
    a_lo = (a - a_hi.astype(F32)).astype(BF16)
    return (jnp.dot(a_hi, b_hi, preferred_element_type=F32) + jnp.dot(a_hi, b_lo, preferred_element_type=F32)
            + jnp.dot(a_lo, b_hi, preferred_element_type=F32))


def _out_ln_body(seg_ref, mixed_ref, x_ref, mod_ref, w_ref, lng_ref, rwh_ref, rwl_ref, rb_ref,
                 x1_ref, h2_ref, h2b_ref, ti_ref, tw_ref):
    m = mod_ref[0]
    mix = jnp.dot(mixed_ref[...], w_ref[...], preferred_element_type=F32)
    x1 = _ln_rows(DEEPNORM_ALPHA * x_ref[...] + m[0:1] * mix) * lng_ref[0:1] + lng_ref[1:2]
    x1_ref[...] = x1
    h2 = _ln_rows(x1) * (1.0 + m[2:3]) + m[1:2]
    h2_ref[...] = h2
    h2b_ref[...] = h2.astype(BF16)
    scores = jax.nn.sigmoid(_split_dot(h2, rwh_ref[...], rwl_ref[...]))
    lane = lax.broadcasted_iota(jnp.int32, scores.shape, 1).astype(F32)
    biased = jnp.where(lane < N_EXPERTS, scores + rb_ref[...], -jnp.inf)
    top_i = jnp.zeros(scores.shape, F32)
    top_s = jnp.zeros(scores.shape, F32)
    for k in range(TOP_K):
        best = jnp.max(biased, axis=-1, keepdims=True)
        pick = jnp.min(jnp.where(biased == best, lane, float(LANES)), axis=-1, keepdims=True)
        hit = lane == pick
        top_i = jnp.where(lane == k, pick, top_i)
        top_s = jnp.where(lane == k, jnp.sum(jnp.where(hit, scores, 0.0), axis=-1, keepdims=True), top_s)
        biased = jnp.where(hit, -jnp.inf, biased)
    ti_ref[...] = top_i.astype(jnp.int32)
    tw_ref[...] = top_s / jnp.sum(top_s, axis=-1, keepdims=True) * ROUTED_SCALE


def _out_proj_norm_route(mixed, x, mod, seg, w, ln_gb, router_w, router_b):
    t, d = x.shape
    tm = ROW_TILE
    rw = jnp.pad(router_w, ((0, 0), (0, LANES - N_EXPERTS)))
    rw_hi, rw_lo = _split_bf16(rw)
    rb = jnp.pad(router_b, (0, LANES - N_EXPERTS)).reshape(1, LANES)
    row = lambda width: pl.BlockSpec((tm, width), lambda i, s: (i, 0))
    const = lambda shape: pl.BlockSpec(shape, lambda i, s: (0,) * len(shape))
    grid_spec = pltpu.PrefetchScalarGridSpec(
        num_scalar_prefetch=1,
        grid=(t // tm,),
        in_specs=[row(d), row(d), pl.BlockSpec((1, 3, d), lambda i, s: (s[i], 0, 0)), const((d, d)),
                  const((2, d)), const((d, LANES)), const((d, LANES)), const((1, LANES))],
        out_specs=[row(d), row(d), row(d), row(LANES), row(LANES)],
    )
    x1, h2, h2b, top_i, top_w = pl.pallas_call(
        _out_ln_body, grid_spec=grid_spec,
        out_shape=[jax.ShapeDtypeStruct((t, d), F32), jax.ShapeDtypeStruct((t, d), F32),
                   jax.ShapeDtypeStruct((t, d), BF16), jax.ShapeDtypeStruct((t, LANES), jnp.int32),
                   jax.ShapeDtypeStruct((t, LANES), F32)],
        compiler_params=pltpu.CompilerParams(dimension_semantics=("parallel",),
                                             vmem_limit_bytes=VMEM_LIMIT_BYTES),
        name="out_proj_norm_route",
    )(seg, mixed, x, mod, w, ln_gb, rw_hi, rw_lo, rb)
    return x1, h2, h2b, top_i[:, :TOP_K], top_w[:, :TOP_K]


def _shared_body(h_ref, wgu_ref, wd_ref, o_ref):
    gu = jnp.dot(h_ref[...], wgu_ref[...], preferred_element_type=F32)
    f = wd_ref.shape[0]
    g, u = gu[:, :f], gu[:, f:]
    act = (g * jax.nn.sigmoid(g) * u).astype(BF16)
    o_ref[...] = jnp.dot(act, wd_ref[...], preferred_element_type=F32)


def _shared_expert(hb, w_gate, w_up, w_down):
    t, d = hb.shape
    f = w_gate.shape[1]
    tm = ROW_TILE
    wgu = jnp.concatenate([w_gate, w_up], axis=1).astype(BF16)
    return pl.pallas_call(
        _shared_body, grid=(t // tm,),
        in_specs=[pl.BlockSpec((tm, d), lambda i: (i, 0)), pl.BlockSpec((d, 2 * f), lambda i: (0, 0)),
                  pl.BlockSpec((f, d), lambda i: (0, 0))],
        out_specs=pl.BlockSpec((tm, d), lambda i: (i, 0)),
        out_shape=jax.ShapeDtypeStruct((t, d), F32),
        compiler_params=pltpu.CompilerParams(dimension_semantics=("parallel",),
                                             vmem_limit_bytes=VMEM_LIMIT_BYTES),
        name="shared_expert",
    )(hb, wgu, w_down.astype(BF16))


def _combine_body(seg_ref, *refs):
    y_refs, (sh_ref, x_ref, mod_ref, lng_ref, o_ref) = refs[:TOP_K], refs[TOP_K:]
    ffn = sh_ref[...]
    for y_ref in y_refs:
        ffn = ffn + y_ref[...]
    o_ref[...] = _ln_rows(DEEPNORM_ALPHA * x_ref[...] + mod_ref[0] * ffn) * lng_ref[0:1] + lng_ref[1:2]


def _combine_norm(y, shared, x1, gate, seg, ln_gb):
    t, d = x1.shape
    tm = COMBINE_ROW_TILE
    tiles = t // tm
    row = pl.BlockSpec((tm, d), lambda i, s: (i, 0))
    y_specs = [pl.BlockSpec((tm, d), lambda i, s, k=k: (k * tiles + i, 0)) for k in range(TOP_K)]
    grid_spec = pltpu.PrefetchScalarGridSpec(
        num_scalar_prefetch=1,
        grid=(tiles,),
        in_specs=y_specs + [row, row, pl.BlockSpec((1, 1, d), lambda i, s: (s[i], 0, 0)),
                            pl.BlockSpec((2, d), lambda i, s: (0, 0))],
        out_specs=row,
    )
    return pl.pallas_call(
        _combine_body, grid_spec=grid_spec,
        out_shape=jax.ShapeDtypeStruct((t, d), F32),
        compiler_params=pltpu.CompilerParams(dimension_semantics=("parallel",),
                                             vmem_limit_bytes=VMEM_LIMIT_BYTES),
        name="combine_norm",
    )(seg, *([y] * TOP_K), shared, x1, gate, ln_gb)


MOE_BLOCK_ROWS = 256
MOE_VMEM_LIMIT_BYTES = 56 * 1024 * 1024
MOE_OUTBOUND_DMA_PRIORITY = 1
MOE_WEIGHT_DMA_PRIORITY = 1


def _moe_body(be_ref, nu_ref, ne_ref, idx_ref, idx_nxt_ref, rw_ref, wg_hbm, wu_hbm, wd_hbm, h_hbm, o_hbm,
              xbuf, ybuf, wg_f, wu_f, wd_f, wg_s, wu_s, wd_s, gsem, ssem, wsem, *, bm, layer):
    i = pl.program_id(0)
    n_used = nu_ref[0]
    slot = i % 2

    def weight_copies(expert):
        return (pltpu.make_async_copy(wg_hbm.at[layer, expert], wg_f, wsem.at[0]),
                pltpu.make_async_copy(wu_hbm.at[layer, expert], wu_f, wsem.at[1]),
                pltpu.make_async_copy(wd_hbm.at[layer, expert], wd_f, wsem.at[2]))

    def gather_rows(tok_ref, dst_slot):
        for r in range(bm):
            pltpu.make_async_copy(h_hbm.at[pl.ds(tok_ref[0, 0, r], 1)], xbuf.at[dst_slot, pl.ds(r, 1)],
                                  gsem.at[dst_slot]).start()

    def wait_gather(s):
        pltpu.make_async_copy(h_hbm.at[pl.ds(0, bm)], xbuf.at[s], gsem.at[s]).wait()

    def wait_scatter(s):
        pltpu.make_async_copy(ybuf.at[s], o_hbm.at[pl.ds(0, bm)], ssem.at[s]).wait()

    e = be_ref[i]
    e_prev = be_ref[jnp.maximum(i - 1, 0)]

    @pl.when(i == 0)
    def _():
        gather_rows(idx_ref, 0)
        for copy in weight_copies(e):
            copy.start(priority=MOE_WEIGHT_DMA_PRIORITY)

    @pl.when((i < n_used) & ((i == 0) | (e != e_prev)))
    def _():
        for copy in weight_copies(e):
            copy.wait()
        wg_s[...] = wg_f[...].astype(BF16)
        wu_s[...] = wu_f[...].astype(BF16)
        wd_s[...] = wd_f[...].astype(BF16)

        @pl.when(ne_ref[i] >= 0)
        def _():
            for copy in weight_copies(ne_ref[i]):
                copy.start(priority=MOE_WEIGHT_DMA_PRIORITY)

    def block_step(s):
        wait_gather(s)

        @pl.when(i >= 2)
        def _():
            wait_scatter(s)

        gather_rows(idx_nxt_ref, 1 - s)

        xb = xbuf[s].astype(BF16)
        g = jnp.dot(xb, wg_s[...], preferred_element_type=F32)
        u = jnp.dot(xb, wu_s[...], preferred_element_type=F32)
        act = (g * jax.nn.sigmoid(g) * u).astype(BF16)
        ybuf[s] = jnp.dot(act, wd_s[...], preferred_element_type=F32) * rw_ref[...]
        for r in range(bm):
            pltpu.make_async_copy(ybuf.at[s, pl.ds(r, 1)], o_hbm.at[pl.ds(idx_ref[0, 0, bm + r], 1)],
                                  ssem.at[s]).start(priority=MOE_OUTBOUND_DMA_PRIORITY)

        @pl.when(i == n_used - 1)
        def _():
            @pl.when(i >= 1)
            def _():
                wait_scatter(1 - s)
            wait_scatter(s)
            wait_gather(1 - s)

    for s in range(2):
        pl.when((i < n_used) & (slot == s))(partial(block_step, s))

    @pl.when(i >= n_used)
    def _():
        xbuf[0] = jnp.zeros(xbuf.shape[1:], F32)
        fill = pltpu.make_async_copy(xbuf.at[0], o_hbm.at[pl.ds(pl.multiple_of(i * bm, bm), bm)], gsem.at[0])
        fill.start()
        fill.wait()


def _routed_experts(h, idx, row_w, block_e, n_used, next_e, w_gate, w_up, w_down, layer):
    bm = MOE_BLOCK_ROWS
    n_blocks = idx.shape[0]
    d = h.shape[1]
    f = w_gate.shape[-1]
    anywhere = pl.BlockSpec(memory_space=pl.ANY)
    grid_spec = pltpu.PrefetchScalarGridSpec(
        num_scalar_prefetch=3,
        grid=(n_blocks,),
        in_specs=[
            pl.BlockSpec((1, 1, 2 * bm), lambda i, be, nu, ne: (i, 0, 0), memory_space=pltpu.SMEM),
            pl.BlockSpec((1, 1, 2 * bm), lambda i, be, nu, ne: (jnp.minimum(i + 1, n_blocks - 1), 0, 0),
                         memory_space=pltpu.SMEM),
            pl.BlockSpec((bm, 1), lambda i, be, nu, ne: (i, 0)),
            anywhere, anywhere, anywhere, anywhere,
        ],
        out_specs=anywhere,
        scratch_shapes=[pltpu.VMEM((2, bm, d), F32), pltpu.VMEM((2, bm, d), F32),
                        pltpu.VMEM((d, f), F32), pltpu.VMEM((d, f), F32), pltpu.VMEM((f, d), F32),
                        pltpu.VMEM((d, f), BF16), pltpu.VMEM((d, f), BF16), pltpu.VMEM((f, d), BF16),
                        pltpu.SemaphoreType.DMA((2,)), pltpu.SemaphoreType.DMA((2,)),
                        pltpu.SemaphoreType.DMA((3,))],
    )
    return pl.pallas_call(
        partial(_moe_body, bm=bm, layer=layer),
        grid_spec=grid_spec,
        out_shape=jax.ShapeDtypeStruct((n_blocks * bm, d), F32),
        compiler_params=pltpu.CompilerParams(
            dimension_semantics=("arbitrary",),
            vmem_limit_bytes=MOE_VMEM_LIMIT_BYTES),
        name="routed_experts",
    )(block_e, n_used, next_e, idx, idx, row_w, w_gate, w_up, w_down, h)


def _route(top_idx, top_w):
    bm = MOE_BLOCK_ROWS
    n_tok = top_idx.shape[0]
    n_assign = n_tok * TOP_K
    flat_e = top_idx.reshape(-1)
    flat_w = top_w.reshape(-1)
    order = jnp.argsort(flat_e).astype(jnp.int32)
    counts = jnp.sum(flat_e[None, :] == jnp.arange(N_EXPERTS)[:, None], axis=1).astype(jnp.int32)
    padded = (counts + bm - 1) // bm * bm
    padded_end = jnp.cumsum(padded)
    live_end = jnp.cumsum(counts)
    shift = padded_end - padded - (live_end - counts)
    n_blocks = -(-n_assign // bm) + N_EXPERTS
    block_row0 = jnp.arange(n_blocks, dtype=jnp.int32) * bm
    block_e = jnp.minimum(jnp.sum(padded_end[None, :] <= block_row0[:, None], axis=1),
                          N_EXPERTS - 1).astype(jnp.int32)
    n_used = (padded_end[-1:] // bm).astype(jnp.int32)
    experts = jnp.arange(N_EXPERTS, dtype=jnp.int32)
    later = jnp.where((experts[None, :] > experts[:, None]) & (counts[None, :] > 0), experts[None, :], N_EXPERTS)
    next_live = jnp.min(later, axis=1)
    next_e = jnp.where(next_live < N_EXPERTS, next_live, -1)[block_e].astype(jnp.int32)
    rows = jnp.arange(n_blocks * bm, dtype=jnp.int32).reshape(n_blocks, bm)
    row_slot = rows - shift[block_e][:, None]
    row_live = row_slot < live_end[block_e][:, None]
    row_assign = order[jnp.where(row_live, row_slot, 0)]
    row_tok = jnp.where(row_live, row_assign // TOP_K, 0)
    row_dst = jnp.where(row_live, (row_assign % TOP_K) * n_tok + row_tok,
                        n_assign + rows - live_end[block_e][:, None])
    row_w = jnp.where(row_live, flat_w[row_assign], 0.0)
    idx = jnp.concatenate([row_tok, row_dst], axis=1).astype(jnp.int32)[:, None, :]
    return idx, row_w.reshape(-1, 1), block_e, n_used, next_e


def kernel(x, c, ctx, c_ctx, ada_w, ada_b, w_in, hy_conv_w, hy_conv_b, hy_w1, hy_b1, hy_w2, hy_b2, hy_w3, hy_freq, hy_skip, lru_conv_w, lru_conv_b, lru_wa, lru_ba, lru_wx, lru_bx, lru_lambda, att_lambda, att_subln, w_out, ln_g, ln_b, router_w, router_b, exp_w_gate, exp_w_up, exp_w_down, sh_w_gate, sh_w_up, sh_w_down):
    bsz, n_lat, d = x.shape
    n_ctx = ctx.shape[1]
    n_all = n_ctx + n_lat
    x_all = jnp.concatenate([ctx, x], axis=1).reshape(bsz * n_all, d)
    both = ((n_ctx, 'ctx'), (n_lat, 'lat'))
    seg_all = _segments(bsz, both, ROW_TILE)
    x_out = None

    for l in range(DEPTH):
        need_ctx = l < DEPTH - 1
        lam_init = 0.8 - 0.6 * math.exp(-0.3 * l)
        cond = jnp.concatenate([c, c_ctx[None, :]], axis=0)
        mod = _ada_modulation(cond, ada_w, ada_b, l)
        sh1, sc1, g1, sh2, sc2, g2 = jnp.split(mod, 6, axis=-1)
        hy_args = (hy_conv_w[l], hy_conv_b[l], hy_w1[l], hy_b1[l], hy_w2[l], hy_b2[l],
                   hy_w3[l], hy_freq[l], hy_skip[l])

        u_all = _ln_mod_matmul(x_all, jnp.stack([sh1, sc1], axis=1), seg_all, w_in, l).reshape(bsz, n_all, -1)
        y_hy_l = _hyena_mixer(u_all[:, n_ctx:, :HY_IN], *hy_args)
        y_lru = _rglru_mixer(u_all, n_ctx, lru_conv_w[l], lru_conv_b[l], lru_wa[l], lru_ba[l],
                             lru_wx[l], lru_bx[l], lru_lambda[l])
        y_att_l, y_att_c = _diff_attention_mixer(u_all[..., HY_IN + LRU_IN:], n_ctx, att_lambda[l],
                                                 att_subln[l], lam_init, need_ctx)
        if need_ctx:
            y_hy = jnp.concatenate([_hyena_mixer(u_all[:, :n_ctx, :HY_IN], *hy_args), y_hy_l], axis=1)
            mixed = jnp.concatenate([y_hy.astype(BF16), y_lru, jnp.concatenate([y_att_c, y_att_l], axis=1)], axis=-1)
            x_res, seg, seg_c = x_all, seg_all, _segments(bsz, both, COMBINE_ROW_TILE)
        else:
            mixed = jnp.concatenate([y_hy_l.astype(BF16), y_lru[:, n_ctx:], y_att_l], axis=-1)
            x_res = x_all.reshape(bsz, n_all, d)[:, n_ctx:].reshape(bsz * n_lat, d)
            seg = _segments(bsz, ((n_lat, 'lat'),), ROW_TILE)
            seg_c = _segments(bsz, ((n_lat, 'lat'),), COMBINE_ROW_TILE)
        n_tok = x_res.shape[0]
        ln_gb1 = jnp.stack([ln_g[l, 0], ln_b[l, 0]])
        ln_gb2 = jnp.stack([ln_g[l, 1], ln_b[l, 1]])
        x1, h2, h2b, top_idx, top_w = _out_proj_norm_route(
            mixed.reshape(n_tok, d), x_res, jnp.stack([g1, sh2, sc2], axis=1), seg,
            w_out[l].astype(BF16), ln_gb1, router_w[l], router_b[l])

        idx, row_w, block_e, n_used, next_e = _route(top_idx, top_w)
        y = _routed_experts(h2, idx, row_w, block_e, n_used, next_e, exp_w_gate, exp_w_up, exp_w_down, l)
        shared = _shared_expert(h2b, sh_w_gate[l], sh_w_up[l], sh_w_down[l])
        x_out = _combine_norm(y, shared, x1, g2[:, None, :], seg_c, ln_gb2)
        x_all = x_out
    return x_out.reshape(bsz, n_lat, d)
```

```python
import math
from functools import lru_cache, partial

import numpy as np
import jax
import jax.numpy as jnp
from jax import lax
from jax.experimental import pallas as pl
from jax.experimental.pallas import tpu as pltpu

D_MODEL = 2048
DEPTH = 2
GRID_W = 64
F32 = jnp.float32
BF16 = jnp.bfloat16
LANES = 128
SUBLANES = 8

HY_WIDTH = D_MODEL // 4
LRU_WIDTH = D_MODEL // 4
ATT_WIDTH = D_MODEL // 2
HY_ORDER = 2
HY_CONV = 3
HY_BANDS = 8
HY_TARGET = 1e-2
HY_MAX_DECAY = math.log(HY_TARGET) / 0.3
HY_MIN_DECAY = math.log(HY_TARGET) / 1.5
HY_IN = (HY_ORDER + 1) * HY_WIDTH

LRU_HEADS = 8
LRU_HEAD_DIM = LRU_WIDTH // LRU_HEADS
LRU_CONV = 4
LRU_C = 8.0
LRU_IN = 2 * LRU_WIDTH

ATT_HEADS = 4
ATT_V_DIM = ATT_WIDTH // ATT_HEADS
ATT_HEAD_DIM = ATT_V_DIM // 2
ATT_QK = ATT_HEADS * 2 * ATT_HEAD_DIM
ROPE_THETA = 10000.0
ROPE_AXIS_DIM = ATT_HEAD_DIM // 2

N_EXPERTS = 64
TOP_K = 8
ROUTED_SCALE = 2.5

LN_EPS = 1e-5
DEEPNORM_ALPHA = (2 * DEPTH) ** 0.25

VMEM_LIMIT_BYTES = 48 * 1024 * 1024


def _depthwise_conv(x, w, b, pad_left, pad_right):
    y = lax.conv_general_dilated(
        x, w[:, None, :], window_strides=(1,),
        padding=[(pad_left, pad_right)], dimension_numbers=('NWC', 'WIO', 'NWC'),
        feature_group_count=x.shape[-1])
    return y + b


def _hyena_filter(n, w1, b1, w2, b2, w3, freq):
    pos = jnp.arange(n, dtype=F32)
    t = (pos / max(n - 1, 1))[:, None]
    bands = jnp.linspace(1e-4, HY_BANDS - 1, HY_BANDS, dtype=F32)
    ang = (2.0 * math.pi / n) * pos[:, None] * bands
    feats = jnp.concatenate([t, jnp.cos(ang), -jnp.sin(ang)], axis=-1)
    hp = lax.Precision.HIGHEST
    decay = jnp.abs(jnp.linspace(HY_MIN_DECAY, HY_MAX_DECAY, HY_WIDTH, dtype=F32))
    side_cols = HY_ORDER * HY_WIDTH

    def one_side(feats, t, w3_side):
        hid = jnp.sin(freq * (jnp.dot(feats, w1, precision=hp) + b1))
        hid = jnp.sin(freq * (jnp.dot(hid, w2, precision=hp) + b2))
        filt = jnp.dot(hid, w3_side, precision=hp).reshape(n, HY_ORDER, HY_WIDTH)
        return filt * jnp.exp(-t * decay)[:, None, :]

    fwd = one_side(feats, t, w3[:, :side_cols])
    bwd_flipped = one_side(feats[::-1], t[::-1], w3[:, side_cols:])
    two_sided = jnp.concatenate([fwd, jnp.zeros_like(fwd[:1]), bwd_flipped[:n - 1]], axis=0)
    return two_sided / jnp.sum(jnp.abs(two_sided), axis=0, keepdims=True)


DFT_BLOCK_BYTES = 4 * 1024 * 1024
DFT_MINOR_LEN = 128
DFT_SINGLE_LEVEL_MAX = 1024


def _bmm3_body(lh_ref, ll_ref, r_ref, o_ref, *, nb, shared):
    for b in range(nb):
        r = r_ref[b]
        r_hi = r.astype(BF16)
        r_lo = (r - r_hi.astype(F32)).astype(BF16)
        lh = lh_ref[0 if shared else b]
        ll = ll_ref[0 if shared else b]
        o_ref[b] = (jnp.dot(lh, r_hi, preferred_element_type=F32)
                    + jnp.dot(lh, r_lo, preferred_element_type=F32)
                    + jnp.dot(ll, r_hi, preferred_element_type=F32))


def _bmm3(l_hi, l_lo, r, nb):
    batch, k, c = r.shape
    m = l_hi.shape[1]
    shared = l_hi.shape[0] == 1
    nb = max(1, min(nb, batch, DFT_BLOCK_BYTES // (4 * c * max(k, m))))
    assert batch % nb == 0
    l_spec = pl.BlockSpec((1 if shared else nb, m, k), (lambda i: (0, 0, 0)) if shared else (lambda i: (i, 0, 0)))
    return pl.pallas_call(
        partial(_bmm3_body, nb=nb, shared=shared),
        grid=(batch // nb,),
        in_specs=[l_spec, l_spec, pl.BlockSpec((nb, k, c), lambda i: (i, 0, 0))],
        out_specs=pl.BlockSpec((nb, m, c), lambda i: (i, 0, 0)),
        out_shape=jax.ShapeDtypeStruct((batch, m, c), F32),
        compiler_params=pltpu.CompilerParams(dimension_semantics=("parallel",), vmem_limit_bytes=VMEM_LIMIT_BYTES),
        name="dft_matmul",
    )(l_hi, l_lo, r)


def _stack_complex(f):
    return np.concatenate([np.concatenate([f.real, -f.imag], axis=-1),
                           np.concatenate([f.imag, f.real], axis=-1)], axis=-2)


def _split_bf16(a):
    a = jnp.asarray(a, F32)
    hi = a.astype(BF16)
    return hi, (a - hi.astype(F32)).astype(BF16)


@lru_cache(maxsize=None)
def _dft_tables(n1_io, n1, n2):
    n = n1 * n2
    k1 = np.arange(n1)
    a = np.arange(n1_io)
    b = np.arange(n2)
    ang = -2.0 * np.pi * (((np.outer(k1, a) * n2)[None] + b[:, None, None] * k1[None, :, None]) % n) / n
    fa = np.exp(1j * ang)
    fb = np.exp(-2j * np.pi * (np.outer(b, b) % n2) / n2)
    return dict(
        fwd_a=_stack_complex(fa),
        fwd_a_real=np.concatenate([fa.real, fa.imag], axis=-2),
        fwd_b=_stack_complex(fb)[None],
        inv_b=_stack_complex(np.conj(fb).T)[None],
        inv_a=_stack_complex(np.conj(fa).transpose(0, 2, 1) / n),
    )


def _dft_split(length):
    if length <= DFT_SINGLE_LEVEL_MAX:
        return 1, length
    return length // DFT_MINOR_LEN, DFT_MINOR_LEN


def _swap_stage(x, n1, n2):
    c = x.shape[-1]
    return x.reshape(n2, 2, n1, c).transpose(2, 1, 0, 3).reshape(n1, 2 * n2, c)


def _dft_forward(rows, table, n1, n2, tabs):
    if n1 > 1:
        rows = _bmm3(*_split_bf16(tabs[table]), rows, 16)
    return _bmm3(*_split_bf16(tabs['fwd_b']), _swap_stage(rows, n1, n2), 8)


def _filter_spectrum(two_sided):
    length, c = two_sided.shape
    n1, n2 = _dft_split(length)
    tabs = _dft_tables(n1, n1, n2)
    rows = two_sided.reshape(n1, n2, c).transpose(1, 0, 2)
    if n1 == 1:
        rows = jnp.concatenate([rows, jnp.zeros_like(rows)], axis=1)
    return _dft_forward(rows, 'fwd_a_real', n1, n2, tabs)


def _long_conv_pair(z, spec):
    pair, n, c = z.shape
    assert pair == 2
    n1, n2 = _dft_split(2 * n)
    n1_io = max(n1 // 2, 1)
    tabs = _dft_tables(n1_io, n1, n2)
    if n1 > 1:
        rows = z.reshape(2, n1_io, n2, c).transpose(2, 0, 1, 3).reshape(n2, 2 * n1_io, c)
    else:
        rows = jnp.pad(z, ((0, 0), (0, n), (0, 0))).transpose(1, 0, 2)
    y = _dft_forward(rows, 'fwd_a', n1, n2, tabs)
    yr, yi = y[:, :n2], y[:, n2:]
    hr, hi = spec[:, :n2], spec[:, n2:]
    y = jnp.concatenate([yr * hr - yi * hi, yr * hi + yi * hr], axis=1)
    d = _bmm3(*_split_bf16(tabs['inv_b']), y, 8)
    d = _swap_stage(d, n2, n1)
    if n1 > 1:
        out = _bmm3(*_split_bf16(tabs['inv_a']), d, 16)
        return out.reshape(n2, 2, n1_io, c).transpose(1, 2, 0, 3).reshape(2, n, c)
    return (d.reshape(n2, 2, c).transpose(1, 0, 2) / (n1 * n2))[:, :n]


def _hyena_mixer(u, conv_w, conv_b, w1, b1, w2, b2, w3, freq, skip):
    n = u.shape[1]
    u = _depthwise_conv(u, conv_w, conv_b, HY_CONV // 2, HY_CONV - 1 - HY_CONV // 2)
    gates = (u[..., :HY_WIDTH], u[..., HY_WIDTH:2 * HY_WIDTH])
    z = u[..., 2 * HY_WIDTH:]
    two_sided = _hyena_filter(n, w1, b1, w2, b2, w3, freq)
    spec = _filter_spectrum(two_sided.reshape(2 * n, HY_ORDER * HY_WIDTH))
    for o in range(HY_ORDER):
        zc = _long_conv_pair(z, spec[..., o * HY_WIDTH:(o + 1) * HY_WIDTH])
        z = gates[o] * (zc + skip[o] * z)
    return z


LRU_CONV_MARGIN = SUBLANES
LRU_TIME_CHUNK = 1024


def _gelu_tanh(x):
    return 0.5 * x * (1.0 + jnp.tanh(math.sqrt(2.0 / math.pi) * (x + 0.044715 * (x * x * x))))


def _rglru_body(ug_ref, ur_ref, cw_ref, cb_ref, wa_ref, ba_ref, wx_ref, bx_ref, cl_ref, o_ref,
                pad_s, af_s, bf_s, ar_s, br_s, hf_s, hr_s, *, n_ctx, n_lat):
    a_s = (af_s, ar_s)
    b_s = (bf_s, br_s)
    mg = LRU_CONV_MARGIN
    cw = cw_ref[...]
    zeros_margin = jnp.zeros((mg, LANES), F32)
    for r0, n in ((0, n_ctx), (n_ctx, n_lat)):
        pad_s[0:mg, :] = zeros_margin
        pad_s[mg:mg + n, :] = ur_ref[0, r0:r0 + n, :]
        pad_s[mg + n:2 * mg + n, :] = zeros_margin
        chunk = min(n, LRU_TIME_CHUNK)
        for c0 in range(0, n, chunk):
            xr = cb_ref[...] + cw[0:1] * pad_s[mg + c0 - 2:mg + c0 - 2 + chunk, :]
            for j in range(1, LRU_CONV):
                xr = xr + cw[j:j + 1] * pad_s[mg + c0 + j - 2:mg + c0 + j - 2 + chunk, :]
            xb = xr.astype(BF16)
            for d in range(2):
                r = jax.nn.sigmoid(jnp.dot(xb, wa_ref[d, 0], preferred_element_type=F32) + ba_ref[d:d + 1, :])
                i = jax.nn.sigmoid(jnp.dot(xb, wx_ref[d, 0], preferred_element_type=F32) + bx_ref[d:d + 1, :])
                log_a = cl_ref[d:d + 1, :] * r
                a_s[d][r0 + c0:r0 + c0 + chunk, :] = jnp.exp(log_a)
                b_s[d][r0 + c0:r0 + c0 + chunk, :] = jnp.sqrt(1.0 - jnp.exp(2.0 * log_a)) * (i * xr)

    def scan_segment(r0, n, carry):
        def step(t, hc):
            hf, hr = hc
            tf = r0 + t
            tr = r0 + n - 1 - t
            hf = af_s[pl.ds(tf, 1), :] * hf + bf_s[pl.ds(tf, 1), :]
            hr = ar_s[pl.ds(tr, 1), :] * hr + br_s[pl.ds(tr, 1), :]
            hf_s[pl.ds(tf, 1), :] = hf
            hr_s[pl.ds(tr, 1), :] = hr
            return hf, hr
        return lax.fori_loop(0, n, step, carry, unroll=8)

    h0 = jnp.zeros((1, LANES), F32)
    carry = scan_segment(0, n_ctx, (h0, h0))
    scan_segment(n_ctx, n_lat, carry)
    o_ref[0] = ((hf_s[...] + hr_s[...]) * _gelu_tanh(ug_ref[0])).astype(o_ref.dtype)


def _block_diag_tiles(w):
    w = w.reshape(2, LRU_HEADS // 2, 2, LRU_HEAD_DIM, LRU_HEAD_DIM)
    z = jnp.zeros_like(w[:, :, 0])
    top = jnp.concatenate([w[:, :, 0], z], axis=-1)
    bot = jnp.concatenate([z, w[:, :, 1]], axis=-1)
    return jnp.concatenate([top, bot], axis=-2).astype(BF16)


def _rglru_mixer(u_all, n_ctx, conv_w, conv_b, wa, ba, wx, bx, lam):
    bsz, n_all, _ = u_all.shape
    n_lat = n_all - n_ctx
    n_tiles = LRU_WIDTH // LANES
    gate_blk0 = HY_IN // LANES
    rec_blk0 = (HY_IN + LRU_WIDTH) // LANES
    decay_scale = -LRU_C * jax.nn.softplus(-lam)
    seq = lambda col0: pl.BlockSpec((1, n_all, LANES), lambda b, j: (b, 0, col0 + j))
    vec = lambda rows: pl.BlockSpec((rows, LANES), lambda b, j: (0, j))
    mat = pl.BlockSpec((2, 1, LANES, LANES), lambda b, j: (0, j, 0, 0))
    full = pltpu.VMEM((n_all, LANES), F32)
    return pl.pallas_call(
        partial(_rglru_body, n_ctx=n_ctx, n_lat=n_lat),
        grid=(bsz, n_tiles),
        in_specs=[seq(gate_blk0), seq(rec_blk0), vec(LRU_CONV), vec(1), mat, vec(2), mat, vec(2), vec(2)],
        out_specs=pl.BlockSpec((1, n_all, LANES), lambda b, j: (b, 0, j)),
        out_shape=jax.ShapeDtypeStruct((bsz, n_all, LRU_WIDTH), BF16),
        scratch_shapes=[pltpu.VMEM((n_lat + 2 * LRU_CONV_MARGIN, LANES), F32),
                        full, full, full, full, full, full],
        compiler_params=pltpu.CompilerParams(
            dimension_semantics=("parallel", "parallel"),
            vmem_limit_bytes=VMEM_LIMIT_BYTES),
        name="rglru",
    )(u_all, u_all, conv_w, conv_b.reshape(1, LRU_WIDTH), _block_diag_tiles(wa), ba,
      _block_diag_tiles(wx), bx, decay_scale)


ATT_Q_TILE = 256
ATT_KV_CHUNK = 2048


def _attn_body(q_ref, k_ref, v_ref, lv_ref, g_ref, o_ref, *, lam_init):
    lv = lv_ref[...]
    lam = (jnp.exp(jnp.sum(lv[0:1] * lv[1:2], keepdims=True))
           - jnp.exp(jnp.sum(lv[2:3] * lv[3:4], keepdims=True)) + lam_init)
    q = q_ref[0]
    nk = k_ref.shape[1]
    work = [(m, c0, min(c0 + ATT_KV_CHUNK, nk)) for c0 in range(0, nk, ATT_KV_CHUNK) for m in range(2)]

    def scores(m, c0, c1):
        qm = q[:, m * ATT_HEAD_DIM:(m + 1) * ATT_HEAD_DIM]
        km = k_ref[0, c0:c1, m * ATT_HEAD_DIM:(m + 1) * ATT_HEAD_DIM]
        return lax.dot_general(qm, km, (((1,), (1,)), ((), ())), preferred_element_type=F32)

    state = [None, None]
    s_next = scores(*work[0])
    for j, (m, c0, c1) in enumerate(work):
        s = s_next
        if j + 1 < len(work):
            s_next = scores(*work[j + 1])
        chunk_max = jnp.max(s, axis=-1, keepdims=True)
        v = v_ref[0, c0:c1, :]
        if state[m] is None:
            p = jnp.exp(s - chunk_max)
            state[m] = (chunk_max, jnp.sum(p, axis=-1, keepdims=True),
                        jnp.dot(p.astype(BF16), v, preferred_element_type=F32))
        else:
            mx_old, sum_old, acc_old = state[m]
            mx = jnp.maximum(mx_old, chunk_max)
            rescale = jnp.exp(mx_old - mx)
            p = jnp.exp(s - mx)
            state[m] = (mx, rescale * sum_old + jnp.sum(p, axis=-1, keepdims=True),
                        rescale * acc_old + jnp.dot(p.astype(BF16), v, preferred_element_type=F32))
    o = state[0][2] / state[0][1] - lam * (state[1][2] / state[1][1])
    y = o * lax.rsqrt(jnp.mean(o * o, axis=-1, keepdims=True) + LN_EPS) * g_ref[...] * (1.0 - lam_init)
    o_ref[0] = y.astype(o_ref.dtype)


def _diff_attention(qkv, lam_vecs, gain, lam_init, q_row0, n_q, n_k):
    bsz = qkv.shape[0]
    tq = ATT_Q_TILE
    q_blk0 = q_row0 // tq
    n_heads = ATT_HEADS
    return pl.pallas_call(
        partial(_attn_body, lam_init=lam_init),
        grid=(bsz, n_heads, n_q // tq),
        in_specs=[
            pl.BlockSpec((1, tq, ATT_V_DIM), lambda b, h, i: (b, q_blk0 + i, h)),
            pl.BlockSpec((1, n_k, ATT_V_DIM), lambda b, h, i: (b, 0, n_heads + h)),
            pl.BlockSpec((1, n_k, ATT_V_DIM), lambda b, h, i: (b, 0, 2 * n_heads + h)),
            pl.BlockSpec((4, ATT_HEAD_DIM), lambda b, h, i: (0, 0)),
            pl.BlockSpec((1, ATT_V_DIM), lambda b, h, i: (0, 0)),
        ],
        out_specs=pl.BlockSpec((1, tq, ATT_V_DIM), lambda b, h, i: (b, i, h)),
        out_shape=jax.ShapeDtypeStruct((bsz, n_q, ATT_WIDTH), BF16),
        compiler_params=pltpu.CompilerParams(
            dimension_semantics=("parallel", "parallel", "arbitrary"),
            vmem_limit_bytes=VMEM_LIMIT_BYTES),
        name="diff_attention",
    )(qkv, qkv, qkv, lam_vecs, gain.reshape(1, ATT_V_DIM))


def _axial_rope_tables(n):
    rows = n // GRID_W
    row = jnp.broadcast_to(jnp.arange(rows, dtype=F32)[:, None], (rows, GRID_W)).reshape(-1)
    col = jnp.broadcast_to(jnp.arange(GRID_W, dtype=F32)[None, :], (rows, GRID_W)).reshape(-1)
    inv = ROPE_THETA ** (-jnp.arange(0, ROPE_AXIS_DIM, 2, dtype=F32) / ROPE_AXIS_DIM)
    ang = jnp.stack([row[:, None] * inv, col[:, None] * inv], axis=1)
    return jnp.cos(ang), jnp.sin(ang)


def _apply_axial_rope(x, cos, sin):
    xs = x.reshape(x.shape[:-1] + (2, 2, ROPE_AXIS_DIM // 2))
    x1, x2 = xs[..., 0, :], xs[..., 1, :]
    cb, sb = cos[None, :, None, None], sin[None, :, None, None]
    out = jnp.stack([x1 * cb - x2 * sb, x2 * cb + x1 * sb], axis=-2)
    return out.reshape(x.shape)


def _diff_attention_mixer(u_att, n_ctx, lam_vecs, subln_gain, lam_init, need_ctx):
    bsz, n_all, _ = u_att.shape
    n_lat = n_all - n_ctx
    scale = ATT_HEAD_DIM ** -0.5
    u_lat = u_att[:, n_ctx:]
    u_ctx = u_att[:, :n_ctx]
    q_l = u_lat[..., :ATT_QK].reshape(bsz, n_lat, ATT_HEADS, 2, ATT_HEAD_DIM)
    k_l = u_lat[..., ATT_QK:2 * ATT_QK].reshape(bsz, n_lat, ATT_HEADS, 2, ATT_HEAD_DIM)
    cos, sin = _axial_rope_tables(n_lat)
    q_l = _apply_axial_rope(q_l, cos, sin) * scale
    k_l = _apply_axial_rope(k_l, cos, sin)
    lat = jnp.concatenate([q_l.reshape(bsz, n_lat, ATT_QK), k_l.reshape(bsz, n_lat, ATT_QK),
                           u_lat[..., 2 * ATT_QK:]], axis=-1).astype(BF16)
    cx = jnp.concatenate([u_ctx[..., :ATT_QK] * scale, u_ctx[..., ATT_QK:]], axis=-1).astype(BF16)
    qkv = jnp.concatenate([cx, lat], axis=1)
    y_lat = _diff_attention(qkv, lam_vecs, subln_gain, lam_init, n_ctx, n_lat, n_all)
    y_ctx = _diff_attention(qkv, lam_vecs, subln_gain, lam_init, 0, n_ctx, n_ctx) if need_ctx else None
    return y_lat, y_ctx


ROW_TILE = 256
IN_PROJ_COL_TILE = 1408
ADA_COL_TILE = 1536
COMBINE_ROW_TILE = 128


def _ln_rows(x):
    mu = jnp.mean(x, axis=-1, keepdims=True)
    xc = x - mu
    return xc * lax.rsqrt(jnp.mean(xc * xc, axis=-1, keepdims=True) + LN_EPS)


def _segments(bsz, seg_rows, tile):
    ids = []
    for b in range(bsz):
        for rows, kind in seg_rows:
            assert rows % tile == 0
            ids += [bsz if kind == 'ctx' else b] * (rows // tile)
    return jnp.asarray(ids, jnp.int32)


def _ada_body(c_ref, w_ref, b_ref, o_ref):
    c = c_ref[...]
    w = w_ref[0]
    w_hi = w.astype(BF16)
    w_lo = (w - w_hi.astype(F32)).astype(BF16)
    o_ref[...] = _split_dot(c * jax.nn.sigmoid(c), w_hi, w_lo) + b_ref[0]


def _ada_modulation(cond, ada_w, ada_b, layer):
    s, d = cond.shape
    n = ada_w.shape[-1]
    rows = -(-s // SUBLANES) * SUBLANES
    tn = ADA_COL_TILE
    assert n % tn == 0
    out = pl.pallas_call(
        _ada_body, grid=(n // tn,),
        in_specs=[pl.BlockSpec((rows, d), lambda j: (0, 0)),
                  pl.BlockSpec((1, d, tn), lambda j: (layer, 0, j)),
                  pl.BlockSpec((1, 1, tn), lambda j: (layer, 0, j))],
        out_specs=pl.BlockSpec((rows, tn), lambda j: (0, j)),
        out_shape=jax.ShapeDtypeStruct((rows, n), F32),
        compiler_params=pltpu.CompilerParams(dimension_semantics=("parallel",),
                                             vmem_limit_bytes=VMEM_LIMIT_BYTES),
        name="ada_modulation",
    )(jnp.pad(cond, ((0, rows - s), (0, 0))), ada_w, ada_b.reshape(ada_b.shape[0], 1, n))
    return out[:s]


def _ln_mm_body(seg_ref, x_ref, mod_ref, w_ref, o_ref):
    m = mod_ref[0]
    h = _ln_rows(x_ref[...]) * (1.0 + m[1:2]) + m[0:1]
    o_ref[...] = jnp.dot(h.astype(BF16), w_ref[...], preferred_element_type=F32)


def _ln_mod_matmul(x, mod, seg, w):
    t, d = x.shape
    n = w.shape[1]
    tm, tn = ROW_TILE, IN_PROJ_COL_TILE
    assert t % tm == 0 and n % tn == 0
    grid_spec = pltpu.PrefetchScalarGridSpec(
        num_scalar_prefetch=1,
        grid=(n // tn, t // tm),
        in_specs=[pl.BlockSpec((tm, d), lambda j, i, s: (i, 0)),
                  pl.BlockSpec((1, 2, d), lambda j, i, s: (s[i], 0, 0)),
                  pl.BlockSpec((d, tn), lambda j, i, s: (0, j))],
        out_specs=pl.BlockSpec((tm, tn), lambda j, i, s: (i, j)),
    )
    return pl.pallas_call(
        _ln_mm_body, grid_spec=grid_spec,
        out_shape=jax.ShapeDtypeStruct((t, n), F32),
        compiler_params=pltpu.CompilerParams(dimension_semantics=("parallel", "arbitrary"),
                                             vmem_limit_bytes=VMEM_LIMIT_BYTES),
        name="ln_mod_in_proj",
    )(seg, x, mod, w)


def _split_dot(a, b_hi, b_lo):
    a_hi = a.astype(BF16)
    a_lo = (a - a_hi.astype(F32)).astype(BF16)
    return (jnp.dot(a_hi, b_hi, preferred_element_type=F32) + jnp.dot(a_hi, b_lo, preferred_element_type=F32)
            + jnp.dot(a_lo, b_hi, preferred_element_type=F32))


def _out_ln_body(seg_ref, mixed_ref, x_ref, mod_ref, w_ref, lng_ref, rwh_ref, rwl_ref, rb_ref,
                 x1_ref, h2_ref, h2b_ref, ti_ref, tw_ref):
    m = mod_ref[0]
    mix = jnp.dot(mixed_ref[...], w_ref[...], preferred_element_type=F32)
    x1 = _ln_rows(DEEPNORM_ALPHA * x_ref[...] + m[0:1] * mix) * lng_ref[0:1] + lng_ref[1:2]
    x1_ref[...] = x1
    h2 = _ln_rows(x1) * (1.0 + m[2:3]) + m[1:2]
    h2_ref[...] = h2
    h2b_ref[...] = h2.astype(BF16)
    scores = jax.nn.sigmoid(_split_dot(h2, rwh_ref[...], rwl_ref[...]))
    lane = lax.broadcasted_iota(jnp.int32, scores.shape, 1).astype(F32)
    biased = jnp.where(lane < N_EXPERTS, scores + rb_ref[...], -jnp.inf)
    top_i = jnp.zeros(scores.shape, F32)
    top_s = jnp.zeros(scores.shape, F32)
    for k in range(TOP_K):
        best = jnp.max(biased, axis=-1, keepdims=True)
        pick = jnp.min(jnp.where(biased == best, lane, float(LANES)), axis=-1, keepdims=True)
        hit = lane == pick
        top_i = jnp.where(lane == k, pick, top_i)
        top_s = jnp.where(lane == k, jnp.sum(jnp.where(hit, scores, 0.0), axis=-1, keepdims=True), top_s)
        biased = jnp.where(hit, -jnp.inf, biased)
    ti_ref[...] = top_i.astype(jnp.int32)
    tw_ref[...] = top_s / jnp.sum(top_s, axis=-1, keepdims=True) * ROUTED_SCALE


def _out_proj_norm_route(mixed, x, mod, seg, w, ln_gb, router_w, router_b):
    t, d = x.shape
    tm = ROW_TILE
    rw = jnp.pad(router_w, ((0, 0), (0, LANES - N_EXPERTS)))
    rw_hi, rw_lo = _split_bf16(rw)
    rb = jnp.pad(router_b, (0, LANES - N_EXPERTS)).reshape(1, LANES)
    row = lambda width: pl.BlockSpec((tm, width), lambda i, s: (i, 0))
    const = lambda shape: pl.BlockSpec(shape, lambda i, s: (0,) * len(shape))
    grid_spec = pltpu.PrefetchScalarGridSpec(
        num_scalar_prefetch=1,
        grid=(t // tm,),
        in_specs=[row(d), row(d), pl.BlockSpec((1, 3, d), lambda i, s: (s[i], 0, 0)), const((d, d)),
                  const((2, d)), const((d, LANES)), const((d, LANES)), const((1, LANES))],
        out_specs=[row(d), row(d), row(d), row(LANES), row(LANES)],
    )
    x1, h2, h2b, top_i, top_w = pl.pallas_call(
        _out_ln_body, grid_spec=grid_spec,
        out_shape=[jax.ShapeDtypeStruct((t, d), F32), jax.ShapeDtypeStruct((t, d), F32),
                   jax.ShapeDtypeStruct((t, d), BF16), jax.ShapeDtypeStruct((t, LANES), jnp.int32),
                   jax.ShapeDtypeStruct((t, LANES), F32)],
        compiler_params=pltpu.CompilerParams(dimension_semantics=("parallel",),
                                             vmem_limit_bytes=VMEM_LIMIT_BYTES),
        name="out_proj_norm_route",
    )(seg, mixed, x, mod, w, ln_gb, rw_hi, rw_lo, rb)
    return x1, h2, h2b, top_i[:, :TOP_K], top_w[:, :TOP_K]


def _shared_body(h_ref, wgu_ref, wd_ref, o_ref):
    gu = jnp.dot(h_ref[...], wgu_ref[...], preferred_element_type=F32)
    f = wd_ref.shape[0]
    g, u = gu[:, :f], gu[:, f:]
    act = (g * jax.nn.sigmoid(g) * u).astype(BF16)
    o_ref[...] = jnp.dot(act, wd_ref[...], preferred_element_type=F32)


def _shared_expert(hb, w_gate, w_up, w_down):
    t, d = hb.shape
    f = w_gate.shape[1]
    tm = ROW_TILE
    wgu = jnp.concatenate([w_gate, w_up], axis=1).astype(BF16)
    return pl.pallas_call(
        _shared_body, grid=(t // tm,),
        in_specs=[pl.BlockSpec((tm, d), lambda i: (i, 0)), pl.BlockSpec((d, 2 * f), lambda i: (0, 0)),
                  pl.BlockSpec((f, d), lambda i: (0, 0))],
        out_specs=pl.BlockSpec((tm, d), lambda i: (i, 0)),
        out_shape=jax.ShapeDtypeStruct((t, d), F32),
        compiler_params=pltpu.CompilerParams(dimension_semantics=("parallel",),
                                             vmem_limit_bytes=VMEM_LIMIT_BYTES),
        name="shared_expert",
    )(hb, wgu, w_down.astype(BF16))


def _combine_body(seg_ref, *refs):
    y_refs, (sh_ref, x_ref, mod_ref, lng_ref, o_ref) = refs[:TOP_K], refs[TOP_K:]
    ffn = sh_ref[...]
    for y_ref in y_refs:
        ffn = ffn + y_ref[...]
    o_ref[...] = _ln_rows(DEEPNORM_ALPHA * x_ref[...] + mod_ref[0] * ffn) * lng_ref[0:1] + lng_ref[1:2]


def _combine_norm(y, shared, x1, gate, seg, ln_gb):
    t, d = x1.shape
    tm = COMBINE_ROW_TILE
    tiles = t // tm
    row = pl.BlockSpec((tm, d), lambda i, s: (i, 0))
    y_specs = [pl.BlockSpec((tm, d), lambda i, s, k=k: (k * tiles + i, 0)) for k in range(TOP_K)]
    grid_spec = pltpu.PrefetchScalarGridSpec(
        num_scalar_prefetch=1,
        grid=(tiles,),
        in_specs=y_specs + [row, row, pl.BlockSpec((1, 1, d), lambda i, s: (s[i], 0, 0)),
                            pl.BlockSpec((2, d), lambda i, s: (0, 0))],
        out_specs=row,
    )
    return pl.pallas_call(
        _combine_body, grid_spec=grid_spec,
        out_shape=jax.ShapeDtypeStruct((t, d), F32),
        compiler_params=pltpu.CompilerParams(dimension_semantics=("parallel",),
                                             vmem_limit_bytes=VMEM_LIMIT_BYTES),
        name="combine_norm",
    )(seg, *([y] * TOP_K), shared, x1, gate, ln_gb)


MOE_BLOCK_ROWS = 256
MOE_VMEM_LIMIT_BYTES = 56 * 1024 * 1024
MOE_OUTBOUND_DMA_PRIORITY = 1
MOE_WEIGHT_DMA_PRIORITY = 1


def _moe_body(be_ref, nu_ref, ne_ref, idx_ref, idx_nxt_ref, rw_ref, wg_hbm, wu_hbm, wd_hbm, h_hbm, o_hbm,
              xbuf, ybuf, wg_f, wu_f, wd_f, wg_s, wu_s, wd_s, gsem, ssem, wsem, *, bm, layer):
    i = pl.program_id(0)
    n_used = nu_ref[0]
    slot = i % 2

    def weight_copies(expert):
        return (pltpu.make_async_copy(wg_hbm.at[layer, expert], wg_f, wsem.at[0]),
                pltpu.make_async_copy(wu_hbm.at[layer, expert], wu_f, wsem.at[1]),
                pltpu.make_async_copy(wd_hbm.at[layer, expert], wd_f, wsem.at[2]))

    def gather_rows(tok_ref, dst_slot):
        for r in range(bm):
            pltpu.make_async_copy(h_hbm.at[pl.ds(tok_ref[0, 0, r], 1)], xbuf.at[dst_slot, pl.ds(r, 1)],
                                  gsem.at[dst_slot]).start()

    def wait_gather(s):
        pltpu.make_async_copy(h_hbm.at[pl.ds(0, bm)], xbuf.at[s], gsem.at[s]).wait()

    def wait_scatter(s):
        pltpu.make_async_copy(ybuf.at[s], o_hbm.at[pl.ds(0, bm)], ssem.at[s]).wait()

    e = be_ref[i]
    e_prev = be_ref[jnp.maximum(i - 1, 0)]

    @pl.when(i == 0)
    def _():
        gather_rows(idx_ref, 0)
        for copy in weight_copies(e):
            copy.start(priority=MOE_WEIGHT_DMA_PRIORITY)

    @pl.when((i < n_used) & ((i == 0) | (e != e_prev)))
    def _():
        for copy in weight_copies(e):
            copy.wait()
        wg_s[...] = wg_f[...].astype(BF16)
        wu_s[...] = wu_f[...].astype(BF16)
        wd_s[...] = wd_f[...].astype(BF16)

        @pl.when(ne_ref[i] >= 0)
        def _():
            for copy in weight_copies(ne_ref[i]):
                copy.start(priority=MOE_WEIGHT_DMA_PRIORITY)

    def block_step(s):
        wait_gather(s)

        @pl.when(i >= 2)
        def _():
            wait_scatter(s)

        gather_rows(idx_nxt_ref, 1 - s)

        xb = xbuf[s].astype(BF16)
        g = jnp.dot(xb, wg_s[...], preferred_element_type=F32)
        u = jnp.dot(xb, wu_s[...], preferred_element_type=F32)
        act = (g * jax.nn.sigmoid(g) * u).astype(BF16)
        ybuf[s] = jnp.dot(act, wd_s[...], preferred_element_type=F32) * rw_ref[...]
        for r in range(bm):
            pltpu.make_async_copy(ybuf.at[s, pl.ds(r, 1)], o_hbm.at[pl.ds(idx_ref[0, 0, bm + r], 1)],
                                  ssem.at[s]).start(priority=MOE_OUTBOUND_DMA_PRIORITY)

        @pl.when(i == n_used - 1)
        def _():
            @pl.when(i >= 1)
            def _():
                wait_scatter(1 - s)
            wait_scatter(s)
            wait_gather(1 - s)

    for s in range(2):
        pl.when((i < n_used) & (slot == s))(partial(block_step, s))

    @pl.when(i >= n_used)
    def _():
        xbuf[0] = jnp.zeros(xbuf.shape[1:], F32)
        fill = pltpu.make_async_copy(xbuf.at[0], o_hbm.at[pl.ds(pl.multiple_of(i * bm, bm), bm)], gsem.at[0])
        fill.start()
        fill.wait()


def _routed_experts(h, idx, row_w, block_e, n_used, next_e, w_gate, w_up, w_down, layer):
    bm = MOE_BLOCK_ROWS
    n_blocks = idx.shape[0]
    d = h.shape[1]
    f = w_gate.shape[-1]
    anywhere = pl.BlockSpec(memory_space=pl.ANY)
    grid_spec = pltpu.PrefetchScalarGridSpec(
        num_scalar_prefetch=3,
        grid=(n_blocks,),
        in_specs=[
            pl.BlockSpec((1, 1, 2 * bm), lambda i, be, nu, ne: (i, 0, 0), memory_space=pltpu.SMEM),
            pl.BlockSpec((1, 1, 2 * bm), lambda i, be, nu, ne: (jnp.minimum(i + 1, n_blocks - 1), 0, 0),
                         memory_space=pltpu.SMEM),
            pl.BlockSpec((bm, 1), lambda i, be, nu, ne: (i, 0)),
            anywhere, anywhere, anywhere, anywhere,
        ],
        out_specs=anywhere,
        scratch_shapes=[pltpu.VMEM((2, bm, d), F32), pltpu.VMEM((2, bm, d), F32),
                        pltpu.VMEM((d, f), F32), pltpu.VMEM((d, f), F32), pltpu.VMEM((f, d), F32),
                        pltpu.VMEM((d, f), BF16), pltpu.VMEM((d, f), BF16), pltpu.VMEM((f, d), BF16),
                        pltpu.SemaphoreType.DMA((2,)), pltpu.SemaphoreType.DMA((2,)),
                        pltpu.SemaphoreType.DMA((3,))],
    )
    return pl.pallas_call(
        partial(_moe_body, bm=bm, layer=layer),
        grid_spec=grid_spec,
        out_shape=jax.ShapeDtypeStruct((n_blocks * bm, d), F32),
        compiler_params=pltpu.CompilerParams(
            dimension_semantics=("arbitrary",),
            vmem_limit_bytes=MOE_VMEM_LIMIT_BYTES),
        name="routed_experts",
    )(block_e, n_used, next_e, idx, idx, row_w, w_gate, w_up, w_down, h)


def _route(top_idx, top_w):
    bm = MOE_BLOCK_ROWS
    n_tok = top_idx.shape[0]
    n_assign = n_tok * TOP_K
    flat_e = top_idx.reshape(-1)
    flat_w = top_w.reshape(-1)
    order = jnp.argsort(flat_e).astype(jnp.int32)
    counts = jnp.sum(flat_e[None, :] == jnp.arange(N_EXPERTS)[:, None], axis=1).astype(jnp.int32)
    padded = (counts + bm - 1) // bm * bm
    padded_end = jnp.cumsum(padded)
    live_end = jnp.cumsum(counts)
    shift = padded_end - padded - (live_end - counts)
    n_blocks = -(-n_assign // bm) + N_EXPERTS
    block_row0 = jnp.arange(n_blocks, dtype=jnp.int32) * bm
    block_e = jnp.minimum(jnp.sum(padded_end[None, :] <= block_row0[:, None], axis=1),
                          N_EXPERTS - 1).astype(jnp.int32)
    n_used = (padded_end[-1:] // bm).astype(jnp.int32)
    experts = jnp.arange(N_EXPERTS, dtype=jnp.int32)
    later = jnp.where((experts[None, :] > experts[:, None]) & (counts[None, :] > 0), experts[None, :], N_EXPERTS)
    next_live = jnp.min(later, axis=1)
    next_e = jnp.where(next_live < N_EXPERTS, next_live, -1)[block_e].astype(jnp.int32)
    rows = jnp.arange(n_blocks * bm, dtype=jnp.int32).reshape(n_blocks, bm)
    row_slot = rows - shift[block_e][:, None]
    row_live = row_slot < live_end[block_e][:, None]
    row_assign = order[jnp.where(row_live, row_slot, 0)]
    row_tok = jnp.where(row_live, row_assign // TOP_K, 0)
    row_dst = jnp.where(row_live, (row_assign % TOP_K) * n_tok + row_tok,
                        n_assign + rows - live_end[block_e][:, None])
    row_w = jnp.where(row_live, flat_w[row_assign], 0.0)
    idx = jnp.concatenate([row_tok, row_dst], axis=1).astype(jnp.int32)[:, None, :]
    return idx, row_w.reshape(-1, 1), block_e, n_used, next_e


def kernel(x, c, ctx, c_ctx, ada_w, ada_b, w_in, hy_conv_w, hy_conv_b, hy_w1, hy_b1, hy_w2, hy_b2, hy_w3, hy_freq, hy_skip, lru_conv_w, lru_conv_b, lru_wa, lru_ba, lru_wx, lru_bx, lru_lambda, att_lambda, att_subln, w_out, ln_g, ln_b, router_w, router_b, exp_w_gate, exp_w_up, exp_w_down, sh_w_gate, sh_w_up, sh_w_down):
    bsz, n_lat, d = x.shape
    n_ctx = ctx.shape[1]
    n_all = n_ctx + n_lat
    x_all = jnp.concatenate([ctx, x], axis=1).reshape(bsz * n_all, d)
    both = ((n_ctx, 'ctx'), (n_lat, 'lat'))
    seg_all = _segments(bsz, both, ROW_TILE)
    x_out = None

    for l in range(DEPTH):
        need_ctx = l < DEPTH - 1
        lam_init = 0.8 - 0.6 * math.exp(-0.3 * l)
        cond = jnp.concatenate([c, c_ctx[None, :]], axis=0)
        mod = _ada_modulation(cond, ada_w, ada_b, l)
        sh1, sc1, g1, sh2, sc2, g2 = jnp.split(mod, 6, axis=-1)
        hy_args = (hy_conv_w[l], hy_conv_b[l], hy_w1[l], hy_b1[l], hy_w2[l], hy_b2[l],
                   hy_w3[l], hy_freq[l], hy_skip[l])

        u_all = _ln_mod_matmul(x_all, jnp.stack([sh1, sc1], axis=1), seg_all,
                               w_in[l].astype(BF16)).reshape(bsz, n_all, -1)
        y_hy_l = _hyena_mixer(u_all[:, n_ctx:, :HY_IN], *hy_args)
        y_lru = _rglru_mixer(u_all, n_ctx, lru_conv_w[l], lru_conv_b[l], lru_wa[l], lru_ba[l],
                             lru_wx[l], lru_bx[l], lru_lambda[l])
        y_att_l, y_att_c = _diff_attention_mixer(u_all[..., HY_IN + LRU_IN:], n_ctx, att_lambda[l],
                                                 att_subln[l], lam_init, need_ctx)
        if need_ctx:
            y_hy = jnp.concatenate([_hyena_mixer(u_all[:, :n_ctx, :HY_IN], *hy_args), y_hy_l], axis=1)
            mixed = jnp.concatenate([y_hy.astype(BF16), y_lru, jnp.concatenate([y_att_c, y_att_l], axis=1)], axis=-1)
            x_res, seg, seg_c = x_all, seg_all, _segments(bsz, both, COMBINE_ROW_TILE)
        else:
            mixed = jnp.concatenate([y_hy_l.astype(BF16), y_lru[:, n_ctx:], y_att_l], axis=-1)
            x_res = x_all.reshape(bsz, n_all, d)[:, n_ctx:].reshape(bsz * n_lat, d)
            seg = _segments(bsz, ((n_lat, 'lat'),), ROW_TILE)
            seg_c = _segments(bsz, ((n_lat, 'lat'),), COMBINE_ROW_TILE)
        n_tok = x_res.shape[0]
        ln_gb1 = jnp.stack([ln_g[l, 0], ln_b[l, 0]])
        ln_gb2 = jnp.stack([ln_g[l, 1], ln_b[l, 1]])
        x1, h2, h2b, top_idx, top_w = _out_proj_norm_route(
            mixed.reshape(n_tok, d), x_res, jnp.stack([g1, sh2, sc2], axis=1), seg,
            w_out[l].astype(BF16), ln_gb1, router_w[l], router_b[l])

        idx, row_w, block_e, n_used, next_e = _route(top_idx, top_w)
        y = _routed_experts(h2, idx, row_w, block_e, n_used, next_e, exp_w_gate, exp_w_up, exp_w_down, l)
        shared = _shared_expert(h2b, sh_w_gate[l], sh_w_up[l], sh_w_down[l])
        x_out = _combine_norm(y, shared, x1, g2[:, None, :], seg_c, ln_gb2)
        x_all = x_out
    return x_out.reshape(bsz, n_lat, d)
```

```python
import math
from functools import lru_cache, partial

import numpy as np
import jax
import jax.numpy as jnp
from jax import lax
from jax.experimental import pallas as pl
from jax.experimental.pallas import tpu as pltpu

D_MODEL = 2048
DEPTH = 2
GRID_W = 64
F32 = jnp.float32
BF16 = jnp.bfloat16
LANES = 128
SUBLANES = 8

HY_WIDTH = D_MODEL // 4
LRU_WIDTH = D_MODEL // 4
ATT_WIDTH = D_MODEL // 2
HY_ORDER = 2
HY_CONV = 3
HY_BANDS = 8
HY_TARGET = 1e-2
HY_MAX_DECAY = math.log(HY_TARGET) / 0.3
HY_MIN_DECAY = math.log(HY_TARGET) / 1.5
HY_IN = (HY_ORDER + 1) * HY_WIDTH

LRU_HEADS = 8
LRU_HEAD_DIM = LRU_WIDTH // LRU_HEADS
LRU_CONV = 4
LRU_C = 8.0
LRU_IN = 2 * LRU_WIDTH

ATT_HEADS = 4
ATT_V_DIM = ATT_WIDTH // ATT_HEADS
ATT_HEAD_DIM = ATT_V_DIM // 2
ATT_QK = ATT_HEADS * 2 * ATT_HEAD_DIM
ROPE_THETA = 10000.0
ROPE_AXIS_DIM = ATT_HEAD_DIM // 2

N_EXPERTS = 64
TOP_K = 8
ROUTED_SCALE = 2.5

LN_EPS = 1e-5
DEEPNORM_ALPHA = (2 * DEPTH) ** 0.25

VMEM_LIMIT_BYTES = 48 * 1024 * 1024


def _depthwise_conv(x, w, b, pad_left, pad_right):
    y = lax.conv_general_dilated(
        x, w[:, None, :], window_strides=(1,),
        padding=[(pad_left, pad_right)], dimension_numbers=('NWC', 'WIO', 'NWC'),
        feature_group_count=x.shape[-1])
    return y + b


def _hyena_filter(n, w1, b1, w2, b2, w3, freq):
    pos = jnp.arange(n, dtype=F32)
    t = (pos / max(n - 1, 1))[:, None]
    bands = jnp.linspace(1e-4, HY_BANDS - 1, HY_BANDS, dtype=F32)
    ang = (2.0 * math.pi / n) * pos[:, None] * bands
    feats = jnp.concatenate([t, jnp.cos(ang), -jnp.sin(ang)], axis=-1)
    hp = lax.Precision.HIGHEST
    decay = jnp.abs(jnp.linspace(HY_MIN_DECAY, HY_MAX_DECAY, HY_WIDTH, dtype=F32))
    side_cols = HY_ORDER * HY_WIDTH

    def one_side(feats, t, w3_side):
        hid = jnp.sin(freq * (jnp.dot(feats, w1, precision=hp) + b1))
        hid = jnp.sin(freq * (jnp.dot(hid, w2, precision=hp) + b2))
        filt = jnp.dot(hid, w3_side, precision=hp).reshape(n, HY_ORDER, HY_WIDTH)
        return filt * jnp.exp(-t * decay)[:, None, :]

    fwd = one_side(feats, t, w3[:, :side_cols])
    bwd_flipped = one_side(feats[::-1], t[::-1], w3[:, side_cols:])
    two_sided = jnp.concatenate([fwd, jnp.zeros_like(fwd[:1]), bwd_flipped[:n - 1]], axis=0)
    return two_sided / jnp.sum(jnp.abs(two_sided), axis=0, keepdims=True)


DFT_BLOCK_BYTES = 4 * 1024 * 1024
DFT_MINOR_LEN = 128
DFT_SINGLE_LEVEL_MAX = 1024


def _bmm3_body(lh_ref, ll_ref, r_ref, o_ref, *, nb, shared):
    for b in range(nb):
        r = r_ref[b]
        r_hi = r.astype(BF16)
        r_lo = (r - r_hi.astype(F32)).astype(BF16)
        lh = lh_ref[0 if shared else b]
        ll = ll_ref[0 if shared else b]
        o_ref[b] = (jnp.dot(lh, r_hi, preferred_element_type=F32)
                    + jnp.dot(lh, r_lo, preferred_element_type=F32)
                    + jnp.dot(ll, r_hi, preferred_element_type=F32))


def _bmm3(l_hi, l_lo, r, nb):
    batch, k, c = r.shape
    m = l_hi.shape[1]
    shared = l_hi.shape[0] == 1
    nb = max(1, min(nb, batch, DFT_BLOCK_BYTES // (4 * c * max(k, m))))
    assert batch % nb == 0
    l_spec = pl.BlockSpec((1 if shared else nb, m, k), (lambda i: (0, 0, 0)) if shared else (lambda i: (i, 0, 0)))
    return pl.pallas_call(
        partial(_bmm3_body, nb=nb, shared=shared),
        grid=(batch // nb,),
        in_specs=[l_spec, l_spec, pl.BlockSpec((nb, k, c), lambda i: (i, 0, 0))],
        out_specs=pl.BlockSpec((nb, m, c), lambda i: (i, 0, 0)),
        out_shape=jax.ShapeDtypeStruct((batch, m, c), F32),
        compiler_params=pltpu.CompilerParams(dimension_semantics=("parallel",), vmem_limit_bytes=VMEM_LIMIT_BYTES),
        name="dft_matmul",
    )(l_hi, l_lo, r)


def _spectral_body(fh_ref, fl_ref, ih_ref, il_ref, r_ref, h_ref, o_ref, *, nb):
    n2 = h_ref.shape[1] // 2
    for b in range(nb):
        y = _split_dot_left(fh_ref[0], fl_ref[0], r_ref[b])
        yr, yi = y[:n2], y[n2:]
        hr, hi = h_ref[b, :n2], h_ref[b, n2:]
        y = jnp.concatenate([yr * hr - yi * hi, yr * hi + yi * hr], axis=0)
        o_ref[b] = _split_dot_left(ih_ref[0], il_ref[0], y)


def _split_dot_left(l_hi, l_lo, r):
    r_hi = r.astype(BF16)
    r_lo = (r - r_hi.astype(F32)).astype(BF16)
    return (jnp.dot(l_hi, r_hi, preferred_element_type=F32) + jnp.dot(l_hi, r_lo, preferred_element_type=F32)
            + jnp.dot(l_lo, r_hi, preferred_element_type=F32))


def _spectral_stage(fwd, inv, r, spec, nb):
    batch, k, c = r.shape
    nb = max(1, min(nb, batch, DFT_BLOCK_BYTES // (4 * c * k)))
    assert batch % nb == 0
    mat = pl.BlockSpec((1, k, k), lambda i: (0, 0, 0))
    blk = pl.BlockSpec((nb, k, c), lambda i: (i, 0, 0))
    return pl.pallas_call(
        partial(_spectral_body, nb=nb), grid=(batch // nb,),
        in_specs=[mat, mat, mat, mat, blk, blk], out_specs=blk,
        out_shape=jax.ShapeDtypeStruct((batch, k, c), F32),
        compiler_params=pltpu.CompilerParams(dimension_semantics=("parallel",), vmem_limit_bytes=VMEM_LIMIT_BYTES),
        name="dft_spectral_stage",
    )(*_split_bf16(fwd), *_split_bf16(inv), r, spec)


def _stack_complex(f):
    return np.concatenate([np.concatenate([f.real, -f.imag], axis=-1),
                           np.concatenate([f.imag, f.real], axis=-1)], axis=-2)


def _split_bf16(a):
    a = jnp.asarray(a, F32)
    hi = a.astype(BF16)
    return hi, (a - hi.astype(F32)).astype(BF16)


@lru_cache(maxsize=None)
def _dft_tables(n1_io, n1, n2):
    n = n1 * n2
    k1 = np.arange(n1)
    a = np.arange(n1_io)
    b = np.arange(n2)
    ang = -2.0 * np.pi * (((np.outer(k1, a) * n2)[None] + b[:, None, None] * k1[None, :, None]) % n) / n
    fa = np.exp(1j * ang)
    fb = np.exp(-2j * np.pi * (np.outer(b, b) % n2) / n2)
    return dict(
        fwd_a=_stack_complex(fa),
        fwd_a_real=np.concatenate([fa.real, fa.imag], axis=-2),
        fwd_b=_stack_complex(fb)[None],
        inv_b=_stack_complex(np.conj(fb).T)[None],
        inv_a=_stack_complex(np.conj(fa).transpose(0, 2, 1) / n),
    )


def _dft_split(length):
    if length <= DFT_SINGLE_LEVEL_MAX:
        return 1, length
    return length // DFT_MINOR_LEN, DFT_MINOR_LEN


def _swap_stage(x, n1, n2):
    c = x.shape[-1]
    return x.reshape(n2, 2, n1, c).transpose(2, 1, 0, 3).reshape(n1, 2 * n2, c)


def _dft_forward(rows, table, n1, n2, tabs):
    if n1 > 1:
        rows = _bmm3(*_split_bf16(tabs[table]), rows, 16)
    return _bmm3(*_split_bf16(tabs['fwd_b']), _swap_stage(rows, n1, n2), 8)


def _filter_spectrum(two_sided):
    length, c = two_sided.shape
    n1, n2 = _dft_split(length)
    tabs = _dft_tables(n1, n1, n2)
    rows = two_sided.reshape(n1, n2, c).transpose(1, 0, 2)
    if n1 == 1:
        rows = jnp.concatenate([rows, jnp.zeros_like(rows)], axis=1)
    return _dft_forward(rows, 'fwd_a_real', n1, n2, tabs)


def _long_conv_pair(z, spec):
    pair, n, c = z.shape
    assert pair == 2
    n1, n2 = _dft_split(2 * n)
    n1_io = max(n1 // 2, 1)
    tabs = _dft_tables(n1_io, n1, n2)
    if n1 > 1:
        rows = z.reshape(2, n1_io, n2, c).transpose(2, 0, 1, 3).reshape(n2, 2 * n1_io, c)
    else:
        rows = jnp.pad(z, ((0, 0), (0, n), (0, 0))).transpose(1, 0, 2)
    if n1 > 1:
        rows = _bmm3(*_split_bf16(tabs['fwd_a']), rows, 16)
    d = _spectral_stage(tabs['fwd_b'], tabs['inv_b'], _swap_stage(rows, n1, n2), spec, 4)
    d = _swap_stage(d, n2, n1)
    if n1 > 1:
        out = _bmm3(*_split_bf16(tabs['inv_a']), d, 16)
        return out.reshape(n2, 2, n1_io, c).transpose(1, 2, 0, 3).reshape(2, n, c)
    return (d.reshape(n2, 2, c).transpose(1, 0, 2) / (n1 * n2))[:, :n]


def _hyena_mixer(u, conv_w, conv_b, w1, b1, w2, b2, w3, freq, skip):
    n = u.shape[1]
    u = _depthwise_conv(u, conv_w, conv_b, HY_CONV // 2, HY_CONV - 1 - HY_CONV // 2)
    gates = (u[..., :HY_WIDTH], u[..., HY_WIDTH:2 * HY_WIDTH])
    z = u[..., 2 * HY_WIDTH:]
    two_sided = _hyena_filter(n, w1, b1, w2, b2, w3, freq)
    spec = _filter_spectrum(two_sided.reshape(2 * n, HY_ORDER * HY_WIDTH))
    for o in range(HY_ORDER):
        zc = _long_conv_pair(z, spec[..., o * HY_WIDTH:(o + 1) * HY_WIDTH])
        z = gates[o] * (zc + skip[o] * z)
    return z


LRU_CONV_MARGIN = SUBLANES
LRU_TIME_CHUNK = 1024


def _gelu_tanh(x):
    return 0.5 * x * (1.0 + jnp.tanh(math.sqrt(2.0 / math.pi) * (x + 0.044715 * (x * x * x))))


def _rglru_body(ug_ref, ur_ref, cw_ref, cb_ref, wa_ref, ba_ref, wx_ref, bx_ref, cl_ref, o_ref,
                pad_s, af_s, bf_s, ar_s, br_s, hf_s, hr_s, *, n_ctx, n_lat):
    a_s = (af_s, ar_s)
    b_s = (bf_s, br_s)
    mg = LRU_CONV_MARGIN
    cw = cw_ref[...]
    zeros_margin = jnp.zeros((mg, LANES), F32)
    for r0, n in ((0, n_ctx), (n_ctx, n_lat)):
        pad_s[0:mg, :] = zeros_margin
        pad_s[mg:mg + n, :] = ur_ref[0, r0:r0 + n, :]
        pad_s[mg + n:2 * mg + n, :] = zeros_margin
        chunk = min(n, LRU_TIME_CHUNK)
        for c0 in range(0, n, chunk):
            xr = cb_ref[...] + cw[0:1] * pad_s[mg + c0 - 2:mg + c0 - 2 + chunk, :]
            for j in range(1, LRU_CONV):
                xr = xr + cw[j:j + 1] * pad_s[mg + c0 + j - 2:mg + c0 + j - 2 + chunk, :]
            xb = xr.astype(BF16)
            for d in range(2):
                r = jax.nn.sigmoid(jnp.dot(xb, wa_ref[d, 0], preferred_element_type=F32) + ba_ref[d:d + 1, :])
                i = jax.nn.sigmoid(jnp.dot(xb, wx_ref[d, 0], preferred_element_type=F32) + bx_ref[d:d + 1, :])
                log_a = cl_ref[d:d + 1, :] * r
                a_s[d][r0 + c0:r0 + c0 + chunk, :] = jnp.exp(log_a)
                b_s[d][r0 + c0:r0 + c0 + chunk, :] = jnp.sqrt(1.0 - jnp.exp(2.0 * log_a)) * (i * xr)

    def scan_segment(r0, n, carry):
        def step(t, hc):
            hf, hr = hc
            tf = r0 + t
            tr = r0 + n - 1 - t
            hf = af_s[pl.ds(tf, 1), :] * hf + bf_s[pl.ds(tf, 1), :]
            hr = ar_s[pl.ds(tr, 1), :] * hr + br_s[pl.ds(tr, 1), :]
            hf_s[pl.ds(tf, 1), :] = hf
            hr_s[pl.ds(tr, 1), :] = hr
            return hf, hr
        return lax.fori_loop(0, n, step, carry, unroll=8)

    h0 = jnp.zeros((1, LANES), F32)
    carry = scan_segment(0, n_ctx, (h0, h0))
    scan_segment(n_ctx, n_lat, carry)
    o_ref[0] = ((hf_s[...] + hr_s[...]) * _gelu_tanh(ug_ref[0])).astype(o_ref.dtype)


def _block_diag_tiles(w):
    w = w.reshape(2, LRU_HEADS // 2, 2, LRU_HEAD_DIM, LRU_HEAD_DIM)
    z = jnp.zeros_like(w[:, :, 0])
    top = jnp.concatenate([w[:, :, 0], z], axis=-1)
    bot = jnp.concatenate([z, w[:, :, 1]], axis=-1)
    return jnp.concatenate([top, bot], axis=-2).astype(BF16)


def _rglru_mixer(u_all, n_ctx, conv_w, conv_b, wa, ba, wx, bx, lam):
    bsz, n_all, _ = u_all.shape
    n_lat = n_all - n_ctx
    n_tiles = LRU_WIDTH // LANES
    gate_blk0 = HY_IN // LANES
    rec_blk0 = (HY_IN + LRU_WIDTH) // LANES
    decay_scale = -LRU_C * jax.nn.softplus(-lam)
    seq = lambda col0: pl.BlockSpec((1, n_all, LANES), lambda b, j: (b, 0, col0 + j))
    vec = lambda rows: pl.BlockSpec((rows, LANES), lambda b, j: (0, j))
    mat = pl.BlockSpec((2, 1, LANES, LANES), lambda b, j: (0, j, 0, 0))
    full = pltpu.VMEM((n_all, LANES), F32)
    return pl.pallas_call(
        partial(_rglru_body, n_ctx=n_ctx, n_lat=n_lat),
        grid=(bsz, n_tiles),
        in_specs=[seq(gate_blk0), seq(rec_blk0), vec(LRU_CONV), vec(1), mat, vec(2), mat, vec(2), vec(2)],
        out_specs=pl.BlockSpec((1, n_all, LANES), lambda b, j: (b, 0, j)),
        out_shape=jax.ShapeDtypeStruct((bsz, n_all, LRU_WIDTH), BF16),
        scratch_shapes=[pltpu.VMEM((n_lat + 2 * LRU_CONV_MARGIN, LANES), F32),
                        full, full, full, full, full, full],
        compiler_params=pltpu.CompilerParams(
            dimension_semantics=("parallel", "parallel"),
            vmem_limit_bytes=VMEM_LIMIT_BYTES),
        name="rglru",
    )(u_all, u_all, conv_w, conv_b.reshape(1, LRU_WIDTH), _block_diag_tiles(wa), ba,
      _block_diag_tiles(wx), bx, decay_scale)


ATT_Q_TILE = 256
ATT_KV_CHUNK = 2048


def _attn_body(q_ref, k_ref, v_ref, lv_ref, g_ref, o_ref, *, lam_init):
    lv = lv_ref[...]
    lam = (jnp.exp(jnp.sum(lv[0:1] * lv[1:2], keepdims=True))
           - jnp.exp(jnp.sum(lv[2:3] * lv[3:4], keepdims=True)) + lam_init)
    q = q_ref[0]
    nk = k_ref.shape[1]
    work = [(m, c0, min(c0 + ATT_KV_CHUNK, nk)) for c0 in range(0, nk, ATT_KV_CHUNK) for m in range(2)]

    def scores(m, c0, c1):
        qm = q[:, m * ATT_HEAD_DIM:(m + 1) * ATT_HEAD_DIM]
        km = k_ref[0, c0:c1, m * ATT_HEAD_DIM:(m + 1) * ATT_HEAD_DIM]
        return lax.dot_general(qm, km, (((1,), (1,)), ((), ())), preferred_element_type=F32)

    state = [None, None]
    s_next = scores(*work[0])
    for j, (m, c0, c1) in enumerate(work):
        s = s_next
        if j + 1 < len(work):
            s_next = scores(*work[j + 1])
        chunk_max = jnp.max(s, axis=-1, keepdims=True)
        v = v_ref[0, c0:c1, :]
        if state[m] is None:
            p = jnp.exp(s - chunk_max)
            state[m] = (chunk_max, jnp.sum(p, axis=-1, keepdims=True),
                        jnp.dot(p.astype(BF16), v, preferred_element_type=F32))
        else:
            mx_old, sum_old, acc_old = state[m]
            mx = jnp.maximum(mx_old, chunk_max)
            rescale = jnp.exp(mx_old - mx)
            p = jnp.exp(s - mx)
            state[m] = (mx, rescale * sum_old + jnp.sum(p, axis=-1, keepdims=True),
                        rescale * acc_old + jnp.dot(p.astype(BF16), v, preferred_element_type=F32))
    o = state[0][2] / state[0][1] - lam * (state[1][2] / state[1][1])
    y = o * lax.rsqrt(jnp.mean(o * o, axis=-1, keepdims=True) + LN_EPS) * g_ref[...] * (1.0 - lam_init)
    o_ref[0] = y.astype(o_ref.dtype)


def _diff_attention(qkv, lam_vecs, gain, lam_init, q_row0, n_q, n_k):
    bsz = qkv.shape[0]
    tq = ATT_Q_TILE
    q_blk0 = q_row0 // tq
    n_heads = ATT_HEADS
    return pl.pallas_call(
        partial(_attn_body, lam_init=lam_init),
        grid=(bsz, n_heads, n_q // tq),
        in_specs=[
            pl.BlockSpec((1, tq, ATT_V_DIM), lambda b, h, i: (b, q_blk0 + i, h)),
            pl.BlockSpec((1, n_k, ATT_V_DIM), lambda b, h, i: (b, 0, n_heads + h)),
            pl.BlockSpec((1, n_k, ATT_V_DIM), lambda b, h, i: (b, 0, 2 * n_heads + h)),
            pl.BlockSpec((4, ATT_HEAD_DIM), lambda b, h, i: (0, 0)),
            pl.BlockSpec((1, ATT_V_DIM), lambda b, h, i: (0, 0)),
        ],
        out_specs=pl.BlockSpec((1, tq, ATT_V_DIM), lambda b, h, i: (b, i, h)),
        out_shape=jax.ShapeDtypeStruct((bsz, n_q, ATT_WIDTH), BF16),
        compiler_params=pltpu.CompilerParams(
            dimension_semantics=("parallel", "parallel", "arbitrary"),
            vmem_limit_bytes=VMEM_LIMIT_BYTES),
        name="diff_attention",
    )(qkv, qkv, qkv, lam_vecs, gain.reshape(1, ATT_V_DIM))


def _axial_rope_tables(n):
    rows = n // GRID_W
    row = jnp.broadcast_to(jnp.arange(rows, dtype=F32)[:, None], (rows, GRID_W)).reshape(-1)
    col = jnp.broadcast_to(jnp.arange(GRID_W, dtype=F32)[None, :], (rows, GRID_W)).reshape(-1)
    inv = ROPE_THETA ** (-jnp.arange(0, ROPE_AXIS_DIM, 2, dtype=F32) / ROPE_AXIS_DIM)
    ang = jnp.stack([row[:, None] * inv, col[:, None] * inv], axis=1)
    return jnp.cos(ang), jnp.sin(ang)


def _apply_axial_rope(x, cos, sin):
    xs = x.reshape(x.shape[:-1] + (2, 2, ROPE_AXIS_DIM // 2))
    x1, x2 = xs[..., 0, :], xs[..., 1, :]
    cb, sb = cos[None, :, None, None], sin[None, :, None, None]
    out = jnp.stack([x1 * cb - x2 * sb, x2 * cb + x1 * sb], axis=-2)
    return out.reshape(x.shape)


def _diff_attention_mixer(u_att, n_ctx, lam_vecs, subln_gain, lam_init, need_ctx):
    bsz, n_all, _ = u_att.shape
    n_lat = n_all - n_ctx
    scale = ATT_HEAD_DIM ** -0.5
    u_lat = u_att[:, n_ctx:]
    u_ctx = u_att[:, :n_ctx]
    q_l = u_lat[..., :ATT_QK].reshape(bsz, n_lat, ATT_HEADS, 2, ATT_HEAD_DIM)
    k_l = u_lat[..., ATT_QK:2 * ATT_QK].reshape(bsz, n_lat, ATT_HEADS, 2, ATT_HEAD_DIM)
    cos, sin = _axial_rope_tables(n_lat)
    q_l = _apply_axial_rope(q_l, cos, sin) * scale
    k_l = _apply_axial_rope(k_l, cos, sin)
    lat = jnp.concatenate([q_l.reshape(bsz, n_lat, ATT_QK), k_l.reshape(bsz, n_lat, ATT_QK),
                           u_lat[..., 2 * ATT_QK:]], axis=-1).astype(BF16)
    cx = jnp.concatenate([u_ctx[..., :ATT_QK] * scale, u_ctx[..., ATT_QK:]], axis=-1).astype(BF16)
    qkv = jnp.concatenate([cx, lat], axis=1)
    y_lat = _diff_attention(qkv, lam_vecs, subln_gain, lam_init, n_ctx, n_lat, n_all)
    y_ctx = _diff_attention(qkv, lam_vecs, subln_gain, lam_init, 0, n_ctx, n_ctx) if need_ctx else None
    return y_lat, y_ctx


ROW_TILE = 256
IN_PROJ_COL_TILE = 1408
ADA_COL_TILE = 1536
COMBINE_ROW_TILE = 128


def _ln_rows(x):
    mu = jnp.mean(x, axis=-1, keepdims=True)
    xc = x - mu
    return xc * lax.rsqrt(jnp.mean(xc * xc, axis=-1, keepdims=True) + LN_EPS)


def _segments(bsz, seg_rows, tile):
    ids = []
    for b in range(bsz):
        for rows, kind in seg_rows:
            assert rows % tile == 0
            ids += [bsz if kind == 'ctx' else b] * (rows // tile)
    return jnp.asarray(ids, jnp.int32)


def _ada_body(c_ref, w_ref, b_ref, o_ref):
    c = c_ref[...]
    w = w_ref[0]
    w_hi = w.astype(BF16)
    w_lo = (w - w_hi.astype(F32)).astype(BF16)
    o_ref[...] = _split_dot(c * jax.nn.sigmoid(c), w_hi, w_lo) + b_ref[0]


def _ada_modulation(cond, ada_w, ada_b, layer):
    s, d = cond.shape
    n = ada_w.shape[-1]
    rows = -(-s // SUBLANES) * SUBLANES
    tn = ADA_COL_TILE
    assert n % tn == 0
    out = pl.pallas_call(
        _ada_body, grid=(n // tn,),
        in_specs=[pl.BlockSpec((rows, d), lambda j: (0, 0)),
                  pl.BlockSpec((1, d, tn), lambda j: (layer, 0, j)),
                  pl.BlockSpec((1, 1, tn), lambda j: (layer, 0, j))],
        out_specs=pl.BlockSpec((rows, tn), lambda j: (0, j)),
        out_shape=jax.ShapeDtypeStruct((rows, n), F32),
        compiler_params=pltpu.CompilerParams(dimension_semantics=("parallel",),
                                             vmem_limit_bytes=VMEM_LIMIT_BYTES),
        name="ada_modulation",
    )(jnp.pad(cond, ((0, rows - s), (0, 0))), ada_w, ada_b.reshape(ada_b.shape[0], 1, n))
    return out[:s]


def _ln_mm_body(seg_ref, x_ref, mod_ref, w_ref, o_ref):
    m = mod_ref[0]
    h = _ln_rows(x_ref[...]) * (1.0 + m[1:2]) + m[0:1]
    o_ref[...] = jnp.dot(h.astype(BF16), w_ref[...], preferred_element_type=F32)


def _ln_mod_matmul(x, mod, seg, w):
    t, d = x.shape
    n = w.shape[1]
    tm, tn = ROW_TILE, IN_PROJ_COL_TILE
    assert t % tm == 0 and n % tn == 0
    grid_spec = pltpu.PrefetchScalarGridSpec(
        num_scalar_prefetch=1,
        grid=(n // tn, t // tm),
        in_specs=[pl.BlockSpec((tm, d), lambda j, i, s: (i, 0)),
                  pl.BlockSpec((1, 2, d), lambda j, i, s: (s[i], 0, 0)),
                  pl.BlockSpec((d, tn), lambda j, i, s: (0, j))],
        out_specs=pl.BlockSpec((tm, tn), lambda j, i, s: (i, j)),
    )
    return pl.pallas_call(
        _ln_mm_body, grid_spec=grid_spec,
        out_shape=jax.ShapeDtypeStruct((t, n), F32),
        compiler_params=pltpu.CompilerParams(dimension_semantics=("parallel", "arbitrary"),
                                             vmem_limit_bytes=VMEM_LIMIT_BYTES),
        name="ln_mod_in_proj",
    )(seg, x, mod, w)


def _split_dot(a, b_hi, b_lo):
    a_hi = a.astype(BF16)
    a_lo = (a - a_hi.astype(F32)).astype(BF16)
    return (jnp.dot(a_hi, b_hi, preferred_element_type=F32) + jnp.dot(a_hi, b_lo, preferred_element_type=F32)
            + jnp.dot(a_lo, b_hi, preferred_element_type=F32))


def _out_ln_body(seg_ref, mixed_ref, x_ref, mod_ref, w_ref, lng_ref, rwh_ref, rwl_ref, rb_ref,
                 x1_ref, h2_ref, h2b_ref, ti_ref, tw_ref):
    m = mod_ref[0]
    mix = jnp.dot(mixed_ref[...], w_ref[...], preferred_element_type=F32)
    x1 = _ln_rows(DEEPNORM_ALPHA * x_ref[...] + m[0:1] * mix) * lng_ref[0:1] + lng_ref[1:2]
    x1_ref[...] = x1
    h2 = _ln_rows(x1) * (1.0 + m[2:3]) + m[1:2]
    h2_ref[...] = h2
    h2b_ref[...] = h2.astype(BF16)
    scores = jax.nn.sigmoid(_split_dot(h2, rwh_ref[...], rwl_ref[...]))
    lane = lax.broadcasted_iota(jnp.int32, scores.shape, 1).astype(F32)
    biased = jnp.where(lane < N_EXPERTS, scores + rb_ref[...], -jnp.inf)
    top_i = jnp.zeros(scores.shape, F32)
    top_s = jnp.zeros(scores.shape, F32)
    for k in range(TOP_K):
        best = jnp.max(biased, axis=-1, keepdims=True)
        pick = jnp.min(jnp.where(biased == best, lane, float(LANES)), axis=-1, keepdims=True)
        hit = lane == pick
        top_i = jnp.where(lane == k, pick, top_i)
        top_s = jnp.where(lane == k, jnp.sum(jnp.where(hit, scores, 0.0), axis=-1, keepdims=True), top_s)
        biased = jnp.where(hit, -jnp.inf, biased)
    ti_ref[...] = top_i.astype(jnp.int32)
    tw_ref[...] = top_s / jnp.sum(top_s, axis=-1, keepdims=True) * ROUTED_SCALE


def _out_proj_norm_route(mixed, x, mod, seg, w, ln_gb, router_w, router_b):
    t, d = x.shape
    tm = ROW_TILE
    rw = jnp.pad(router_w, ((0, 0), (0, LANES - N_EXPERTS)))
    rw_hi, rw_lo = _split_bf16(rw)
    rb = jnp.pad(router_b, (0, LANES - N_EXPERTS)).reshape(1, LANES)
    row = lambda width: pl.BlockSpec((tm, width), lambda i, s: (i, 0))
    const = lambda shape: pl.BlockSpec(shape, lambda i, s: (0,) * len(shape))
    grid_spec = pltpu.PrefetchScalarGridSpec(
        num_scalar_prefetch=1,
        grid=(t // tm,),
        in_specs=[row(d), row(d), pl.BlockSpec((1, 3, d), lambda i, s: (s[i], 0, 0)), const((d, d)),
                  const((2, d)), const((d, LANES)), const((d, LANES)), const((1, LANES))],
        out_specs=[row(d), row(d), row(d), row(LANES), row(LANES)],
    )
    x1, h2, h2b, top_i, top_w = pl.pallas_call(
        _out_ln_body, grid_spec=grid_spec,
        out_shape=[jax.ShapeDtypeStruct((t, d), F32), jax.ShapeDtypeStruct((t, d), F32),
                   jax.ShapeDtypeStruct((t, d), BF16), jax.ShapeDtypeStruct((t, LANES), jnp.int32),
                   jax.ShapeDtypeStruct((t, LANES), F32)],
        compiler_params=pltpu.CompilerParams(dimension_semantics=("parallel",),
                                             vmem_limit_bytes=VMEM_LIMIT_BYTES),
        name="out_proj_norm_route",
    )(seg, mixed, x, mod, w, ln_gb, rw_hi, rw_lo, rb)
    return x1, h2, h2b, top_i[:, :TOP_K], top_w[:, :TOP_K]


def _shared_body(h_ref, wgu_ref, wd_ref, o_ref):
    gu = jnp.dot(h_ref[...], wgu_ref[...], preferred_element_type=F32)
    f = wd_ref.shape[0]
    g, u = gu[:, :f], gu[:, f:]
    act = (g * jax.nn.sigmoid(g) * u).astype(BF16)
    o_ref[...] = jnp.dot(act, wd_ref[...], preferred_element_type=F32)


def _shared_expert(hb, w_gate, w_up, w_down):
    t, d = hb.shape
    f = w_gate.shape[1]
    tm = ROW_TILE
    wgu = jnp.concatenate([w_gate, w_up], axis=1).astype(BF16)
    return pl.pallas_call(
        _shared_body, grid=(t // tm,),
        in_specs=[pl.BlockSpec((tm, d), lambda i: (i, 0)), pl.BlockSpec((d, 2 * f), lambda i: (0, 0)),
                  pl.BlockSpec((f, d), lambda i: (0, 0))],
        out_specs=pl.BlockSpec((tm, d), lambda i: (i, 0)),
        out_shape=jax.ShapeDtypeStruct((t, d), F32),
        compiler_params=pltpu.CompilerParams(dimension_semantics=("parallel",),
                                             vmem_limit_bytes=VMEM_LIMIT_BYTES),
        name="shared_expert",
    )(hb, wgu, w_down.astype(BF16))


def _combine_body(seg_ref, *refs):
    y_refs, (sh_ref, x_ref, mod_ref, lng_ref, o_ref) = refs[:TOP_K], refs[TOP_K:]
    ffn = sh_ref[...]
    for y_ref in y_refs:
        ffn = ffn + y_ref[...]
    o_ref[...] = _ln_rows(DEEPNORM_ALPHA * x_ref[...] + mod_ref[0] * ffn) * lng_ref[0:1] + lng_ref[1:2]


def _combine_norm(y, shared, x1, gate, seg, ln_gb):
    t, d = x1.shape
    tm = COMBINE_ROW_TILE
    tiles = t // tm
    row = pl.BlockSpec((tm, d), lambda i, s: (i, 0))
    y_specs = [pl.BlockSpec((tm, d), lambda i, s, k=k: (k * tiles + i, 0)) for k in range(TOP_K)]
    grid_spec = pltpu.PrefetchScalarGridSpec(
        num_scalar_prefetch=1,
        grid=(tiles,),
        in_specs=y_specs + [row, row, pl.BlockSpec((1, 1, d), lambda i, s: (s[i], 0, 0)),
                            pl.BlockSpec((2, d), lambda i, s: (0, 0))],
        out_specs=row,
    )
    return pl.pallas_call(
        _combine_body, grid_spec=grid_spec,
        out_shape=jax.ShapeDtypeStruct((t, d), F32),
        compiler_params=pltpu.CompilerParams(dimension_semantics=("parallel",),
                                             vmem_limit_bytes=VMEM_LIMIT_BYTES),
        name="combine_norm",
    )(seg, *([y] * TOP_K), shared, x1, gate, ln_gb)


MOE_BLOCK_ROWS = 256
MOE_VMEM_LIMIT_BYTES = 56 * 1024 * 1024
MOE_OUTBOUND_DMA_PRIORITY = 1
MOE_WEIGHT_DMA_PRIORITY = 1


def _moe_body(be_ref, nu_ref, ne_ref, idx_ref, idx_nxt_ref, rw_ref, wg_hbm, wu_hbm, wd_hbm, h_hbm, o_hbm,
              xbuf, ybuf, wg_f, wu_f, wd_f, wg_s, wu_s, wd_s, gsem, ssem, wsem, *, bm, layer):
    i = pl.program_id(0)
    n_used = nu_ref[0]
    slot = i % 2

    def weight_copies(expert):
        return (pltpu.make_async_copy(wg_hbm.at[layer, expert], wg_f, wsem.at[0]),
                pltpu.make_async_copy(wu_hbm.at[layer, expert], wu_f, wsem.at[1]),
                pltpu.make_async_copy(wd_hbm.at[layer, expert], wd_f, wsem.at[2]))

    def gather_rows(tok_ref, dst_slot):
        for r in range(bm):
            pltpu.make_async_copy(h_hbm.at[pl.ds(tok_ref[0, 0, r], 1)], xbuf.at[dst_slot, pl.ds(r, 1)],
                                  gsem.at[dst_slot]).start()

    def wait_gather(s):
        pltpu.make_async_copy(h_hbm.at[pl.ds(0, bm)], xbuf.at[s], gsem.at[s]).wait()

    def wait_scatter(s):
        pltpu.make_async_copy(ybuf.at[s], o_hbm.at[pl.ds(0, bm)], ssem.at[s]).wait()

    e = be_ref[i]
    e_prev = be_ref[jnp.maximum(i - 1, 0)]

    @pl.when(i == 0)
    def _():
        gather_rows(idx_ref, 0)
        for copy in weight_copies(e):
            copy.start(priority=MOE_WEIGHT_DMA_PRIORITY)

    @pl.when((i < n_used) & ((i == 0) | (e != e_prev)))
    def _():
        for copy in weight_copies(e):
            copy.wait()
        wg_s[...] = wg_f[...].astype(BF16)
        wu_s[...] = wu_f[...].astype(BF16)
        wd_s[...] = wd_f[...].astype(BF16)

        @pl.when(ne_ref[i] >= 0)
        def _():
            for copy in weight_copies(ne_ref[i]):
                copy.start(priority=MOE_WEIGHT_DMA_PRIORITY)

    def block_step(s):
        wait_gather(s)

        @pl.when(i >= 2)
        def _():
            wait_scatter(s)

        gather_rows(idx_nxt_ref, 1 - s)

        xb = xbuf[s].astype(BF16)
        g = jnp.dot(xb, wg_s[...], preferred_element_type=F32)
        u = jnp.dot(xb, wu_s[...], preferred_element_type=F32)
        act = (g * jax.nn.sigmoid(g) * u).astype(BF16)
        ybuf[s] = jnp.dot(act, wd_s[...], preferred_element_type=F32) * rw_ref[...]
        for r in range(bm):
            pltpu.make_async_copy(ybuf.at[s, pl.ds(r, 1)], o_hbm.at[pl.ds(idx_ref[0, 0, bm + r], 1)],
                                  ssem.at[s]).start(priority=MOE_OUTBOUND_DMA_PRIORITY)

        @pl.when(i == n_used - 1)
        def _():
            @pl.when(i >= 1)
            def _():
                wait_scatter(1 - s)
            wait_scatter(s)
            wait_gather(1 - s)

    for s in range(2):
        pl.when((i < n_used) & (slot == s))(partial(block_step, s))

    @pl.when(i >= n_used)
    def _():
        xbuf[0] = jnp.zeros(xbuf.shape[1:], F32)
        fill = pltpu.make_async_copy(xbuf.at[0], o_hbm.at[pl.ds(pl.multiple_of(i * bm, bm), bm)], gsem.at[0])
        fill.start()
        fill.wait()


def _routed_experts(h, idx, row_w, block_e, n_used, next_e, w_gate, w_up, w_down, layer):
    bm = MOE_BLOCK_ROWS
    n_blocks = idx.shape[0]
    d = h.shape[1]
    f = w_gate.shape[-1]
    anywhere = pl.BlockSpec(memory_space=pl.ANY)
    grid_spec = pltpu.PrefetchScalarGridSpec(
        num_scalar_prefetch=3,
        grid=(n_blocks,),
        in_specs=[
            pl.BlockSpec((1, 1, 2 * bm), lambda i, be, nu, ne: (i, 0, 0), memory_space=pltpu.SMEM),
            pl.BlockSpec((1, 1, 2 * bm), lambda i, be, nu, ne: (jnp.minimum(i + 1, n_blocks - 1), 0, 0),
                         memory_space=pltpu.SMEM),
            pl.BlockSpec((bm, 1), lambda i, be, nu, ne: (i, 0)),
            anywhere, anywhere, anywhere, anywhere,
        ],
        out_specs=anywhere,
        scratch_shapes=[pltpu.VMEM((2, bm, d), F32), pltpu.VMEM((2, bm, d), F32),
                        pltpu.VMEM((d, f), F32), pltpu.VMEM((d, f), F32), pltpu.VMEM((f, d), F32),
                        pltpu.VMEM((d, f), BF16), pltpu.VMEM((d, f), BF16), pltpu.VMEM((f, d), BF16),
                        pltpu.SemaphoreType.DMA((2,)), pltpu.SemaphoreType.DMA((2,)),
                        pltpu.SemaphoreType.DMA((3,))],
    )
    return pl.pallas_call(
        partial(_moe_body, bm=bm, layer=layer),
        grid_spec=grid_spec,
        out_shape=jax.ShapeDtypeStruct((n_blocks * bm, d), F32),
        compiler_params=pltpu.CompilerParams(
            dimension_semantics=("arbitrary",),
            vmem_limit_bytes=MOE_VMEM_LIMIT_BYTES),
        name="routed_experts",
    )(block_e, n_used, next_e, idx, idx, row_w, w_gate, w_up, w_down, h)


def _route(top_idx, top_w):
    bm = MOE_BLOCK_ROWS
    n_tok = top_idx.shape[0]
    n_assign = n_tok * TOP_K
    flat_e = top_idx.reshape(-1)
    flat_w = top_w.reshape(-1)
    order = jnp.argsort(flat_e).astype(jnp.int32)
    counts = jnp.sum(flat_e[None, :] == jnp.arange(N_EXPERTS)[:, None], axis=1).astype(jnp.int32)
    padded = (counts + bm - 1) // bm * bm
    padded_end = jnp.cumsum(padded)
    live_end = jnp.cumsum(counts)
    shift = padded_end - padded - (live_end - counts)
    n_blocks = -(-n_assign // bm) + N_EXPERTS
    block_row0 = jnp.arange(n_blocks, dtype=jnp.int32) * bm
    block_e = jnp.minimum(jnp.sum(padded_end[None, :] <= block_row0[:, None], axis=1),
                          N_EXPERTS - 1).astype(jnp.int32)
    n_used = (padded_end[-1:] // bm).astype(jnp.int32)
    experts = jnp.arange(N_EXPERTS, dtype=jnp.int32)
    later = jnp.where((experts[None, :] > experts[:, None]) & (counts[None, :] > 0), experts[None, :], N_EXPERTS)
    next_live = jnp.min(later, axis=1)
    next_e = jnp.where(next_live < N_EXPERTS, next_live, -1)[block_e].astype(jnp.int32)
    rows = jnp.arange(n_blocks * bm, dtype=jnp.int32).reshape(n_blocks, bm)
    row_slot = rows - shift[block_e][:, None]
    row_live = row_slot < live_end[block_e][:, None]
    row_assign = order[jnp.where(row_live, row_slot, 0)]
    row_tok = jnp.where(row_live, row_assign // TOP_K, 0)
    row_dst = jnp.where(row_live, (row_assign % TOP_K) * n_tok + row_tok,
                        n_assign + rows - live_end[block_e][:, None])
    row_w = jnp.where(row_live, flat_w[row_assign], 0.0)
    idx = jnp.concatenate([row_tok, row_dst], axis=1).astype(jnp.int32)[:, None, :]
    return idx, row_w.reshape(-1, 1), block_e, n_used, next_e


def kernel(x, c, ctx, c_ctx, ada_w, ada_b, w_in, hy_conv_w, hy_conv_b, hy_w1, hy_b1, hy_w2, hy_b2, hy_w3, hy_freq, hy_skip, lru_conv_w, lru_conv_b, lru_wa, lru_ba, lru_wx, lru_bx, lru_lambda, att_lambda, att_subln, w_out, ln_g, ln_b, router_w, router_b, exp_w_gate, exp_w_up, exp_w_down, sh_w_gate, sh_w_up, sh_w_down):
    bsz, n_lat, d = x.shape
    n_ctx = ctx.shape[1]
    n_all = n_ctx + n_lat
    x_all = jnp.concatenate([ctx, x], axis=1).reshape(bsz * n_all, d)
    both = ((n_ctx, 'ctx'), (n_lat, 'lat'))
    seg_all = _segments(bsz, both, ROW_TILE)
    x_out = None

    for l in range(DEPTH):
        need_ctx = l < DEPTH - 1
        lam_init = 0.8 - 0.6 * math.exp(-0.3 * l)
        cond = jnp.concatenate([c, c_ctx[None, :]], axis=0)
        mod = _ada_modulation(cond, ada_w, ada_b, l)
        sh1, sc1, g1, sh2, sc2, g2 = jnp.split(mod, 6, axis=-1)
        hy_args = (hy_conv_w[l], hy_conv_b[l], hy_w1[l], hy_b1[l], hy_w2[l], hy_b2[l],
                   hy_w3[l], hy_freq[l], hy_skip[l])

        u_all = _ln_mod_matmul(x_all, jnp.stack([sh1, sc1], axis=1), seg_all,
                               w_in[l].astype(BF16)).reshape(bsz, n_all, -1)
        y_hy_l = _hyena_mixer(u_all[:, n_ctx:, :HY_IN], *hy_args)
        y_lru = _rglru_mixer(u_all, n_ctx, lru_conv_w[l], lru_conv_b[l], lru_wa[l], lru_ba[l],
                             lru_wx[l], lru_bx[l], lru_lambda[l])
        y_att_l, y_att_c = _diff_attention_mixer(u_all[..., HY_IN + LRU_IN:], n_ctx, att_lambda[l],
                                                 att_subln[l], lam_init, need_ctx)
        if need_ctx:
            y_hy = jnp.concatenate([_hyena_mixer(u_all[:, :n_ctx, :HY_IN], *hy_args), y_hy_l], axis=1)
            mixed = jnp.concatenate([y_hy.astype(BF16), y_lru, jnp.concatenate([y_att_c, y_att_l], axis=1)], axis=-1)
            x_res, seg, seg_c = x_all, seg_all, _segments(bsz, both, COMBINE_ROW_TILE)
        else:
            mixed = jnp.concatenate([y_hy_l.astype(BF16), y_lru[:, n_ctx:], y_att_l], axis=-1)
            x_res = x_all.reshape(bsz, n_all, d)[:, n_ctx:].reshape(bsz * n_lat, d)
            seg = _segments(bsz, ((n_lat, 'lat'),), ROW_TILE)
            seg_c = _segments(bsz, ((n_lat, 'lat'),), COMBINE_ROW_TILE)
        n_tok = x_res.shape[0]
        ln_gb1 = jnp.stack([ln_g[l, 0], ln_b[l, 0]])
        ln_gb2 = jnp.stack([ln_g[l, 1], ln_b[l, 1]])
        x1, h2, h2b, top_idx, top_w = _out_proj_norm_route(
            mixed.reshape(n_tok, d), x_res, jnp.stack([g1, sh2, sc2], axis=1), seg,
            w_out[l].astype(BF16), ln_gb1, router_w[l], router_b[l])

        idx, row_w, block_e, n_used, next_e = _route(top_idx, top_w)
        y = _routed_experts(h2, idx, row_w, block_e, n_used, next_e, exp_w_gate, exp_w_up, exp_w_down, l)
        shared = _shared_expert(h2b, sh_w_gate[l], sh_w_up[l], sh_w_down[l])
        x_out = _combine_norm(y, shared, x1, g2[:, None, :], seg_c, ln_gb2)
        x_all = x_out
    return x_out.reshape(bsz, n_lat, d)
```
